```python
import jax
import jax.numpy as jnp
from jax import lax
import numpy as np

D_MODEL = 1024
BATCH = 8
SEQ = 2048
DEPTH = 2

D_MIX = D_MODEL
N_MIXERS = 4
GROUP_W = D_MIX // N_MIXERS
HEAD_DIM = 64
GROUP_HEADS = GROUP_W // HEAD_DIM
D_FF = 256 * ((8 * D_MODEL // 3 + 255) // 256)
ALPHA = (2.0 * DEPTH) ** 0.25
BETA = (8.0 * DEPTH) ** -0.25
MLA_HEADS = GROUP_HEADS
MLA_NOPE = HEAD_DIM
MLA_ROPE = HEAD_DIM // 2
MLA_V = GROUP_W // MLA_HEADS
Q_LORA = D_MODEL // 4
KV_LORA = D_MODEL // 8
ROPE_THETA = 10000.0
Q_BLOCK = 128
LRU_HEADS = GROUP_HEADS
LRU_BLOCK = GROUP_W // LRU_HEADS
CONV_W = 4
LRU_C = 8.0
SWA_HEADS = GROUP_HEADS
DILATED_PATTERNS = ((128, 1), (512, 4), (2048, 16))
ML_HEADS = GROUP_HEADS
ML_CHUNK = 64
IN_SIZES = (Q_LORA, KV_LORA, MLA_ROPE, GROUP_W, GROUP_W, GROUP_W, GROUP_W, GROUP_W, GROUP_W, GROUP_W, GROUP_W, GROUP_W, ML_HEADS, ML_HEADS)
N_IN = sum(IN_SIZES)

kernel_name = 'hybrid_parallel_mla_rglru_dilated_mlstm_macaron'


def _layernorm(x, g, b, eps=1e-5):
    xf = x.astype(jnp.float32)
    mu = jnp.mean(xf, axis=-1, keepdims=True)
    var = jnp.mean(jnp.square(xf - mu), axis=-1, keepdims=True)
    return ((xf - mu) * lax.rsqrt(var + eps) * g + b).astype(x.dtype)


def _rmsnorm(x, g, eps=1e-6):
    xf = x.astype(jnp.float32)
    return (xf * lax.rsqrt(jnp.mean(xf * xf, axis=-1, keepdims=True) + eps) * g).astype(x.dtype)


def _group_rmsnorm(y, g, eps=1e-6):
    B, S, _ = y.shape
    yf = y.astype(jnp.float32).reshape(B, S, N_MIXERS, GROUP_W)
    yf = yf * lax.rsqrt(jnp.mean(yf * yf, axis=-1, keepdims=True) + eps)
    return (yf.reshape(B, S, D_MIX) * g).astype(y.dtype)


def _swiglu(x, w1, w3, w2):
    return (jax.nn.silu(x @ w1) * (x @ w3)) @ w2


def _split_columns(z):
    parts, start = [], 0
    for size in IN_SIZES:
        parts.append(z[..., start:start + size])
        start += size
    return parts


def _rope(x, pos):
    half = x.shape[-1] // 2
    freqs = ROPE_THETA ** (-jnp.arange(half, dtype=jnp.float32) / half)
    ang = pos.astype(jnp.float32)[:, None] * freqs[None, :]
    cos, sin = jnp.cos(ang)[:, None, :], jnp.sin(ang)[:, None, :]
    x1, x2 = x[..., :half].astype(jnp.float32), x[..., half:].astype(jnp.float32)
    return jnp.concatenate([x1 * cos - x2 * sin, x1 * sin + x2 * cos], axis=-1).astype(x.dtype)


def _causal_block_attention(q, k, v, scale):
    B, S, H, E = q.shape
    nb = S // Q_BLOCK
    qb = jnp.moveaxis(q.reshape(B, nb, Q_BLOCK, H, E), 1, 0)
    kpos = jnp.arange(S)

    def one_block(args):
        qi, i = args
        s = jnp.einsum('bqhe,bkhe->bhqk', qi, k, preferred_element_type=jnp.float32) * scale
        qpos = i * Q_BLOCK + jnp.arange(Q_BLOCK)
        s = jnp.where(kpos[None, :] <= qpos[:, None], s, -jnp.inf)
        p = jax.nn.softmax(s, axis=-1)
        return jnp.einsum('bhqk,bkhe->bqhe', p.astype(v.dtype), v)

    out = lax.map(one_block, (qb, jnp.arange(nb)))
    return jnp.moveaxis(out, 0, 1).reshape(B, S, H, v.shape[-1])


def _mla(c_q, c_kv, k_rope, q_norm, kv_norm, w_uq, w_ukv):
    B, S, _ = c_q.shape
    pos = jnp.arange(S)
    q = (_rmsnorm(c_q, q_norm) @ w_uq).reshape(B, S, MLA_HEADS, MLA_NOPE + MLA_ROPE)
    kv = (_rmsnorm(c_kv, kv_norm) @ w_ukv).reshape(B, S, MLA_HEADS, MLA_NOPE + MLA_V)
    q = jnp.concatenate([q[..., :MLA_NOPE], _rope(q[..., MLA_NOPE:], pos)], axis=-1)
    k_r = jnp.broadcast_to(_rope(k_rope[:, :, None, :], pos), (B, S, MLA_HEADS, MLA_ROPE))
    k = jnp.concatenate([kv[..., :MLA_NOPE], k_r], axis=-1)
    v = kv[..., MLA_NOPE:]
    o = _causal_block_attention(q, k, v, (MLA_NOPE + MLA_ROPE) ** -0.5)
    return o.reshape(B, S, MLA_HEADS * MLA_V)


def _rglru(xb, gate, conv_w, conv_b, w_a, b_a, w_x, b_x, lam):
    B, S, W = xb.shape
    xc = lax.conv_general_dilated(xb, conv_w[:, None, :], window_strides=(1,), padding=((CONV_W - 1, 0),),
                                  dimension_numbers=('NWC', 'WIO', 'NWC'), feature_group_count=W) + conv_b
    xh = xc.reshape(B, S, LRU_HEADS, LRU_BLOCK)
    r = jax.nn.sigmoid(jnp.einsum('bshi,hij->bshj', xh, w_a).reshape(B, S, W) + b_a)
    i = jax.nn.sigmoid(jnp.einsum('bshi,hij->bshj', xh, w_x).reshape(B, S, W) + b_x)
    log_a = -LRU_C * r.astype(jnp.float32) * jax.nn.softplus(-lam.astype(jnp.float32))
    a = jnp.exp(log_a)
    u = jnp.sqrt(-jnp.expm1(2.0 * log_a)) * (i * xc).astype(jnp.float32)

    def combine(left, right):
        a1, b1 = left
        a2, b2 = right
        return a1 * a2, a2 * b1 + b2

    _, h = lax.associative_scan(combine, (a, u), axis=1)
    return (h * jax.nn.gelu(gate.astype(jnp.float32))).astype(xb.dtype)


def _band_attention(q, k, v, back):
    N, L, H, E = q.shape
    blk = back
    nb = -(-L // blk)
    pad = nb * blk - L
    padf = lambda t: jnp.pad(t, ((0, 0), (0, pad), (0, 0), (0, 0))).reshape(N, nb, blk, H, E)
    qb, kb, vb = padf(q), padf(k), padf(v)

    def with_prev(t):
        prev = jnp.concatenate([jnp.zeros_like(t[:, :1]), t[:, :-1]], axis=1)
        return jnp.concatenate([prev, t], axis=2)

    kk, vv = with_prev(kb), with_prev(vb)
    s = jnp.einsum('nbqhe,nbkhe->nbhqk', qb, kk, preferred_element_type=jnp.float32) * (E ** -0.5)
    a_idx = jnp.arange(blk)[:, None]
    c_idx = jnp.arange(2 * blk)[None, :]
    rel = a_idx + blk - c_idx
    blk_id = jnp.arange(nb)[:, None, None]
    mask = (rel >= 0) & (rel <= back) & ((blk_id > 0) | (c_idx >= blk))
    s = jnp.where(mask[None, :, None], s, -jnp.inf)
    m = jnp.max(s, axis=-1, keepdims=True)
    p = jnp.exp(s - m)
    l = jnp.sum(p, axis=-1, keepdims=True)
    o = jnp.einsum('nbhqk,nbkhe->nbqhe', p / l, vv.astype(jnp.float32))
    lse = (m + jnp.log(l))[..., 0]
    o = o.reshape(N, nb * blk, H, E)[:, :L]
    lse = lse.transpose(0, 1, 3, 2).reshape(N, nb * blk, H)[:, :L]
    return o, lse


def _dilated_attention(q, k, v):
    B, S, H, E = q.shape
    outs, lses = [], []
    for window, dil in DILATED_PATTERNS:
        L = S // dil
        fold = lambda t: t.reshape(B, L, dil, H, E).transpose(0, 2, 1, 3, 4).reshape(B * dil, L, H, E)
        o, lse = _band_attention(fold(q), fold(k), fold(v), window // dil)
        outs.append(o.reshape(B, dil, L, H, E).transpose(0, 2, 1, 3, 4).reshape(B, S, H, E))
        lses.append(lse.reshape(B, dil, L, H).transpose(0, 2, 1, 3).reshape(B, S, H))
    wts = jax.nn.softmax(jnp.stack(lses), axis=0)
    return jnp.einsum('pbsh,pbshe->bshe', wts, jnp.stack(outs)).astype(q.dtype)


def _mlstm_chunkwise(q, k, v, log_i, log_f):
    B, S, H, E = q.shape
    nc = S // ML_CHUNK

    def chunks(t):
        return jnp.moveaxis(t.reshape((B, nc, ML_CHUNK) + t.shape[2:]), 1, 0)

    tri = jnp.tril(jnp.ones((ML_CHUNK, ML_CHUNK), dtype=bool))[None, :, :, None]

    def step(carry, inp):
        C, n, m = carry
        qc, kc, vc, ic, fc = inp
        b = jnp.cumsum(fc, axis=1)
        g = b[:, -1]
        D = jnp.where(tri, b[:, :, None, :] - b[:, None, :, :] + ic[:, None, :, :], -jnp.inf)
        inter = b + m[:, None, :]
        mt = jnp.maximum(jnp.max(D, axis=2), inter)
        s = jnp.einsum('bthe,bshe->btsh', qc, kc) * jnp.exp(D - mt[:, :, None, :])
        w_inter = jnp.exp(inter - mt)
        num = jnp.einsum('btsh,bshe->bthe', s, vc) + w_inter[..., None] * jnp.einsum('bthe,bhef->bthf', qc, C)
        den = jnp.sum(s, axis=2) + w_inter * jnp.einsum('bthe,bhe->bth', qc, n)
        h = num / jnp.maximum(jnp.abs(den), jnp.exp(-mt))[..., None]
        wk = g[:, None, :] - b + ic
        m_new = jnp.maximum(g + m, jnp.max(wk, axis=1))
        decay = jnp.exp(g + m - m_new)
        wk = jnp.exp(wk - m_new[:, None, :])
        C_new = decay[..., None, None] * C + jnp.einsum('bsh,bshe,bshf->bhef', wk, kc, vc)
        n_new = decay[..., None] * n + jnp.einsum('bsh,bshe->bhe', wk, kc)
        return (C_new, n_new, m_new), h

    init = (jnp.zeros((B, H, E, E), jnp.float32), jnp.zeros((B, H, E), jnp.float32), jnp.zeros((B, H), jnp.float32))
    _, hs = lax.scan(step, init, (chunks(q), chunks(k), chunks(v), chunks(log_i), chunks(log_f)))
    return jnp.moveaxis(hs, 0, 1).reshape(B, S, H, E)


def _mlstm(q, k, v, o_gate, ig, fg, b_i, b_f):
    B, S, _ = q.shape
    heads = lambda t: t.reshape(B, S, ML_HEADS, HEAD_DIM).astype(jnp.float32)
    log_i = ig.astype(jnp.float32) + b_i
    log_f = jax.nn.log_sigmoid(fg.astype(jnp.float32) + b_f)
    h = _mlstm_chunkwise(heads(q), heads(k) * (HEAD_DIM ** -0.5), heads(v), log_i, log_f)
    return (jax.nn.sigmoid(o_gate.astype(jnp.float32)) * h.reshape(B, S, GROUP_W)).astype(q.dtype)


def _token_mixing(x, w_in, mla_q_norm, mla_kv_norm, mla_w_uq, mla_w_ukv, lru_conv_w, lru_conv_b, lru_w_a,
                  lru_b_a, lru_w_x, lru_b_x, lru_lambda, ml_b_i, ml_b_f, out_norm, w_out):
    B, S, _ = x.shape
    (c_q, c_kv, k_rope, lru_x, lru_gate, sw_q, sw_k, sw_v,
     ml_q, ml_k, ml_v, ml_o, ml_i, ml_f) = _split_columns(x @ w_in)
    heads = lambda t: t.reshape(B, S, SWA_HEADS, HEAD_DIM)
    y_a = _mla(c_q, c_kv, k_rope, mla_q_norm, mla_kv_norm, mla_w_uq, mla_w_ukv)
    y_b = _rglru(lru_x, lru_gate, lru_conv_w, lru_conv_b, lru_w_a, lru_b_a, lru_w_x, lru_b_x, lru_lambda)
    y_c = _dilated_attention(heads(sw_q), heads(sw_k), heads(sw_v)).reshape(B, S, GROUP_W)
    y_d = _mlstm(ml_q, ml_k, ml_v, ml_o, ml_i, ml_f, ml_b_i, ml_b_f)
    y = _group_rmsnorm(jnp.concatenate([y_a, y_b, y_c, y_d], axis=-1), out_norm)
    return y @ w_out


def setup_inputs(seed: int = 0) -> dict:
    key = jax.random.key(seed)
    ks = jax.random.split(key, 24)
    f32 = jnp.float32
    nrm = lambda k, shape, scale: jax.random.normal(k, shape, f32) * scale
    x = nrm(ks[0], (BATCH, SEQ, D_MODEL), 1.0)
    ln_g = 1.0 + nrm(ks[1], (DEPTH, 3, D_MODEL), 0.02)
    ln_b = nrm(ks[2], (DEPTH, 3, D_MODEL), 0.02)
    ffn_w1 = nrm(ks[3], (DEPTH, 2, D_MODEL, D_FF), D_MODEL ** -0.5)
    ffn_w3 = nrm(ks[4], (DEPTH, 2, D_MODEL, D_FF), D_MODEL ** -0.5)
    ffn_w2 = nrm(ks[5], (DEPTH, 2, D_FF, D_MODEL), D_FF ** -0.5 * BETA)
    w_in = nrm(ks[6], (DEPTH, D_MODEL, N_IN), D_MODEL ** -0.5)
    mla_q_norm = 1.0 + nrm(ks[7], (DEPTH, Q_LORA), 0.02)
    mla_kv_norm = 1.0 + nrm(ks[8], (DEPTH, KV_LORA), 0.02)
    mla_w_uq = nrm(ks[9], (DEPTH, Q_LORA, MLA_HEADS * (MLA_NOPE + MLA_ROPE)), Q_LORA ** -0.5)
    mla_w_ukv = nrm(ks[10], (DEPTH, KV_LORA, MLA_HEADS * (MLA_NOPE + MLA_V)), KV_LORA ** -0.5)
    lru_conv_w = nrm(ks[11], (DEPTH, CONV_W, GROUP_W), CONV_W ** -0.5)
    lru_conv_b = nrm(ks[12], (DEPTH, GROUP_W), 0.02)
    lru_w_a = nrm(ks[13], (DEPTH, LRU_HEADS, LRU_BLOCK, LRU_BLOCK), LRU_BLOCK ** -0.5)
    lru_b_a = nrm(ks[14], (DEPTH, GROUP_W), 0.02)
    lru_w_x = nrm(ks[15], (DEPTH, LRU_HEADS, LRU_BLOCK, LRU_BLOCK), LRU_BLOCK ** -0.5)
    lru_b_x = nrm(ks[16], (DEPTH, GROUP_W), 0.02)
    a0 = jax.random.uniform(ks[17], (DEPTH, GROUP_W), f32, 0.9, 0.999)
    s0 = a0 ** (1.0 / LRU_C)
    lru_lambda = jnp.log(s0) - jnp.log1p(-s0)
    ml_b_i = nrm(ks[18], (DEPTH, ML_HEADS), 0.1)
    ml_b_f = jnp.linspace(3.0, 6.0, ML_HEADS, dtype=f32)[None, :] + nrm(ks[19], (DEPTH, ML_HEADS), 0.02)
    out_norm = 1.0 + nrm(ks[20], (DEPTH, D_MIX), 0.02)
    w_out = nrm(ks[21], (DEPTH, D_MIX, D_MODEL), D_MIX ** -0.5 * BETA)
    return {'x': x, 'ln_g': ln_g, 'ln_b': ln_b, 'ffn_w1': ffn_w1, 'ffn_w3': ffn_w3, 'ffn_w2': ffn_w2,
            'w_in': w_in, 'mla_q_norm': mla_q_norm, 'mla_kv_norm': mla_kv_norm, 'mla_w_uq': mla_w_uq,
            'mla_w_ukv': mla_w_ukv, 'lru_conv_w': lru_conv_w, 'lru_conv_b': lru_conv_b, 'lru_w_a': lru_w_a,
            'lru_b_a': lru_b_a, 'lru_w_x': lru_w_x, 'lru_b_x': lru_b_x, 'lru_lambda': lru_lambda,
            'ml_b_i': ml_b_i, 'ml_b_f': ml_b_f, 'out_norm': out_norm, 'w_out': w_out}


def reference(x, ln_g, ln_b, ffn_w1, ffn_w3, ffn_w2, w_in, mla_q_norm, mla_kv_norm, mla_w_uq, mla_w_ukv,
              lru_conv_w, lru_conv_b, lru_w_a, lru_b_a, lru_w_x, lru_b_x, lru_lambda, ml_b_i, ml_b_f,
              out_norm, w_out):
    for l in range(DEPTH):
        x = _layernorm(ALPHA * x + 0.5 * _swiglu(x, ffn_w1[l, 0], ffn_w3[l, 0], ffn_w2[l, 0]), ln_g[l, 0], ln_b[l, 0])
        y = _token_mixing(x, w_in[l], mla_q_norm[l], mla_kv_norm[l], mla_w_uq[l], mla_w_ukv[l],
                          lru_conv_w[l], lru_conv_b[l], lru_w_a[l], lru_b_a[l], lru_w_x[l], lru_b_x[l],
                          lru_lambda[l], ml_b_i[l], ml_b_f[l], out_norm[l], w_out[l])
        x = _layernorm(ALPHA * x + y, ln_g[l, 1], ln_b[l, 1])
        x = _layernorm(ALPHA * x + 0.5 * _swiglu(x, ffn_w1[l, 1], ffn_w3[l, 1], ffn_w2[l, 1]), ln_g[l, 2], ln_b[l, 2])
    return x
```

```python
import functools

import numpy as np
import jax
import jax.numpy as jnp
from jax import lax
from jax.experimental import pallas as pl
from jax.experimental.pallas import tpu as pltpu

F32 = jnp.float32
BF16 = jnp.bfloat16

D_MODEL = 1024
DEPTH = 2
GROUP_W = 256
HEAD_DIM = 64
HEADS = 4
D_FF = 2816
ALPHA = (2.0 * DEPTH) ** 0.25
Q_LORA = 256
KV_LORA = 128
MLA_NOPE = 64
MLA_ROPE = 32
ROPE_HALF = MLA_ROPE // 2
ROPE_THETA = 10000.0
CONV_W = 4
LRU_C = 8.0
DILATED_PATTERNS = ((128, 1), (512, 4), (2048, 16))
BAND = 128

V7X_LANES = 128
V7X_VMEM_BYTES = 64 * 1024 * 1024

FF_CHUNK = 256
N_FF_CHUNKS = D_FF // FF_CHUNK
TOKEN_TILE = 512
ML_CHUNK = 128
ATT_TQ = 128
ATT_TK = 256

Z_A = 512
Z_LRU = 512
Z_ML = 1024
Z_G = 128
N_Z = Z_A + Z_LRU + 3 * GROUP_W + Z_ML + Z_G

NEG_INF = float("-inf")


def _vmem_limit(nbytes):
    return int(min(nbytes, V7X_VMEM_BYTES - 6 * 1024 * 1024))


def _const_spec(shape):
    nd = len(shape)
    return pl.BlockSpec(shape, lambda *_: (0,) * nd, pipeline_mode=pl.Buffered(1))


def _layernorm(v, g, b):
    mu = jnp.mean(v, axis=-1, keepdims=True)
    d = v - mu
    var = jnp.mean(d * d, axis=-1, keepdims=True)
    return d * lax.rsqrt(var + 1e-5) * g + b


def _rmsnorm(v, g, eps=1e-6):
    return v * lax.rsqrt(jnp.mean(v * v, axis=-1, keepdims=True) + eps) * g


def _dot(a, b):
    return jnp.dot(a, b, preferred_element_type=F32)


def _dot_nt(a, b):
    return lax.dot_general(a, b, (((1,), (1,)), ((), ())), preferred_element_type=F32)


def _head_of_lane(width, group=HEAD_DIM):
    return lax.broadcasted_iota(jnp.int32, (1, width), 1) // group


def _ffn_ln(x, w1_ref, w3_ref, w2_ref, acc_ref, g, b):
    xb = x.astype(BF16)
    acc_ref[...] = jnp.zeros_like(acc_ref)

    def body(c, carry):
        h1 = _dot(xb, w1_ref[c])
        h3 = _dot(xb, w3_ref[c])
        h = (h1 * jax.nn.sigmoid(h1)) * h3
        acc_ref[...] += _dot(h.astype(BF16), w2_ref[c])
        return carry

    lax.fori_loop(0, N_FF_CHUNKS, body, 0)
    return _layernorm(ALPHA * x + 0.5 * acc_ref[...], g, b)


def _ffn_inproj_kernel(x_ref, w1_ref, w3_ref, w2_ref, g_ref, b_ref, win_ref,
                       x1_ref, za_ref, zlru_ref, sq_ref, sk_ref, sv_ref, zml_ref, zg_ref, acc_ref):
    x1 = _ffn_ln(x_ref[...], w1_ref, w3_ref, w2_ref, acc_ref, g_ref[...], b_ref[...])
    x1_ref[...] = x1
    xb = x1.astype(BF16)
    off = 0
    for ref in (za_ref, zlru_ref, sq_ref, sk_ref, sv_ref, zml_ref, zg_ref):
        w = ref.shape[-1]
        ref[...] = _dot(xb, win_ref[:, off:off + w])
        off += w


def _ffn_inproj(x, w1, w3, w2, g, b, win):
    n = x.shape[0]
    tm = TOKEN_TILE
    tile = lambda w: pl.BlockSpec((tm, w), lambda i: (i, 0))
    widths = (D_MODEL, Z_A, Z_LRU, GROUP_W, GROUP_W, GROUP_W, Z_ML, Z_G)
    return pl.pallas_call(
        _ffn_inproj_kernel,
        grid=(n // tm,),
        in_specs=[tile(D_MODEL), _const_spec(w1.shape), _const_spec(w3.shape), _const_spec(w2.shape),
                  _const_spec(g.shape), _const_spec(b.shape), _const_spec(win.shape)],
        out_specs=[tile(w) for w in widths],
        out_shape=[jax.ShapeDtypeStruct((n, w), F32) for w in widths],
        scratch_shapes=[pltpu.VMEM((tm, D_MODEL), F32)],
        compiler_params=pltpu.CompilerParams(dimension_semantics=("arbitrary",),
                                             vmem_limit_bytes=_vmem_limit(56 * 1024 * 1024)),
        name="ffn_inproj",
    )(x, w1, w3, w2, g, b, win)


def _outproj_ffn_kernel(x_ref, ya_ref, yb_ref, o1_ref, o4_ref, o16_ref, l1_ref, l4_ref, l16_ref, yd_ref,
                        on_ref, wout_ref, g1_ref, b1_ref, w1_ref, w3_ref, w2_ref, g2_ref, b2_ref,
                        out_ref, acc_ref):
    l1, l4, l16 = l1_ref[...], l4_ref[...], l16_ref[...]
    mx = jnp.maximum(jnp.maximum(l1, l4), l16)
    e1, e4, e16 = jnp.exp(l1 - mx), jnp.exp(l4 - mx), jnp.exp(l16 - mx)
    yc = (e1 * o1_ref[...] + e4 * o4_ref[...] + e16 * o16_ref[...]) / (e1 + e4 + e16)
    y = jnp.zeros(x_ref.shape, F32)
    for gi, yg in enumerate((ya_ref[...], yb_ref[...], yc, yd_ref[...])):
        lo = gi * GROUP_W
        yn = _rmsnorm(yg, on_ref[:, lo:lo + GROUP_W])
        y = y + _dot(yn.astype(BF16), wout_ref[lo:lo + GROUP_W, :])
    x2 = _layernorm(ALPHA * x_ref[...] + y, g1_ref[...], b1_ref[...])
    out_ref[...] = _ffn_ln(x2, w1_ref, w3_ref, w2_ref, acc_ref, g2_ref[...], b2_ref[...])


def _outproj_ffn(x, ys, on, wout, g1, b1, w1, w3, w2, g2, b2):
    n = x.shape[0]
    tm = TOKEN_TILE
    tile = lambda w: pl.BlockSpec((tm, w), lambda i: (i, 0))
    consts = (on, wout, g1, b1, w1, w3, w2, g2, b2)
    return pl.pallas_call(
        _outproj_ffn_kernel,
        grid=(n // tm,),
        in_specs=[tile(D_MODEL)] + [tile(GROUP_W)] * len(ys) + [_const_spec(c.shape) for c in consts],
        out_specs=tile(D_MODEL),
        out_shape=jax.ShapeDtypeStruct((n, D_MODEL), F32),
        scratch_shapes=[pltpu.VMEM((tm, D_MODEL), F32)],
        compiler_params=pltpu.CompilerParams(dimension_semantics=("arbitrary",),
                                             vmem_limit_bytes=_vmem_limit(56 * 1024 * 1024)),
        name="outproj_ffn",
    )(x, *ys, *consts)


def _mla_kernel(za_ref, qn_ref, kvn_ref, wuq_ref, wukv_ref, cos_ref, sin_ref, o_ref,
                q_scr, k_scr, v_scr):
    seq = za_ref.shape[0]
    cos, sin = cos_ref[...], sin_ref[...]

    def rope(r):
        return r * cos + pltpu.roll(r, V7X_LANES // 2, axis=1) * sin

    cq = _rmsnorm(za_ref[:, 0:Q_LORA], qn_ref[...])
    q = _dot(cq.astype(BF16), wuq_ref[...])
    q_scr[:, 0:GROUP_W] = q[:, 0:GROUP_W].astype(BF16)
    q_scr[:, GROUP_W:] = rope(q[:, GROUP_W:]).astype(BF16)
    ckv = _rmsnorm(za_ref[:, Q_LORA:Q_LORA + KV_LORA], kvn_ref[...])
    kv = _dot(ckv.astype(BF16), wukv_ref[...])
    k_scr[:, 0:GROUP_W] = kv[:, 0:GROUP_W].astype(BF16)
    k_scr[:, GROUP_W:] = rope(za_ref[:, Q_LORA + KV_LORA:]).astype(BF16)
    v_scr[...] = kv[:, GROUP_W:].astype(BF16)

    scale = (MLA_NOPE + MLA_ROPE) ** -0.5
    qk_w = q_scr.shape[1]
    lane = lax.broadcasted_iota(jnp.int32, (1, qk_w), 1)
    q_head = jnp.where(lane < GROUP_W, lane // HEAD_DIM, (lane % (HEADS * ROPE_HALF)) // ROPE_HALF)
    v_head = _head_of_lane(GROUP_W)
    rows = HEADS * ATT_TQ
    q_in_blk = lax.broadcasted_iota(jnp.int32, (rows, ATT_TK), 0) % ATT_TQ
    k_in_blk = lax.broadcasted_iota(jnp.int32, (rows, ATT_TK), 1)

    def q_block(i, carry):
        q0 = pl.multiple_of(i * ATT_TQ, ATT_TQ)
        qb = q_scr[pl.ds(q0, ATT_TQ), :]
        qs = jnp.concatenate([jnp.where(q_head == h, qb, jnp.zeros_like(qb)) for h in range(HEADS)], axis=0)
        rel = q_in_blk + q0 - k_in_blk

        def kv_block(j, st):
            m, l, acc = st
            k0 = pl.multiple_of(j * ATT_TK, ATT_TK)
            s = _dot_nt(qs, k_scr[pl.ds(k0, ATT_TK), :]) * scale
            s = jnp.where(rel - k0 >= 0, s, NEG_INF)
            m_new = jnp.maximum(m, jnp.max(s, axis=-1, keepdims=True))
            a = jnp.exp(m - m_new)
            p = jnp.exp(s - m_new)
            l = a * l + jnp.sum(p, axis=-1, keepdims=True)
            acc = a * acc + _dot(p.astype(BF16), v_scr[pl.ds(k0, ATT_TK), :])
            return m_new, l, acc

        init = (jnp.full((rows, 1), NEG_INF, F32), jnp.zeros((rows, 1), F32), jnp.zeros((rows, GROUP_W), F32))
        n_kv = (q0 + ATT_TQ + ATT_TK - 1) // ATT_TK
        _, l, acc = lax.fori_loop(0, n_kv, kv_block, init)
        o = acc / l
        out = jnp.zeros((ATT_TQ, GROUP_W), F32)
        for h in range(HEADS):
            out = out + jnp.where(v_head == h, o[h * ATT_TQ:(h + 1) * ATT_TQ], 0.0)
        o_ref[pl.ds(q0, ATT_TQ), :] = out
        return carry

    lax.fori_loop(0, seq // ATT_TQ, q_block, 0)


def _mla(za, qn, kvn, wuq, wukv, cos, sin):
    b, s, _ = za.shape
    consts = (qn, kvn, wuq, wukv, cos, sin)
    return pl.pallas_call(
        _mla_kernel,
        grid=(b,),
        in_specs=[pl.BlockSpec((None, s, Z_A), lambda i: (i, 0, 0))] + [_const_spec(c.shape) for c in consts],
        out_specs=pl.BlockSpec((None, s, GROUP_W), lambda i: (i, 0, 0)),
        out_shape=jax.ShapeDtypeStruct((b, s, GROUP_W), F32),
        scratch_shapes=[pltpu.VMEM((s, GROUP_W + V7X_LANES), BF16), pltpu.VMEM((s, GROUP_W + V7X_LANES), BF16),
                        pltpu.VMEM((s, GROUP_W), BF16)],
        compiler_params=pltpu.CompilerParams(dimension_semantics=("arbitrary",),
                                             vmem_limit_bytes=_vmem_limit(48 * 1024 * 1024)),
        name="mla",
    )(za, *consts)


def _shift_rows(x, d, fill):
    row = lax.broadcasted_iota(jnp.int32, x.shape, 0)
    return jnp.where(row >= d, pltpu.roll(x, d, axis=0), fill)


def _softplus(x):
    return jnp.maximum(x, 0.0) + jnp.log1p(jnp.exp(-jnp.abs(x)))


def _gelu_tanh(x):
    c = np.sqrt(2.0 / np.pi).astype(np.float32)
    return 0.5 * x * (1.0 + jnp.tanh(c * (x + 0.044715 * (x * x * x))))


def _rglru_kernel(z_ref, cw_ref, cb_ref, wa_ref, ba_ref, wx_ref, bx_ref, lam_ref, o_ref):
    seq = z_ref.shape[0]
    x = z_ref[:, 0:GROUP_W]
    xc = cb_ref[...] + cw_ref[CONV_W - 1:CONV_W, :] * x
    for j in range(CONV_W - 1):
        xc = xc + cw_ref[j:j + 1, :] * _shift_rows(x, CONV_W - 1 - j, 0.0)
    xcb = xc.astype(BF16)
    r = jax.nn.sigmoid(_dot(xcb, wa_ref[...]) + ba_ref[...])
    i = jax.nn.sigmoid(_dot(xcb, wx_ref[...]) + bx_ref[...])
    log_a = -LRU_C * r * _softplus(-lam_ref[...])
    a = jnp.exp(log_a)
    u = jnp.sqrt(-jnp.tanh(log_a) * (a * a + 1.0)) * (i * xc)
    d = 1
    while d < seq:
        u = a * _shift_rows(u, d, 0.0) + u
        a = a * _shift_rows(a, d, 1.0)
        d *= 2
    o_ref[...] = u * _gelu_tanh(z_ref[:, GROUP_W:])


def _rglru(z, cw, cb, wa, ba, wx, bx, lam):
    b, s, _ = z.shape
    consts = (cw, cb, wa, ba, wx, bx, lam)
    return pl.pallas_call(
        _rglru_kernel,
        grid=(b,),
        in_specs=[pl.BlockSpec((None, s, Z_LRU), lambda i: (i, 0, 0))] + [_const_spec(c.shape) for c in consts],
        out_specs=pl.BlockSpec((None, s, GROUP_W), lambda i: (i, 0, 0)),
        out_shape=jax.ShapeDtypeStruct((b, s, GROUP_W), F32),
        compiler_params=pltpu.CompilerParams(dimension_semantics=("arbitrary",),
                                             vmem_limit_bytes=_vmem_limit(48 * 1024 * 1024)),
        name="rglru",
    )(z, *consts)


def _band_kernel(q_ref, k_ref, v_ref, o_ref, lse_ref):
    length = q_ref.shape[0]
    n_blk = length // BAND
    win = min(2 * BAND, length)
    scale = HEAD_DIM ** -0.5
    head = _head_of_lane(GROUP_W)
    rows = HEADS * BAND
    rel0 = (lax.broadcasted_iota(jnp.int32, (rows, win), 0) % BAND) - lax.broadcasted_iota(jnp.int32, (rows, win), 1)

    def block(jb, carry):
        q0 = pl.multiple_of(jb * BAND, BAND)
        w0 = pl.multiple_of(jnp.maximum(jb - 1, 0) * BAND, BAND)
        qb = q_ref[pl.ds(q0, BAND), :].astype(BF16)
        qs = jnp.concatenate([jnp.where(head == h, qb, jnp.zeros_like(qb)) for h in range(HEADS)], axis=0)
        s = _dot_nt(qs, k_ref[pl.ds(w0, win), :].astype(BF16)) * scale
        rel = rel0 + (q0 - w0)
        s = jnp.where((rel >= 0) & (rel <= BAND), s, NEG_INF)
        m = jnp.max(s, axis=-1, keepdims=True)
        p = jnp.exp(s - m)
        l = jnp.sum(p, axis=-1, keepdims=True)
        o = _dot(p.astype(BF16), v_ref[pl.ds(w0, win), :].astype(BF16)) / l
        lse = m + jnp.log(l)
        out = jnp.zeros((BAND, GROUP_W), F32)
        lse_out = jnp.zeros((BAND, GROUP_W), F32)
        for h in range(HEADS):
            sl = slice(h * BAND, (h + 1) * BAND)
            out = out + jnp.where(head == h, o[sl], 0.0)
            lse_out = lse_out + jnp.where(head == h, lse[sl], 0.0)
        o_ref[pl.ds(q0, BAND), :] = out
        lse_ref[pl.ds(q0, BAND), :] = lse_out
        return carry

    lax.fori_loop(0, n_blk, block, 0)


def _band_attention(q, k, v, dil):
    b, s, w = q.shape
    length = s // dil
    view = lambda t: t.reshape(b, length, dil * w)
    spec = pl.BlockSpec((None, length, w), lambda i, r: (i, 0, r))
    o, lse = pl.pallas_call(
        _band_kernel,
        grid=(b, dil),
        in_specs=[spec, spec, spec],
        out_specs=[spec, spec],
        out_shape=[jax.ShapeDtypeStruct((b, length, dil * w), F32)] * 2,
        compiler_params=pltpu.CompilerParams(dimension_semantics=("arbitrary", "arbitrary")),
        name=f"band_d{dil}",
    )(view(q), view(k), view(v))
    return o.reshape(b, s, w), lse.reshape(b, s, w)


def _expand_heads(cols, head):
    out = jnp.zeros((cols.shape[0], GROUP_W), F32)
    for h in range(HEADS):
        out = out + jnp.where(head == h, cols[:, h:h + 1], 0.0)
    return out


def _mlstm_kernel(z_ref, g_ref, bi_ref, bf_ref, o_ref, c_scr, n_scr, m_scr):
    seq = z_ref.shape[0]
    lc = ML_CHUNK
    head = _head_of_lane(GROUP_W)
    row_head = lax.broadcasted_iota(jnp.int32, (GROUP_W, GROUP_W), 0) // HEAD_DIM
    same_head = row_head == head
    tri = lax.broadcasted_iota(jnp.int32, (lc, lc), 0) >= lax.broadcasted_iota(jnp.int32, (lc, lc), 1)
    c_scr[...] = jnp.zeros_like(c_scr)
    n_scr[...] = jnp.zeros_like(n_scr)
    m_scr[...] = jnp.zeros_like(m_scr)

    def chunk(ci, carry):
        t0 = pl.multiple_of(ci * lc, lc)
        rows = pl.ds(t0, lc)
        q = z_ref[rows, 0:GROUP_W]
        k = z_ref[rows, GROUP_W:2 * GROUP_W] * (HEAD_DIM ** -0.5)
        v = z_ref[rows, 2 * GROUP_W:3 * GROUP_W]
        og = z_ref[rows, 3 * GROUP_W:4 * GROUP_W]
        gates = g_ref[rows, :]
        log_i = _expand_heads(gates[:, 0:HEADS], head) + bi_ref[...]
        f_pre = _expand_heads(gates[:, HEADS:2 * HEADS], head) + bf_ref[...]
        log_f = jnp.minimum(f_pre, 0.0) - jnp.log1p(jnp.exp(-jnp.abs(f_pre)))
        bcum = log_f
        d = 1
        while d < lc:
            bcum = bcum + _shift_rows(bcum, d, 0.0)
            d *= 2
        g_tot = bcum[lc - 1:lc, :]
        m_prev = m_scr[...]
        c_prev = c_scr[...]
        n_prev = n_scr[...]
        qb, kb, vb = q.astype(BF16), k.astype(BF16), v.astype(BF16)
        qs = jnp.concatenate([jnp.where(head == h, qb, jnp.zeros_like(qb)) for h in range(HEADS)], axis=0)
        s_all = _dot_nt(qs, kb)
        q_c = _dot(qb, c_prev.astype(BF16))
        qn = q * n_prev
        key_row = jnp.transpose(log_i - bcum)
        inter = bcum + m_prev
        out = jnp.zeros((lc, GROUP_W), F32)
        for h in range(HEADS):
            lane0 = h * HEAD_DIM
            dmat = jnp.where(tri, bcum[:, lane0:lane0 + 1] + key_row[lane0:lane0 + 1, :], NEG_INF)
            inter_h = inter[:, lane0:lane0 + 1]
            mt = jnp.maximum(jnp.max(dmat, axis=-1, keepdims=True), inter_h)
            sw = s_all[h * lc:(h + 1) * lc] * jnp.exp(dmat - mt)
            w_inter = jnp.exp(inter_h - mt)
            num = _dot(sw.astype(BF16), vb) + w_inter * q_c
            qn_h = jnp.sum(jnp.where(head == h, qn, 0.0), axis=-1, keepdims=True)
            den = jnp.sum(sw, axis=-1, keepdims=True) + w_inter * qn_h
            hh = num / jnp.maximum(jnp.abs(den), jnp.exp(-mt))
            out = out + jnp.where(head == h, hh, 0.0)
        o_ref[rows, :] = jax.nn.sigmoid(og) * out
        wk = g_tot - bcum + log_i
        m_new = jnp.maximum(g_tot + m_prev, jnp.max(wk, axis=0, keepdims=True))
        decay = jnp.exp(g_tot + m_prev - m_new)
        kw = k * jnp.exp(wk - m_new)
        kv = _dot(jnp.transpose(kw).astype(BF16), vb)
        c_scr[...] = decay * c_prev + jnp.where(same_head, kv, 0.0)
        n_scr[...] = decay * n_prev + jnp.sum(kw, axis=0, keepdims=True)
        m_scr[...] = m_new
        return carry

    lax.fori_loop(0, seq // lc, chunk, 0)


def _mlstm(z, gates, bi, bf):
    b, s, _ = z.shape
    return pl.pallas_call(
        _mlstm_kernel,
        grid=(b,),
        in_specs=[pl.BlockSpec((None, s, Z_ML), lambda i: (i, 0, 0)),
                  pl.BlockSpec((None, s, Z_G), lambda i: (i, 0, 0)),
                  _const_spec(bi.shape), _const_spec(bf.shape)],
        out_specs=pl.BlockSpec((None, s, GROUP_W), lambda i: (i, 0, 0)),
        out_shape=jax.ShapeDtypeStruct((b, s, GROUP_W), F32),
        scratch_shapes=[pltpu.VMEM((GROUP_W, GROUP_W), F32), pltpu.VMEM((1, GROUP_W), F32),
                        pltpu.VMEM((1, GROUP_W), F32)],
        compiler_params=pltpu.CompilerParams(dimension_semantics=("arbitrary",),
                                             vmem_limit_bytes=_vmem_limit(48 * 1024 * 1024)),
        name="mlstm",
    )(z, gates, bi, bf)


def _chunk_cols(w):
    return w.reshape(w.shape[0], N_FF_CHUNKS, FF_CHUNK).transpose(1, 0, 2).astype(BF16)


def _chunk_rows(w):
    return w.reshape(N_FF_CHUNKS, FF_CHUNK, w.shape[1]).astype(BF16)


def _block_diag(w):
    out = jnp.zeros((GROUP_W, GROUP_W), w.dtype)
    for h in range(HEADS):
        out = out.at[h * HEAD_DIM:(h + 1) * HEAD_DIM, h * HEAD_DIM:(h + 1) * HEAD_DIM].set(w[h])
    return out


def _permute_w_in(w):
    kr = Q_LORA + KV_LORA
    x1 = jnp.tile(w[:, kr:kr + ROPE_HALF], (1, HEADS))
    x2 = jnp.tile(w[:, kr + ROPE_HALF:kr + MLA_ROPE], (1, HEADS))
    body_lo = kr + MLA_ROPE
    body_hi = body_lo + 9 * GROUP_W
    pad = jnp.zeros((w.shape[0], Z_G - 2 * HEADS), w.dtype)
    return jnp.concatenate([w[:, :kr], x1, x2, w[:, body_lo:body_hi], w[:, body_hi:], pad], axis=1).astype(BF16)


def _mla_up_perms():
    per_q = MLA_NOPE + MLA_ROPE
    nope = [h * per_q + i for h in range(HEADS) for i in range(MLA_NOPE)]
    x1 = [h * per_q + MLA_NOPE + i for h in range(HEADS) for i in range(ROPE_HALF)]
    x2 = [h * per_q + MLA_NOPE + ROPE_HALF + i for h in range(HEADS) for i in range(ROPE_HALF)]
    per_kv = MLA_NOPE + HEAD_DIM
    kn = [h * per_kv + i for h in range(HEADS) for i in range(MLA_NOPE)]
    vv = [h * per_kv + MLA_NOPE + i for h in range(HEADS) for i in range(HEAD_DIM)]
    return np.array(nope + x1 + x2), np.array(kn + vv)


def _rope_tables(seq):
    freqs = ROPE_THETA ** (-jnp.arange(ROPE_HALF, dtype=F32) / ROPE_HALF)
    ang = jnp.arange(seq, dtype=F32)[:, None] * freqs[None, :]
    cos = jnp.tile(jnp.cos(ang), (1, 2 * HEADS))
    sin = jnp.tile(jnp.sin(ang), (1, HEADS))
    return cos, jnp.concatenate([-sin, sin], axis=1)


def kernel(x, ln_g, ln_b, ffn_w1, ffn_w3, ffn_w2, w_in, mla_q_norm, mla_kv_norm, mla_w_uq, mla_w_ukv,
           lru_conv_w, lru_conv_b, lru_w_a, lru_b_a, lru_w_x, lru_b_x, lru_lambda, ml_b_i, ml_b_f,
           out_norm, w_out):
    bsz, seq, d = x.shape
    n = bsz * seq
    row = lambda t: t.reshape(1, -1)
    q_perm, kv_perm = _mla_up_perms()
    cos, sin = _rope_tables(seq)
    xt = x.reshape(n, d)
    for l in range(DEPTH):
        ffn = lambda j: (_chunk_cols(ffn_w1[l, j]), _chunk_cols(ffn_w3[l, j]), _chunk_rows(ffn_w2[l, j]))
        x1, za, zlru, sq, sk, sv, zml, zg = _ffn_inproj(
            xt, *ffn(0), row(ln_g[l, 0]), row(ln_b[l, 0]), _permute_w_in(w_in[l]))
        seqv = lambda t: t.reshape(bsz, seq, t.shape[-1])
        y_a = _mla(seqv(za), row(mla_q_norm[l]), row(mla_kv_norm[l]),
                   mla_w_uq[l][:, q_perm].astype(BF16), mla_w_ukv[l][:, kv_perm].astype(BF16), cos, sin)
        y_b = _rglru(seqv(zlru), lru_conv_w[l], row(lru_conv_b[l]),
                     _block_diag(lru_w_a[l]).astype(BF16), row(lru_b_a[l]),
                     _block_diag(lru_w_x[l]).astype(BF16), row(lru_b_x[l]), row(lru_lambda[l]))
        band = [_band_attention(seqv(sq), seqv(sk), seqv(sv), dil) for _, dil in DILATED_PATTERNS]
        y_d = _mlstm(seqv(zml), seqv(zg), row(jnp.repeat(ml_b_i[l], HEAD_DIM)), row(jnp.repeat(ml_b_f[l], HEAD_DIM)))
        flat = lambda t: t.reshape(n, GROUP_W)
        ys = [flat(y_a), flat(y_b)] + [flat(o) for o, _ in band] + [flat(s) for _, s in band] + [flat(y_d)]
        xt = _outproj_ffn(x1, ys, row(out_norm[l]), w_out[l].astype(BF16), row(ln_g[l, 1]), row(ln_b[l, 1]),
                          *ffn(1), row(ln_g[l, 2]), row(ln_b[l, 2]))
    return xt.reshape(bsz, seq, d)
```

```python
import numpy as np
import jax
import jax.numpy as jnp
from jax import lax
from jax.experimental import pallas as pl
from jax.experimental.pallas import tpu as pltpu

F32 = jnp.float32
BF16 = jnp.bfloat16

D_MODEL = 1024
DEPTH = 2
GROUP_W = 256
HEAD_DIM = 64
HEADS = 4
D_FF = 2816
ALPHA = (2.0 * DEPTH) ** 0.25
Q_LORA = 256
KV_LORA = 128
MLA_NOPE = 64
MLA_ROPE = 32
ROPE_HALF = MLA_ROPE // 2
ROPE_THETA = 10000.0
CONV_W = 4
LRU_C = 8.0
DILATED_PATTERNS = ((128, 1), (512, 4), (2048, 16))

V7X_LANES = 128
V7X_VMEM_BYTES = 64 * 1024 * 1024

FF_CHUNK = 256
N_FF_CHUNKS = D_FF // FF_CHUNK
TOKEN_TILE = 512
ML_CHUNK = 128
ATT_T = 256
PAIR_W = 2 * HEAD_DIM
N_BIAS_TILES = 4

Z_A = 512
Z_LRU = 512
Z_ML = 1024
Z_G = 128

NEG_INF = float("-inf")
LOG2E = float(np.log2(np.e))


def _vmem_limit(nbytes):
    return int(min(nbytes, V7X_VMEM_BYTES - 6 * 1024 * 1024))


def _const_spec(shape):
    nd = len(shape)
    return pl.BlockSpec(shape, lambda *_: (0,) * nd, pipeline_mode=pl.Buffered(1))


def _layernorm(v, g, b):
    mu = jnp.mean(v, axis=-1, keepdims=True)
    d = v - mu
    var = jnp.mean(d * d, axis=-1, keepdims=True)
    return d * lax.rsqrt(var + 1e-5) * g + b


def _rmsnorm(v, g, eps=1e-6):
    return v * lax.rsqrt(jnp.mean(v * v, axis=-1, keepdims=True) + eps) * g


def _dot(a, b):
    return jnp.dot(a, b, preferred_element_type=F32)


def _dot_nt(a, b):
    return lax.dot_general(a, b, (((1,), (1,)), ((), ())), preferred_element_type=F32)


def _head_of_lane(width, group=HEAD_DIM):
    return lax.broadcasted_iota(jnp.int32, (1, width), 1) // group


def _ffn_ln(x, w1_ref, w3_ref, w2_ref, acc_ref, g, b):
    xb = x.astype(BF16)
    acc_ref[...] = jnp.zeros_like(acc_ref)

    def body(c, carry):
        h1 = _dot(xb, w1_ref[c])
        h3 = _dot(xb, w3_ref[c])
        h = (h1 * jax.nn.sigmoid(h1)) * h3
        acc_ref[...] += _dot(h.astype(BF16), w2_ref[c])
        return carry

    lax.fori_loop(0, N_FF_CHUNKS, body, 0)
    return _layernorm(ALPHA * x + 0.5 * acc_ref[...], g, b)


def _ffn_inproj_kernel(x_ref, w1_ref, w3_ref, w2_ref, g_ref, b_ref, win_ref,
                       x1_ref, za_ref, zlru_ref, sq_ref, sk_ref, sv_ref, zml_ref, zg_ref, acc_ref):
    x1 = _ffn_ln(x_ref[...], w1_ref, w3_ref, w2_ref, acc_ref, g_ref[...], b_ref[...])
    x1_ref[...] = x1
    xb = x1.astype(BF16)
    off = 0
    for ref in (za_ref, zlru_ref, sq_ref, sk_ref, sv_ref, zml_ref, zg_ref):
        w = ref.shape[-1]
        ref[...] = _dot(xb, win_ref[:, off:off + w])
        off += w


def _ffn_inproj(x, w1, w3, w2, g, b, win):
    n = x.shape[0]
    tm = TOKEN_TILE
    tile = lambda w: pl.BlockSpec((tm, w), lambda i: (i, 0))
    widths = (D_MODEL, Z_A, Z_LRU, GROUP_W, GROUP_W, GROUP_W, Z_ML, Z_G)
    return pl.pallas_call(
        _ffn_inproj_kernel,
        grid=(n // tm,),
        in_specs=[tile(D_MODEL), _const_spec(w1.shape), _const_spec(w3.shape), _const_spec(w2.shape),
                  _const_spec(g.shape), _const_spec(b.shape), _const_spec(win.shape)],
        out_specs=[tile(w) for w in widths],
        out_shape=[jax.ShapeDtypeStruct((n, w), F32) for w in widths],
        scratch_shapes=[pltpu.VMEM((tm, D_MODEL), F32)],
        compiler_params=pltpu.CompilerParams(dimension_semantics=("arbitrary",),
                                             vmem_limit_bytes=_vmem_limit(56 * 1024 * 1024)),
        name="ffn_inproj",
    )(x, w1, w3, w2, g, b, win)


def _outproj_ffn_kernel(x_ref, ya_ref, yb_ref, yc_ref, yd_ref,
                        on_ref, wout_ref, g1_ref, b1_ref, w1_ref, w3_ref, w2_ref, g2_ref, b2_ref,
                        out_ref, acc_ref):
    y = jnp.zeros(x_ref.shape, F32)
    for gi, y_ref in enumerate((ya_ref, yb_ref, yc_ref, yd_ref)):
        lo = gi * GROUP_W
        yn = _rmsnorm(y_ref[...], on_ref[:, lo:lo + GROUP_W])
        y = y + _dot(yn.astype(BF16), wout_ref[lo:lo + GROUP_W, :])
    x2 = _layernorm(ALPHA * x_ref[...] + y, g1_ref[...], b1_ref[...])
    out_ref[...] = _ffn_ln(x2, w1_ref, w3_ref, w2_ref, acc_ref, g2_ref[...], b2_ref[...])


def _outproj_ffn(x, ys, on, wout, g1, b1, w1, w3, w2, g2, b2):
    n = x.shape[0]
    tm = TOKEN_TILE
    tile = lambda w: pl.BlockSpec((tm, w), lambda i: (i, 0))
    consts = (on, wout, g1, b1, w1, w3, w2, g2, b2)
    return pl.pallas_call(
        _outproj_ffn_kernel,
        grid=(n // tm,),
        in_specs=[tile(D_MODEL)] + [tile(GROUP_W)] * len(ys) + [_const_spec(c.shape) for c in consts],
        out_specs=tile(D_MODEL),
        out_shape=jax.ShapeDtypeStruct((n, D_MODEL), F32),
        scratch_shapes=[pltpu.VMEM((tm, D_MODEL), F32)],
        compiler_params=pltpu.CompilerParams(dimension_semantics=("arbitrary",),
                                             vmem_limit_bytes=_vmem_limit(56 * 1024 * 1024)),
        name="outproj_ffn",
    )(x, *ys, *consts)


def _flash_init():
    cols = 2 * ATT_T
    one = (jnp.full((1, cols), NEG_INF, F32), jnp.zeros((1, cols), F32), jnp.zeros((PAIR_W, cols), F32))
    return (one, one)


def _flash_queries(qm_refs, q0):
    rows = pl.ds(q0, ATT_T)
    return [jnp.concatenate([qm_refs[2 * p][rows, :], qm_refs[2 * p + 1][rows, :]], axis=0) for p in range(2)]


def _flash_step(j, state, qm, kp_refs, vt_ref, bias):
    k0 = pl.multiple_of(j * ATT_T, ATT_T)
    new = []
    for p in range(2):
        m, l, acc = state[p]
        s_t = _dot_nt(kp_refs[p][pl.ds(k0, ATT_T), :], qm[p])
        if bias is not None:
            s_t = s_t + jnp.concatenate([bias, bias], axis=1)
        m_new = jnp.maximum(m, jnp.max(s_t, axis=0, keepdims=True))
        a = jnp.exp2(m - m_new)
        pr = jnp.exp2(s_t - m_new)
        l = a * l + jnp.sum(pr, axis=0, keepdims=True)
        acc = a * acc + _dot(vt_ref[j, p * PAIR_W:(p + 1) * PAIR_W, :], pr.astype(BF16))
        new.append((m_new, l, acc))
    return tuple(new)


def _flash_write(state, o_ref, q0):
    lane = lax.broadcasted_iota(jnp.int32, (1, PAIR_W), 1)
    for p in range(2):
        _, l, acc = state[p]
        o_t = jnp.transpose(acc * (1.0 / l))
        o_ref[pl.ds(q0, ATT_T), p * PAIR_W:(p + 1) * PAIR_W] = jnp.where(lane < HEAD_DIM, o_t[:ATT_T], o_t[ATT_T:])


def _store_transposed_values(v, vt_ref, j):
    vt_ref[j] = jnp.transpose(v).astype(BF16)


def _mla_kernel(za_ref, qn_ref, kvn_ref, wuq_ref, wukv_ref, cos_ref, sin_ref, o_ref,
                q0_scr, q1_scr, q2_scr, q3_scr, k0_scr, k1_scr, vt_scr):
    seq = za_ref.shape[0]
    qm_refs = (q0_scr, q1_scr, q2_scr, q3_scr)
    kp_refs = (k0_scr, k1_scr)
    c_q = (MLA_NOPE + MLA_ROPE) ** -0.5 * LOG2E

    def prep(j, carry):
        rows = pl.ds(pl.multiple_of(j * ATT_T, ATT_T), ATT_T)
        cos, sin = cos_ref[rows, :], sin_ref[rows, :]

        def rope(r):
            return r * cos + pltpu.roll(r, V7X_LANES // 2, axis=1) * sin

        cq = _rmsnorm(za_ref[rows, 0:Q_LORA], qn_ref[...])
        q = _dot(cq.astype(BF16), wuq_ref[...]) * c_q
        for h in range(HEADS):
            lo = h * GROUP_W
            qm_refs[h][rows, 0:V7X_LANES] = q[:, lo:lo + V7X_LANES].astype(BF16)
            qm_refs[h][rows, V7X_LANES:] = rope(q[:, lo + V7X_LANES:lo + GROUP_W]).astype(BF16)
        ckv = _rmsnorm(za_ref[rows, Q_LORA:Q_LORA + KV_LORA], kvn_ref[...])
        kv = _dot(ckv.astype(BF16), wukv_ref[...])
        k_rope = rope(za_ref[rows, Q_LORA + KV_LORA:]).astype(BF16)
        for p in range(2):
            kp_refs[p][rows, 0:V7X_LANES] = kv[:, p * PAIR_W:(p + 1) * PAIR_W].astype(BF16)
            kp_refs[p][rows, V7X_LANES:] = k_rope
        _store_transposed_values(kv[:, GROUP_W:], vt_scr, j)
        return carry

    lax.fori_loop(0, seq // ATT_T, prep, 0)

    causal = jnp.where(lax.broadcasted_iota(jnp.int32, (ATT_T, ATT_T), 0)
                       <= lax.broadcasted_iota(jnp.int32, (ATT_T, ATT_T), 1), 0.0, NEG_INF)

    def q_block(i, carry):
        q0 = pl.multiple_of(i * ATT_T, ATT_T)
        qm = _flash_queries(qm_refs, q0)
        state = lax.fori_loop(0, i, lambda j, st: _flash_step(j, st, qm, kp_refs, vt_scr, None), _flash_init())
        state = _flash_step(i, state, qm, kp_refs, vt_scr, causal)
        _flash_write(state, o_ref, q0)
        return carry

    lax.fori_loop(0, seq // ATT_T, q_block, 0)


def _mla(za, qn, kvn, wuq, wukv, cos, sin):
    b, s, _ = za.shape
    consts = (qn, kvn, wuq, wukv, cos, sin)
    return pl.pallas_call(
        _mla_kernel,
        grid=(b,),
        in_specs=[pl.BlockSpec((None, s, Z_A), lambda i: (i, 0, 0))] + [_const_spec(c.shape) for c in consts],
        out_specs=pl.BlockSpec((None, s, GROUP_W), lambda i: (i, 0, 0)),
        out_shape=jax.ShapeDtypeStruct((b, s, GROUP_W), F32),
        scratch_shapes=[pltpu.VMEM((s, GROUP_W), BF16)] * 6 + [pltpu.VMEM((s // ATT_T, GROUP_W, ATT_T), BF16)],
        compiler_params=pltpu.CompilerParams(dimension_semantics=("arbitrary",),
                                             vmem_limit_bytes=_vmem_limit(48 * 1024 * 1024)),
        name="mla",
    )(za, *consts)


def _swa_kernel(q_ref, k_ref, v_ref, bias_ref, o_ref, q0_scr, q1_scr, q2_scr, q3_scr, k_scr, vt_scr):
    seq = q_ref.shape[0]
    qm_refs = (q0_scr, q1_scr, q2_scr, q3_scr)
    kp_refs = (k_scr, k_scr)
    head = _head_of_lane(GROUP_W)
    c_q = HEAD_DIM ** -0.5 * LOG2E

    def prep(j, carry):
        rows = pl.ds(pl.multiple_of(j * ATT_T, ATT_T), ATT_T)
        q = (q_ref[rows, :] * c_q).astype(BF16)
        for h in range(HEADS):
            qm_refs[h][rows, :] = jnp.where(head == h, q, jnp.zeros_like(q))
        k_scr[rows, :] = k_ref[rows, :].astype(BF16)
        _store_transposed_values(v_ref[rows, :], vt_scr, j)
        return carry

    lax.fori_loop(0, seq // ATT_T, prep, 0)

    def q_block(i, carry):
        q0 = pl.multiple_of(i * ATT_T, ATT_T)
        qm = _flash_queries(qm_refs, q0)

        def step(j, st):
            return _flash_step(j, st, qm, kp_refs, vt_scr, bias_ref[jnp.minimum(i - j, N_BIAS_TILES - 1)])

        state = lax.fori_loop(0, i + 1, step, _flash_init())
        _flash_write(state, o_ref, q0)
        return carry

    lax.fori_loop(0, seq // ATT_T, q_block, 0)


def _swa_bias_tiles(seq):
    assert DILATED_PATTERNS[-1][0] >= seq - 1, "far tiles assume the widest window covers the sequence"
    assert all(w <= (N_BIAS_TILES - 2) * ATT_T for w, _ in DILATED_PATTERNS[:-1]), "narrow windows must end before the far tile"
    key = np.arange(ATT_T)[:, None]
    qry = np.arange(ATT_T)[None, :]
    tiles = []
    for dist in range(N_BIAS_TILES):
        d = dist * ATT_T + qry - key
        mult = sum(((d >= 0) & (d % dil == 0) & (d <= window)).astype(np.float32) for window, dil in DILATED_PATTERNS)
        with np.errstate(divide="ignore"):
            tiles.append(np.log2(mult))
    return jnp.asarray(np.stack(tiles), F32)


def _swa(q, k, v, bias):
    b, s, w = q.shape
    spec = pl.BlockSpec((None, s, w), lambda i: (i, 0, 0))
    return pl.pallas_call(
        _swa_kernel,
        grid=(b,),
        in_specs=[spec, spec, spec, _const_spec(bias.shape)],
        out_specs=spec,
        out_shape=jax.ShapeDtypeStruct((b, s, w), F32),
        scratch_shapes=[pltpu.VMEM((s, w), BF16)] * 5 + [pltpu.VMEM((s // ATT_T, w, ATT_T), BF16)],
        compiler_params=pltpu.CompilerParams(dimension_semantics=("arbitrary",),
                                             vmem_limit_bytes=_vmem_limit(48 * 1024 * 1024)),
        name="swa",
    )(q, k, v, bias)


def _shift_rows(x, d, fill):
    row = lax.broadcasted_iota(jnp.int32, x.shape, 0)
    return jnp.where(row >= d, pltpu.roll(x, d, axis=0), fill)


def _softplus(x):
    return jnp.maximum(x, 0.0) + jnp.log1p(jnp.exp(-jnp.abs(x)))


def _gelu_tanh(x):
    c = np.sqrt(2.0 / np.pi).astype(np.float32)
    return 0.5 * x * (1.0 + jnp.tanh(c * (x + 0.044715 * (x * x * x))))


def _rglru_kernel(z_ref, cw_ref, cb_ref, wa_ref, ba_ref, wx_ref, bx_ref, lam_ref, o_ref):
    seq = z_ref.shape[0]
    x = z_ref[:, 0:GROUP_W]
    xc = cb_ref[...] + cw_ref[CONV_W - 1:CONV_W, :] * x
    for j in range(CONV_W - 1):
        xc = xc + cw_ref[j:j + 1, :] * _shift_rows(x, CONV_W - 1 - j, 0.0)
    xcb = xc.astype(BF16)
    r = jax.nn.sigmoid(_dot(xcb, wa_ref[...]) + ba_ref[...])
    i = jax.nn.sigmoid(_dot(xcb, wx_ref[...]) + bx_ref[...])
    log_a = -LRU_C * r * _softplus(-lam_ref[...])
    a = jnp.exp(log_a)
    u = jnp.sqrt(-jnp.tanh(log_a) * (a * a + 1.0)) * (i * xc)
    d = 1
    while d < seq:
        u = a * _shift_rows(u, d, 0.0) + u
        a = a * _shift_rows(a, d, 1.0)
        d *= 2
    o_ref[...] = u * _gelu_tanh(z_ref[:, GROUP_W:])


def _rglru(z, cw, cb, wa, ba, wx, bx, lam):
    b, s, _ = z.shape
    consts = (cw, cb, wa, ba, wx, bx, lam)
    return pl.pallas_call(
        _rglru_kernel,
        grid=(b,),
        in_specs=[pl.BlockSpec((None, s, Z_LRU), lambda i: (i, 0, 0))] + [_const_spec(c.shape) for c in consts],
        out_specs=pl.BlockSpec((None, s, GROUP_W), lambda i: (i, 0, 0)),
        out_shape=jax.ShapeDtypeStruct((b, s, GROUP_W), F32),
        compiler_params=pltpu.CompilerParams(dimension_semantics=("arbitrary",),
                                             vmem_limit_bytes=_vmem_limit(48 * 1024 * 1024)),
        name="rglru",
    )(z, *consts)


def _expand_heads(cols, head):
    out = jnp.zeros((cols.shape[0], GROUP_W), F32)
    for h in range(HEADS):
        out = out + jnp.where(head == h, cols[:, h:h + 1], 0.0)
    return out


def _mlstm_kernel(z_ref, g_ref, bi_ref, bf_ref, o_ref, c_scr, n_scr, m_scr):
    seq = z_ref.shape[0]
    lc = ML_CHUNK
    head = _head_of_lane(GROUP_W)
    row_head = lax.broadcasted_iota(jnp.int32, (GROUP_W, GROUP_W), 0) // HEAD_DIM
    same_head = row_head == head
    tri = lax.broadcasted_iota(jnp.int32, (lc, lc), 0) >= lax.broadcasted_iota(jnp.int32, (lc, lc), 1)
    c_scr[...] = jnp.zeros_like(c_scr)
    n_scr[...] = jnp.zeros_like(n_scr)
    m_scr[...] = jnp.zeros_like(m_scr)

    def chunk(ci, carry):
        t0 = pl.multiple_of(ci * lc, lc)
        rows = pl.ds(t0, lc)
        q = z_ref[rows, 0:GROUP_W]
        k = z_ref[rows, GROUP_W:2 * GROUP_W] * (HEAD_DIM ** -0.5)
        v = z_ref[rows, 2 * GROUP_W:3 * GROUP_W]
        og = z_ref[rows, 3 * GROUP_W:4 * GROUP_W]
        gates = g_ref[rows, :]
        log_i = _expand_heads(gates[:, 0:HEADS], head) + bi_ref[...]
        f_pre = _expand_heads(gates[:, HEADS:2 * HEADS], head) + bf_ref[...]
        log_f = jnp.minimum(f_pre, 0.0) - jnp.log1p(jnp.exp(-jnp.abs(f_pre)))
        bcum = log_f
        d = 1
        while d < lc:
            bcum = bcum + _shift_rows(bcum, d, 0.0)
            d *= 2
        g_tot = bcum[lc - 1:lc, :]
        m_prev = m_scr[...]
        c_prev = c_scr[...]
        n_prev = n_scr[...]
        qb, kb, vb = q.astype(BF16), k.astype(BF16), v.astype(BF16)
        qs = jnp.concatenate([jnp.where(head == h, qb, jnp.zeros_like(qb)) for h in range(HEADS)], axis=0)
        s_all = _dot_nt(qs, kb)
        q_c = _dot(qb, c_prev.astype(BF16))
        qn = q * n_prev
        key_row = jnp.transpose(log_i - bcum)
        inter = bcum + m_prev
        out = jnp.zeros((lc, GROUP_W), F32)
        for h in range(HEADS):
            lane0 = h * HEAD_DIM
            dmat = jnp.where(tri, bcum[:, lane0:lane0 + 1] + key_row[lane0:lane0 + 1, :], NEG_INF)
            inter_h = inter[:, lane0:lane0 + 1]
            mt = jnp.maximum(jnp.max(dmat, axis=-1, keepdims=True), inter_h)
            sw = s_all[h * lc:(h + 1) * lc] * jnp.exp(dmat - mt)
            w_inter = jnp.exp(inter_h - mt)
            num = _dot(sw.astype(BF16), vb) + w_inter * q_c
            qn_h = jnp.sum(jnp.where(head == h, qn, 0.0), axis=-1, keepdims=True)
            den = jnp.sum(sw, axis=-1, keepdims=True) + w_inter * qn_h
            hh = num / jnp.maximum(jnp.abs(den), jnp.exp(-mt))
            out = out + jnp.where(head == h, hh, 0.0)
        o_ref[rows, :] = jax.nn.sigmoid(og) * out
        wk = g_tot - bcum + log_i
        m_new = jnp.maximum(g_tot + m_prev, jnp.max(wk, axis=0, keepdims=True))
        decay = jnp.exp(g_tot + m_prev - m_new)
        kw = k * jnp.exp(wk - m_new)
        kv = _dot(jnp.transpose(kw).astype(BF16), vb)
        c_scr[...] = decay * c_prev + jnp.where(same_head, kv, 0.0)
        n_scr[...] = decay * n_prev + jnp.sum(kw, axis=0, keepdims=True)
        m_scr[...] = m_new
        return carry

    lax.fori_loop(0, seq // lc, chunk, 0)


def _mlstm(z, gates, bi, bf):
    b, s, _ = z.shape
    return pl.pallas_call(
        _mlstm_kernel,
        grid=(b,),
        in_specs=[pl.BlockSpec((None, s, Z_ML), lambda i: (i, 0, 0)),
                  pl.BlockSpec((None, s, Z_G), lambda i: (i, 0, 0)),
                  _const_spec(bi.shape), _const_spec(bf.shape)],
        out_specs=pl.BlockSpec((None, s, GROUP_W), lambda i: (i, 0, 0)),
        out_shape=jax.ShapeDtypeStruct((b, s, GROUP_W), F32),
        scratch_shapes=[pltpu.VMEM((GROUP_W, GROUP_W), F32), pltpu.VMEM((1, GROUP_W), F32),
                        pltpu.VMEM((1, GROUP_W), F32)],
        compiler_params=pltpu.CompilerParams(dimension_semantics=("arbitrary",),
                                             vmem_limit_bytes=_vmem_limit(48 * 1024 * 1024)),
        name="mlstm",
    )(z, gates, bi, bf)


def _chunk_cols(w):
    return w.reshape(w.shape[0], N_FF_CHUNKS, FF_CHUNK).transpose(1, 0, 2).astype(BF16)


def _chunk_rows(w):
    return w.reshape(N_FF_CHUNKS, FF_CHUNK, w.shape[1]).astype(BF16)


def _block_diag(w):
    out = jnp.zeros((GROUP_W, GROUP_W), w.dtype)
    for h in range(HEADS):
        out = out.at[h * HEAD_DIM:(h + 1) * HEAD_DIM, h * HEAD_DIM:(h + 1) * HEAD_DIM].set(w[h])
    return out


def _permute_w_in(w):
    kr = Q_LORA + KV_LORA
    gap = jnp.zeros((w.shape[0], HEAD_DIM - ROPE_HALF), w.dtype)
    body_lo = kr + MLA_ROPE
    body_hi = body_lo + 9 * GROUP_W
    pad = jnp.zeros((w.shape[0], Z_G - 2 * HEADS), w.dtype)
    return jnp.concatenate([w[:, :kr], w[:, kr:kr + ROPE_HALF], gap, w[:, kr + ROPE_HALF:kr + MLA_ROPE], gap,
                            w[:, body_lo:body_hi], w[:, body_hi:], pad], axis=1).astype(BF16)


def _mla_up_layout():
    per_q = MLA_NOPE + MLA_ROPE
    q_src, q_dst = [], []
    for h in range(HEADS):
        base = h * GROUP_W
        nope_lo = base + (h % 2) * HEAD_DIM
        q_src += [h * per_q + i for i in range(per_q)]
        q_dst += [nope_lo + i for i in range(MLA_NOPE)]
        q_dst += [base + V7X_LANES + i for i in range(ROPE_HALF)]
        q_dst += [base + V7X_LANES + HEAD_DIM + i for i in range(ROPE_HALF)]
    per_kv = MLA_NOPE + HEAD_DIM
    kn = [h * per_kv + i for h in range(HEADS) for i in range(MLA_NOPE)]
    vv = [h * per_kv + MLA_NOPE + i for h in range(HEADS) for i in range(HEAD_DIM)]
    return np.array(q_src), np.array(q_dst), np.array(kn + vv)


def _rope_tables(seq):
    freqs = ROPE_THETA ** (-jnp.arange(ROPE_HALF, dtype=F32) / ROPE_HALF)
    ang = jnp.arange(seq, dtype=F32)[:, None] * freqs[None, :]
    cos = jnp.tile(jnp.cos(ang), (1, 2 * HEADS))
    sin = jnp.tile(jnp.sin(ang), (1, HEADS))
    return cos, jnp.concatenate([-sin, sin], axis=1)


def kernel(x, ln_g, ln_b, ffn_w1, ffn_w3, ffn_w2, w_in, mla_q_norm, mla_kv_norm, mla_w_uq, mla_w_ukv,
           lru_conv_w, lru_conv_b, lru_w_a, lru_b_a, lru_w_x, lru_b_x, lru_lambda, ml_b_i, ml_b_f,
           out_norm, w_out):
    bsz, seq, d = x.shape
    n = bsz * seq
    row = lambda t: t.reshape(1, -1)
    q_src, q_dst, kv_perm = _mla_up_layout()
    cos, sin = _rope_tables(seq)
    swa_bias = _swa_bias_tiles(seq)
    xt = x.reshape(n, d)
    for l in range(DEPTH):
        ffn = lambda j: (_chunk_cols(ffn_w1[l, j]), _chunk_cols(ffn_w3[l, j]), _chunk_rows(ffn_w2[l, j]))
        x1, za, zlru, sq, sk, sv, zml, zg = _ffn_inproj(
            xt, *ffn(0), row(ln_g[l, 0]), row(ln_b[l, 0]), _permute_w_in(w_in[l]))
        seqv = lambda t: t.reshape(bsz, seq, t.shape[-1])
        wuq = jnp.zeros((Q_LORA, HEADS * GROUP_W), F32).at[:, q_dst].set(mla_w_uq[l][:, q_src]).astype(BF16)
        y_a = _mla(seqv(za), row(mla_q_norm[l]), row(mla_kv_norm[l]), wuq,
                   mla_w_ukv[l][:, kv_perm].astype(BF16), cos, sin)
        y_b = _rglru(seqv(zlru), lru_conv_w[l], row(lru_conv_b[l]),
                     _block_diag(lru_w_a[l]).astype(BF16), row(lru_b_a[l]),
                     _block_diag(lru_w_x[l]).astype(BF16), row(lru_b_x[l]), row(lru_lambda[l]))
        y_c = _swa(seqv(sq), seqv(sk), seqv(sv), swa_bias)
        y_d = _mlstm(seqv(zml), seqv(zg), row(jnp.repeat(ml_b_i[l], HEAD_DIM)), row(jnp.repeat(ml_b_f[l], HEAD_DIM)))
        ys = [t.reshape(n, GROUP_W) for t in (y_a, y_b, y_c, y_d)]
        xt = _outproj_ffn(x1, ys, row(out_norm[l]), w_out[l].astype(BF16), row(ln_g[l, 1]), row(ln_b[l, 1]),
                          *ffn(1), row(ln_g[l, 2]), row(ln_b[l, 2]))
    return xt.reshape(bsz, seq, d)
```

```python
import numpy as np
import jax
import jax.numpy as jnp
from jax import lax
from jax.experimental import pallas as pl
from jax.experimental.pallas import tpu as pltpu

F32 = jnp.float32
BF16 = jnp.bfloat16

D_MODEL = 1024
DEPTH = 2
GROUP_W = 256
HEAD_DIM = 64
HEADS = 4
D_FF = 2816
ALPHA = (2.0 * DEPTH) ** 0.25
Q_LORA = 256
KV_LORA = 128
MLA_NOPE = 64
MLA_ROPE = 32
ROPE_HALF = MLA_ROPE // 2
ROPE_THETA = 10000.0
CONV_W = 4
LRU_C = 8.0
DILATED_PATTERNS = ((128, 1), (512, 4), (2048, 16))

V7X_LANES = 128
V7X_VMEM_BYTES = 64 * 1024 * 1024

FF_CHUNK = 256
N_FF_CHUNKS = D_FF // FF_CHUNK
TOKEN_TILE = 512
ML_CHUNK = 128
ATT_T = 256
ATT_COLS = 1024
MLA_TQ = 512
PAIR_W = 2 * HEAD_DIM
N_BIAS_TILES = 4

Z_A = 512
Z_LRU = 512
Z_ML = 1024
Z_G = 128

NEG_INF = float("-inf")
LOG2E = float(np.log2(np.e))


def _vmem_limit(nbytes):
    return int(min(nbytes, V7X_VMEM_BYTES - 6 * 1024 * 1024))


def _const_spec(shape):
    nd = len(shape)
    return pl.BlockSpec(shape, lambda *_: (0,) * nd, pipeline_mode=pl.Buffered(1))


def _layernorm(v, g, b):
    mu = jnp.mean(v, axis=-1, keepdims=True)
    d = v - mu
    var = jnp.mean(d * d, axis=-1, keepdims=True)
    return d * lax.rsqrt(var + 1e-5) * g + b


def _rmsnorm(v, g, eps=1e-6):
    return v * lax.rsqrt(jnp.mean(v * v, axis=-1, keepdims=True) + eps) * g


def _dot(a, b):
    return jnp.dot(a, b, preferred_element_type=F32)


def _dot_nt(a, b):
    return lax.dot_general(a, b, (((1,), (1,)), ((), ())), preferred_element_type=F32)


def _head_of_lane(width, group=HEAD_DIM):
    return lax.broadcasted_iota(jnp.int32, (1, width), 1) // group


def _ffn_ln(x, w1_ref, w3_ref, w2_ref, acc_ref, g, b):
    xb = x.astype(BF16)
    acc_ref[...] = jnp.zeros_like(acc_ref)

    def body(c, carry):
        h1 = _dot(xb, w1_ref[c])
        h3 = _dot(xb, w3_ref[c])
        h = (h1 * jax.nn.sigmoid(h1)) * h3
        acc_ref[...] += _dot(h.astype(BF16), w2_ref[c])
        return carry

    lax.fori_loop(0, N_FF_CHUNKS, body, 0)
    return _layernorm(ALPHA * x + 0.5 * acc_ref[...], g, b)


def _ffn_inproj_kernel(x_ref, w1_ref, w3_ref, w2_ref, g_ref, b_ref, win_ref,
                       x1_ref, za_ref, zlru_ref, sq_ref, sk_ref, sv_ref, zml_ref, zg_ref, acc_ref):
    x1 = _ffn_ln(x_ref[...], w1_ref, w3_ref, w2_ref, acc_ref, g_ref[...], b_ref[...])
    x1_ref[...] = x1
    xb = x1.astype(BF16)
    off = 0
    for ref in (za_ref, zlru_ref, sq_ref, sk_ref, sv_ref, zml_ref, zg_ref):
        w = ref.shape[-1]
        ref[...] = _dot(xb, win_ref[:, off:off + w])
        off += w


def _ffn_inproj(x, w1, w3, w2, g, b, win):
    n = x.shape[0]
    tm = TOKEN_TILE
    tile = lambda w: pl.BlockSpec((tm, w), lambda i: (i, 0))
    widths = (D_MODEL, Z_A, Z_LRU, GROUP_W, GROUP_W, GROUP_W, Z_ML, Z_G)
    return pl.pallas_call(
        _ffn_inproj_kernel,
        grid=(n // tm,),
        in_specs=[tile(D_MODEL), _const_spec(w1.shape), _const_spec(w3.shape), _const_spec(w2.shape),
                  _const_spec(g.shape), _const_spec(b.shape), _const_spec(win.shape)],
        out_specs=[tile(w) for w in widths],
        out_shape=[jax.ShapeDtypeStruct((n, w), F32) for w in widths],
        scratch_shapes=[pltpu.VMEM((tm, D_MODEL), F32)],
        compiler_params=pltpu.CompilerParams(dimension_semantics=("arbitrary",),
                                             vmem_limit_bytes=_vmem_limit(56 * 1024 * 1024)),
        name="ffn_inproj",
    )(x, w1, w3, w2, g, b, win)


def _outproj_ffn_kernel(x_ref, ya_ref, yb_ref, yc_ref, yd_ref,
                        on_ref, wout_ref, g1_ref, b1_ref, w1_ref, w3_ref, w2_ref, g2_ref, b2_ref,
                        out_ref, acc_ref):
    y = jnp.zeros(x_ref.shape, F32)
    for gi, y_ref in enumerate((ya_ref, yb_ref, yc_ref, yd_ref)):
        lo = gi * GROUP_W
        yn = _rmsnorm(y_ref[...], on_ref[:, lo:lo + GROUP_W])
        y = y + _dot(yn.astype(BF16), wout_ref[lo:lo + GROUP_W, :])
    x2 = _layernorm(ALPHA * x_ref[...] + y, g1_ref[...], b1_ref[...])
    out_ref[...] = _ffn_ln(x2, w1_ref, w3_ref, w2_ref, acc_ref, g2_ref[...], b2_ref[...])


def _outproj_ffn(x, ys, on, wout, g1, b1, w1, w3, w2, g2, b2):
    n = x.shape[0]
    tm = TOKEN_TILE
    tile = lambda w: pl.BlockSpec((tm, w), lambda i: (i, 0))
    consts = (on, wout, g1, b1, w1, w3, w2, g2, b2)
    return pl.pallas_call(
        _outproj_ffn_kernel,
        grid=(n // tm,),
        in_specs=[tile(D_MODEL)] + [tile(GROUP_W)] * len(ys) + [_const_spec(c.shape) for c in consts],
        out_specs=tile(D_MODEL),
        out_shape=jax.ShapeDtypeStruct((n, D_MODEL), F32),
        scratch_shapes=[pltpu.VMEM((tm, D_MODEL), F32)],
        compiler_params=pltpu.CompilerParams(dimension_semantics=("arbitrary",),
                                             vmem_limit_bytes=_vmem_limit(56 * 1024 * 1024)),
        name="outproj_ffn",
    )(x, *ys, *consts)


def _flash_state_scratch():
    return [pltpu.VMEM((1, ATT_COLS), F32), pltpu.VMEM((1, ATT_COLS), F32), pltpu.VMEM((PAIR_W, ATT_COLS), F32)]


def _flash_reset(state):
    m_ref, l_ref, acc_ref = state
    m_ref[...] = jnp.full(m_ref.shape, NEG_INF, F32)
    l_ref[...] = jnp.zeros_like(l_ref)
    acc_ref[...] = jnp.zeros_like(acc_ref)


def _flash_pipe_scratch():
    return [pltpu.VMEM((2, ATT_T, ATT_COLS), F32), pltpu.VMEM((2, 1, ATT_COLS), F32)]


def _flash_attend(n_steps, qm, key_block, value_block, bias_block, state, pipe, groups):
    m_ref, l_ref, acc_ref = state
    s_ref, mx_ref = pipe
    _flash_reset(state)

    def scores(j, slot):
        bias = bias_block(j)
        s_t = _dot_nt(key_block(j), qm) + jnp.concatenate([bias] * (ATT_COLS // bias.shape[1]), axis=1)
        s_ref[slot] = s_t
        mx_ref[slot] = jnp.max(s_t, axis=0, keepdims=True)

    scores(0, 0)

    def step(j, carry):
        cur = lax.rem(j, 2)
        m = m_ref[...]
        m_new = jnp.maximum(m, mx_ref[cur])
        a = jnp.exp2(m - m_new)
        pr = jnp.exp2(s_ref[cur] - m_new)
        m_ref[...] = m_new
        l_ref[...] = a * l_ref[...] + jnp.sum(pr, axis=0, keepdims=True)
        prb = pr.astype(BF16)
        scores(jnp.minimum(j + 1, n_steps - 1), 1 - cur)
        vt_blk = value_block(j)
        for v_lo, c_lo, c_hi in groups:
            acc_ref[:, c_lo:c_hi] = (a[:, c_lo:c_hi] * acc_ref[:, c_lo:c_hi]
                                     + _dot(vt_blk[v_lo:v_lo + PAIR_W, :], prb[:, c_lo:c_hi]))
        return carry

    lax.fori_loop(0, n_steps, step, 0)


def _flash_result(state):
    _, l_ref, acc_ref = state
    return jnp.transpose(acc_ref[...] * (1.0 / l_ref[...]))


def _merge_pair(o_a, o_b):
    lane = lax.broadcasted_iota(jnp.int32, (1, PAIR_W), 1)
    return jnp.where(lane < HEAD_DIM, o_a, o_b)


def _store_transposed_values(v, vt_ref, j):
    vt_ref[j] = jnp.transpose(v).astype(BF16)


def _mla_kernel(za_ref, qn_ref, kvn_ref, wuq_ref, wukv_ref, cos_ref, sin_ref, o_ref,
                qm_scr, kp_scr, vt_scr, o_scr, bias_scr, m_scr, l_scr, acc_scr, s_scr, mx_scr):
    seq = za_ref.shape[0]
    state = (m_scr, l_scr, acc_scr)
    pipe = (s_scr, mx_scr)
    c_q = (MLA_NOPE + MLA_ROPE) ** -0.5 * LOG2E

    def prep(j, carry):
        rows = pl.ds(pl.multiple_of(j * ATT_T, ATT_T), ATT_T)
        cos, sin = cos_ref[rows, :], sin_ref[rows, :]

        def rope(r):
            return r * cos + pltpu.roll(r, V7X_LANES // 2, axis=1) * sin

        cq = _rmsnorm(za_ref[rows, 0:Q_LORA], qn_ref[...])
        q = _dot(cq.astype(BF16), wuq_ref[...]) * c_q
        for h in range(HEADS):
            lo = h * GROUP_W
            qm_scr[h, rows, 0:V7X_LANES] = q[:, lo:lo + V7X_LANES].astype(BF16)
            qm_scr[h, rows, V7X_LANES:] = rope(q[:, lo + V7X_LANES:lo + GROUP_W]).astype(BF16)
        ckv = _rmsnorm(za_ref[rows, Q_LORA:Q_LORA + KV_LORA], kvn_ref[...])
        kv = _dot(ckv.astype(BF16), wukv_ref[...])
        k_rope = rope(za_ref[rows, Q_LORA + KV_LORA:]).astype(BF16)
        for p in range(2):
            kp_scr[p, rows, 0:V7X_LANES] = kv[:, p * PAIR_W:(p + 1) * PAIR_W].astype(BF16)
            kp_scr[p, rows, V7X_LANES:] = k_rope
        _store_transposed_values(kv[:, GROUP_W:], vt_scr, j)
        return carry

    lax.fori_loop(0, seq // ATT_T, prep, 0)

    key = lax.broadcasted_iota(jnp.int32, (ATT_T, MLA_TQ), 0)
    qry = lax.broadcasted_iota(jnp.int32, (ATT_T, MLA_TQ), 1)
    n_diag = MLA_TQ // ATT_T
    bias_scr[0] = jnp.zeros((ATT_T, MLA_TQ), F32)
    for d in range(n_diag):
        bias_scr[1 + d] = jnp.where(key + d * ATT_T <= qry, 0.0, NEG_INF)
    groups = [(0, 0, ATT_COLS)]

    def q_block(i, carry):
        q0 = pl.multiple_of(i * MLA_TQ, MLA_TQ)
        q_rows = pl.ds(q0, MLA_TQ)
        first_diag = i * n_diag

        def pair(p, c):
            qm = jnp.concatenate([qm_scr[2 * p, q_rows, :], qm_scr[2 * p + 1, q_rows, :]], axis=0)
            v_lo = pl.multiple_of(p * PAIR_W, PAIR_W)
            _flash_attend(
                first_diag + n_diag, qm,
                lambda j: kp_scr[p, pl.ds(pl.multiple_of(j * ATT_T, ATT_T), ATT_T), :],
                lambda j: vt_scr.at[j, pl.ds(v_lo, PAIR_W), :],
                lambda j: bias_scr[jnp.maximum(j - first_diag + 1, 0)],
                state, pipe, groups)
            o_t = _flash_result(state)
            o_scr[p] = _merge_pair(o_t[:MLA_TQ], o_t[MLA_TQ:])
            return c

        lax.fori_loop(0, 2, pair, 0)
        for p in range(2):
            o_ref[q_rows, p * PAIR_W:(p + 1) * PAIR_W] = o_scr[p]
        return carry

    lax.fori_loop(0, seq // MLA_TQ, q_block, 0)


def _mla(za, qn, kvn, wuq, wukv, cos, sin):
    b, s, _ = za.shape
    consts = (qn, kvn, wuq, wukv, cos, sin)
    return pl.pallas_call(
        _mla_kernel,
        grid=(b,),
        in_specs=[pl.BlockSpec((None, s, Z_A), lambda i: (i, 0, 0))] + [_const_spec(c.shape) for c in consts],
        out_specs=pl.BlockSpec((None, s, GROUP_W), lambda i: (i, 0, 0)),
        out_shape=jax.ShapeDtypeStruct((b, s, GROUP_W), F32),
        scratch_shapes=([pltpu.VMEM((HEADS, s, GROUP_W), BF16), pltpu.VMEM((2, s, GROUP_W), BF16),
                         pltpu.VMEM((s // ATT_T, GROUP_W, ATT_T), BF16), pltpu.VMEM((2, MLA_TQ, PAIR_W), F32),
                         pltpu.VMEM((1 + MLA_TQ // ATT_T, ATT_T, MLA_TQ), F32)]
                        + _flash_state_scratch() + _flash_pipe_scratch()),
        compiler_params=pltpu.CompilerParams(dimension_semantics=("arbitrary",),
                                             vmem_limit_bytes=_vmem_limit(48 * 1024 * 1024)),
        name="mla",
    )(za, *consts)


def _swa_kernel(q_ref, k_ref, v_ref, bias_ref, o_ref, qm_scr, k_scr, vt_scr, m_scr, l_scr, acc_scr,
                s_scr, mx_scr):
    seq = q_ref.shape[0]
    state = (m_scr, l_scr, acc_scr)
    pipe = (s_scr, mx_scr)
    head = _head_of_lane(GROUP_W)
    c_q = HEAD_DIM ** -0.5 * LOG2E
    half = ATT_COLS // 2
    groups = [(0, 0, half), (PAIR_W, half, ATT_COLS)]

    def prep(j, carry):
        rows = pl.ds(pl.multiple_of(j * ATT_T, ATT_T), ATT_T)
        q = (q_ref[rows, :] * c_q).astype(BF16)
        for h in range(HEADS):
            qm_scr[h, rows, :] = jnp.where(head == h, q, jnp.zeros_like(q))
        k_scr[rows, :] = k_ref[rows, :].astype(BF16)
        _store_transposed_values(v_ref[rows, :], vt_scr, j)
        return carry

    lax.fori_loop(0, seq // ATT_T, prep, 0)

    def q_block(i, carry):
        q_rows = pl.ds(pl.multiple_of(i * ATT_T, ATT_T), ATT_T)
        qm = jnp.concatenate([qm_scr[h, q_rows, :] for h in range(HEADS)], axis=0)
        _flash_attend(
            i + 1, qm,
            lambda j: k_scr[pl.ds(pl.multiple_of(j * ATT_T, ATT_T), ATT_T), :],
            lambda j: vt_scr.at[j],
            lambda j: bias_ref[jnp.minimum(i - j, N_BIAS_TILES - 1)],
            state, pipe, groups)
        o_t = _flash_result(state)
        for p in range(2):
            lo = 2 * p * ATT_T
            o_ref[q_rows, p * PAIR_W:(p + 1) * PAIR_W] = _merge_pair(o_t[lo:lo + ATT_T], o_t[lo + ATT_T:lo + 2 * ATT_T])
        return carry

    lax.fori_loop(0, seq // ATT_T, q_block, 0)


def _swa_bias_tiles(seq):
    assert DILATED_PATTERNS[-1][0] >= seq - 1, "far tiles assume the widest window covers the sequence"
    assert all(w <= (N_BIAS_TILES - 2) * ATT_T for w, _ in DILATED_PATTERNS[:-1]), "narrow windows must end before the far tile"
    key = np.arange(ATT_T)[:, None]
    qry = np.arange(ATT_T)[None, :]
    tiles = []
    for dist in range(N_BIAS_TILES):
        d = dist * ATT_T + qry - key
        mult = sum(((d >= 0) & (d % dil == 0) & (d <= window)).astype(np.float32) for window, dil in DILATED_PATTERNS)
        with np.errstate(divide="ignore"):
            tiles.append(np.log2(mult))
    return jnp.asarray(np.stack(tiles), F32)


def _swa(q, k, v, bias):
    b, s, w = q.shape
    spec = pl.BlockSpec((None, s, w), lambda i: (i, 0, 0))
    return pl.pallas_call(
        _swa_kernel,
        grid=(b,),
        in_specs=[spec, spec, spec, _const_spec(bias.shape)],
        out_specs=spec,
        out_shape=jax.ShapeDtypeStruct((b, s, w), F32),
        scratch_shapes=([pltpu.VMEM((HEADS, s, w), BF16), pltpu.VMEM((s, w), BF16),
                         pltpu.VMEM((s // ATT_T, w, ATT_T), BF16)] + _flash_state_scratch()
                        + _flash_pipe_scratch()),
        compiler_params=pltpu.CompilerParams(dimension_semantics=("arbitrary",),
                                             vmem_limit_bytes=_vmem_limit(48 * 1024 * 1024)),
        name="swa",
    )(q, k, v, bias)


def _shift_rows(x, d, fill):
    row = lax.broadcasted_iota(jnp.int32, x.shape, 0)
    return jnp.where(row >= d, pltpu.roll(x, d, axis=0), fill)


def _softplus(x):
    return jnp.maximum(x, 0.0) + jnp.log1p(jnp.exp(-jnp.abs(x)))


def _gelu_tanh(x):
    c = np.sqrt(2.0 / np.pi).astype(np.float32)
    return 0.5 * x * (1.0 + jnp.tanh(c * (x + 0.044715 * (x * x * x))))


def _rglru_kernel(z_ref, cw_ref, cb_ref, wa_ref, ba_ref, wx_ref, bx_ref, lam_ref, o_ref):
    seq = z_ref.shape[0]
    x = z_ref[:, 0:GROUP_W]
    xc = cb_ref[...] + cw_ref[CONV_W - 1:CONV_W, :] * x
    for j in range(CONV_W - 1):
        xc = xc + cw_ref[j:j + 1, :] * _shift_rows(x, CONV_W - 1 - j, 0.0)
    xcb = xc.astype(BF16)
    r = jax.nn.sigmoid(_dot(xcb, wa_ref[...]) + ba_ref[...])
    i = jax.nn.sigmoid(_dot(xcb, wx_ref[...]) + bx_ref[...])
    log_a = -LRU_C * r * _softplus(-lam_ref[...])
    a = jnp.exp(log_a)
    u = jnp.sqrt(-jnp.tanh(log_a) * (a * a + 1.0)) * (i * xc)
    d = 1
    while d < seq:
        u = a * _shift_rows(u, d, 0.0) + u
        a = a * _shift_rows(a, d, 1.0)
        d *= 2
    o_ref[...] = u * _gelu_tanh(z_ref[:, GROUP_W:])


def _rglru(z, cw, cb, wa, ba, wx, bx, lam):
    b, s, _ = z.shape
    consts = (cw, cb, wa, ba, wx, bx, lam)
    return pl.pallas_call(
        _rglru_kernel,
        grid=(b,),
        in_specs=[pl.BlockSpec((None, s, Z_LRU), lambda i: (i, 0, 0))] + [_const_spec(c.shape) for c in consts],
        out_specs=pl.BlockSpec((None, s, GROUP_W), lambda i: (i, 0, 0)),
        out_shape=jax.ShapeDtypeStruct((b, s, GROUP_W), F32),
        compiler_params=pltpu.CompilerParams(dimension_semantics=("arbitrary",),
                                             vmem_limit_bytes=_vmem_limit(48 * 1024 * 1024)),
        name="rglru",
    )(z, *consts)


def _expand_heads(cols, head):
    out = jnp.zeros((cols.shape[0], GROUP_W), F32)
    for h in range(HEADS):
        out = out + jnp.where(head == h, cols[:, h:h + 1], 0.0)
    return out


def _mlstm_kernel(z_ref, g_ref, bi_ref, bf_ref, o_ref, c_scr, n_scr, m_scr):
    seq = z_ref.shape[0]
    lc = ML_CHUNK
    head = _head_of_lane(GROUP_W)
    row_head = lax.broadcasted_iota(jnp.int32, (GROUP_W, GROUP_W), 0) // HEAD_DIM
    same_head = row_head == head
    tri = lax.broadcasted_iota(jnp.int32, (lc, lc), 0) >= lax.broadcasted_iota(jnp.int32, (lc, lc), 1)
    c_scr[...] = jnp.zeros_like(c_scr)
    n_scr[...] = jnp.zeros_like(n_scr)
    m_scr[...] = jnp.zeros_like(m_scr)

    def chunk(ci, carry):
        t0 = pl.multiple_of(ci * lc, lc)
        rows = pl.ds(t0, lc)
        q = z_ref[rows, 0:GROUP_W]
        k = z_ref[rows, GROUP_W:2 * GROUP_W] * (HEAD_DIM ** -0.5)
        v = z_ref[rows, 2 * GROUP_W:3 * GROUP_W]
        og = z_ref[rows, 3 * GROUP_W:4 * GROUP_W]
        gates = g_ref[rows, :]
        log_i = _expand_heads(gates[:, 0:HEADS], head) + bi_ref[...]
        f_pre = _expand_heads(gates[:, HEADS:2 * HEADS], head) + bf_ref[...]
        log_f = jnp.minimum(f_pre, 0.0) - jnp.log1p(jnp.exp(-jnp.abs(f_pre)))
        bcum = log_f
        d = 1
        while d < lc:
            bcum = bcum + _shift_rows(bcum, d, 0.0)
            d *= 2
        g_tot = bcum[lc - 1:lc, :]
        m_prev = m_scr[...]
        c_prev = c_scr[...]
        n_prev = n_scr[...]
        qb, kb, vb = q.astype(BF16), k.astype(BF16), v.astype(BF16)
        qs = jnp.concatenate([jnp.where(head == h, qb, jnp.zeros_like(qb)) for h in range(HEADS)], axis=0)
        s_all = _dot_nt(qs, kb)
        q_c = _dot(qb, c_prev.astype(BF16))
        qn = q * n_prev
        key_row = jnp.transpose(log_i - bcum)
        inter = bcum + m_prev
        out = jnp.zeros((lc, GROUP_W), F32)
        for h in range(HEADS):
            lane0 = h * HEAD_DIM
            dmat = jnp.where(tri, bcum[:, lane0:lane0 + 1] + key_row[lane0:lane0 + 1, :], NEG_INF)
            inter_h = inter[:, lane0:lane0 + 1]
            mt = jnp.maximum(jnp.max(dmat, axis=-1, keepdims=True), inter_h)
            sw = s_all[h * lc:(h + 1) * lc] * jnp.exp(dmat - mt)
            w_inter = jnp.exp(inter_h - mt)
            num = _dot(sw.astype(BF16), vb) + w_inter * q_c
            qn_h = jnp.sum(jnp.where(head == h, qn, 0.0), axis=-1, keepdims=True)
            den = jnp.sum(sw, axis=-1, keepdims=True) + w_inter * qn_h
            hh = num / jnp.maximum(jnp.abs(den), jnp.exp(-mt))
            out = out + jnp.where(head == h, hh, 0.0)
        o_ref[rows, :] = jax.nn.sigmoid(og) * out
        wk = g_tot - bcum + log_i
        m_new = jnp.maximum(g_tot + m_prev, jnp.max(wk, axis=0, keepdims=True))
        decay = jnp.exp(g_tot + m_prev - m_new)
        kw = k * jnp.exp(wk - m_new)
        kv = _dot(jnp.transpose(kw).astype(BF16), vb)
        c_scr[...] = decay * c_prev + jnp.where(same_head, kv, 0.0)
        n_scr[...] = decay * n_prev + jnp.sum(kw, axis=0, keepdims=True)
        m_scr[...] = m_new
        return carry

    lax.fori_loop(0, seq // lc, chunk, 0)


def _mlstm(z, gates, bi, bf):
    b, s, _ = z.shape
    return pl.pallas_call(
        _mlstm_kernel,
        grid=(b,),
        in_specs=[pl.BlockSpec((None, s, Z_ML), lambda i: (i, 0, 0)),
                  pl.BlockSpec((None, s, Z_G), lambda i: (i, 0, 0)),
                  _const_spec(bi.shape), _const_spec(bf.shape)],
        out_specs=pl.BlockSpec((None, s, GROUP_W), lambda i: (i, 0, 0)),
        out_shape=jax.ShapeDtypeStruct((b, s, GROUP_W), F32),
        scratch_shapes=[pltpu.VMEM((GROUP_W, GROUP_W), F32), pltpu.VMEM((1, GROUP_W), F32),
                        pltpu.VMEM((1, GROUP_W), F32)],
        compiler_params=pltpu.CompilerParams(dimension_semantics=("arbitrary",),
                                             vmem_limit_bytes=_vmem_limit(48 * 1024 * 1024)),
        name="mlstm",
    )(z, gates, bi, bf)


def _chunk_cols(w):
    return w.reshape(w.shape[0], N_FF_CHUNKS, FF_CHUNK).transpose(1, 0, 2).astype(BF16)


def _chunk_rows(w):
    return w.reshape(N_FF_CHUNKS, FF_CHUNK, w.shape[1]).astype(BF16)


def _block_diag(w):
    out = jnp.zeros((GROUP_W, GROUP_W), w.dtype)
    for h in range(HEADS):
        out = out.at[h * HEAD_DIM:(h + 1) * HEAD_DIM, h * HEAD_DIM:(h + 1) * HEAD_DIM].set(w[h])
    return out


def _permute_w_in(w):
    kr = Q_LORA + KV_LORA
    gap = jnp.zeros((w.shape[0], HEAD_DIM - ROPE_HALF), w.dtype)
    body_lo = kr + MLA_ROPE
    body_hi = body_lo + 9 * GROUP_W
    pad = jnp.zeros((w.shape[0], Z_G - 2 * HEADS), w.dtype)
    return jnp.concatenate([w[:, :kr], w[:, kr:kr + ROPE_HALF], gap, w[:, kr + ROPE_HALF:kr + MLA_ROPE], gap,
                            w[:, body_lo:body_hi], w[:, body_hi:], pad], axis=1).astype(BF16)


def _mla_up_layout():
    per_q = MLA_NOPE + MLA_ROPE
    q_src, q_dst = [], []
    for h in range(HEADS):
        base = h * GROUP_W
        nope_lo = base + (h % 2) * HEAD_DIM
        q_src += [h * per_q + i for i in range(per_q)]
        q_dst += [nope_lo + i for i in range(MLA_NOPE)]
        q_dst += [base + V7X_LANES + i for i in range(ROPE_HALF)]
        q_dst += [base + V7X_LANES + HEAD_DIM + i for i in range(ROPE_HALF)]
    per_kv = MLA_NOPE + HEAD_DIM
    kn = [h * per_kv + i for h in range(HEADS) for i in range(MLA_NOPE)]
    vv = [h * per_kv + MLA_NOPE + i for h in range(HEADS) for i in range(HEAD_DIM)]
    return np.array(q_src), np.array(q_dst), np.array(kn + vv)


def _rope_tables(seq):
    freqs = ROPE_THETA ** (-jnp.arange(ROPE_HALF, dtype=F32) / ROPE_HALF)
    ang = jnp.arange(seq, dtype=F32)[:, None] * freqs[None, :]
    cos = jnp.tile(jnp.cos(ang), (1, 2 * HEADS))
    sin = jnp.tile(jnp.sin(ang), (1, HEADS))
    return cos, jnp.concatenate([-sin, sin], axis=1)


def kernel(x, ln_g, ln_b, ffn_w1, ffn_w3, ffn_w2, w_in, mla_q_norm, mla_kv_norm, mla_w_uq, mla_w_ukv,
           lru_conv_w, lru_conv_b, lru_w_a, lru_b_a, lru_w_x, lru_b_x, lru_lambda, ml_b_i, ml_b_f,
           out_norm, w_out):
    bsz, seq, d = x.shape
    n = bsz * seq
    row = lambda t: t.reshape(1, -1)
    q_src, q_dst, kv_perm = _mla_up_layout()
    cos, sin = _rope_tables(seq)
    swa_bias = _swa_bias_tiles(seq)
    xt = x.reshape(n, d)
    for l in range(DEPTH):
        ffn = lambda j: (_chunk_cols(ffn_w1[l, j]), _chunk_cols(ffn_w3[l, j]), _chunk_rows(ffn_w2[l, j]))
        x1, za, zlru, sq, sk, sv, zml, zg = _ffn_inproj(
            xt, *ffn(0), row(ln_g[l, 0]), row(ln_b[l, 0]), _permute_w_in(w_in[l]))
        seqv = lambda t: t.reshape(bsz, seq, t.shape[-1])
        wuq = jnp.zeros((Q_LORA, HEADS * GROUP_W), F32).at[:, q_dst].set(mla_w_uq[l][:, q_src]).astype(BF16)
        y_a = _mla(seqv(za), row(mla_q_norm[l]), row(mla_kv_norm[l]), wuq,
                   mla_w_ukv[l][:, kv_perm].astype(BF16), cos, sin)
        y_b = _rglru(seqv(zlru), lru_conv_w[l], row(lru_conv_b[l]),
                     _block_diag(lru_w_a[l]).astype(BF16), row(lru_b_a[l]),
                     _block_diag(lru_w_x[l]).astype(BF16), row(lru_b_x[l]), row(lru_lambda[l]))
        y_c = _swa(seqv(sq), seqv(sk), seqv(sv), swa_bias)
        y_d = _mlstm(seqv(zml), seqv(zg), row(jnp.repeat(ml_b_i[l], HEAD_DIM)), row(jnp.repeat(ml_b_f[l], HEAD_DIM)))
        ys = [t.reshape(n, GROUP_W) for t in (y_a, y_b, y_c, y_d)]
        xt = _outproj_ffn(x1, ys, row(out_norm[l]), w_out[l].astype(BF16), row(ln_g[l, 1]), row(ln_b[l, 1]),
                          *ffn(1), row(ln_g[l, 2]), row(ln_b[l, 2]))
    return xt.reshape(bsz, seq, d)
```

```python
import numpy as np
import jax
import jax.numpy as jnp
from jax import lax
from jax.experimental import pallas as pl
from jax.experimental.pallas import tpu as pltpu

F32 = jnp.float32
BF16 = jnp.bfloat16

D_MODEL = 1024
DEPTH = 2
GROUP_W = 256
HEAD_DIM = 64
HEADS = 4
D_FF = 2816
ALPHA = (2.0 * DEPTH) ** 0.25
Q_LORA = 256
KV_LORA = 128
MLA_NOPE = 64
MLA_ROPE = 32
ROPE_HALF = MLA_ROPE // 2
ROPE_THETA = 10000.0
CONV_W = 4
LRU_C = 8.0
DILATED_PATTERNS = ((128, 1), (512, 4), (2048, 16))

V7X_LANES = 128
V7X_VMEM_BYTES = 64 * 1024 * 1024

FF_CHUNK = 256
N_FF_CHUNKS = D_FF // FF_CHUNK
TOKEN_TILE = 512
ML_CHUNK = 128
ATT_T = 256
ATT_COLS = 1024
MLA_TQ = 512
PAIR_W = 2 * HEAD_DIM
N_BIAS_TILES = 4

Z_A = 512
Z_LRU = 512
Z_ML = 1024
Z_G = 128

NEG_INF = float("-inf")
LOG2E = float(np.log2(np.e))


def _vmem_limit(nbytes):
    return int(min(nbytes, V7X_VMEM_BYTES - 6 * 1024 * 1024))


def _const_spec(shape):
    nd = len(shape)
    return pl.BlockSpec(shape, lambda *_: (0,) * nd, pipeline_mode=pl.Buffered(1))


def _layernorm(v, g, b):
    mu = jnp.mean(v, axis=-1, keepdims=True)
    d = v - mu
    var = jnp.mean(d * d, axis=-1, keepdims=True)
    return d * lax.rsqrt(var + 1e-5) * g + b


def _rmsnorm(v, g, eps=1e-6):
    return v * lax.rsqrt(jnp.mean(v * v, axis=-1, keepdims=True) + eps) * g


def _dot(a, b):
    return jnp.dot(a, b, preferred_element_type=F32)


def _dot_nt(a, b):
    return lax.dot_general(a, b, (((1,), (1,)), ((), ())), preferred_element_type=F32)


def _head_of_lane(width, group=HEAD_DIM):
    return lax.broadcasted_iota(jnp.int32, (1, width), 1) // group


def _ffn_ln(x, w1_ref, w3_ref, w2_ref, acc_ref, g, b):
    xb = x.astype(BF16)
    acc_ref[...] = jnp.zeros_like(acc_ref)

    def body(c, carry):
        h1 = _dot(xb, w1_ref[c])
        h3 = _dot(xb, w3_ref[c])
        h = (h1 * jax.nn.sigmoid(h1)) * h3
        acc_ref[...] += _dot(h.astype(BF16), w2_ref[c])
        return carry

    lax.fori_loop(0, N_FF_CHUNKS, body, 0, unroll=True)
    return _layernorm(ALPHA * x + 0.5 * acc_ref[...], g, b)


def _ffn_inproj_kernel(x_ref, w1_ref, w3_ref, w2_ref, g_ref, b_ref, win_ref,
                       x1_ref, za_ref, zlru_ref, sq_ref, sk_ref, sv_ref, zml_ref, zg_ref, acc_ref):
    x1 = _ffn_ln(x_ref[...], w1_ref, w3_ref, w2_ref, acc_ref, g_ref[...], b_ref[...])
    x1_ref[...] = x1
    xb = x1.astype(BF16)
    off = 0
    for ref in (za_ref, zlru_ref, sq_ref, sk_ref, sv_ref, zml_ref, zg_ref):
        w = ref.shape[-1]
        ref[...] = _dot(xb, win_ref[:, off:off + w])
        off += w


def _ffn_inproj(x, w1, w3, w2, g, b, win):
    n = x.shape[0]
    tm = TOKEN_TILE
    tile = lambda w: pl.BlockSpec((tm, w), lambda i: (i, 0))
    widths = (D_MODEL, Z_A, Z_LRU, GROUP_W, GROUP_W, GROUP_W, Z_ML, Z_G)
    return pl.pallas_call(
        _ffn_inproj_kernel,
        grid=(n // tm,),
        in_specs=[tile(D_MODEL), _const_spec(w1.shape), _const_spec(w3.shape), _const_spec(w2.shape),
                  _const_spec(g.shape), _const_spec(b.shape), _const_spec(win.shape)],
        out_specs=[tile(w) for w in widths],
        out_shape=[jax.ShapeDtypeStruct((n, w), F32) for w in widths],
        scratch_shapes=[pltpu.VMEM((tm, D_MODEL), F32)],
        compiler_params=pltpu.CompilerParams(dimension_semantics=("arbitrary",),
                                             vmem_limit_bytes=_vmem_limit(56 * 1024 * 1024)),
        name="ffn_inproj",
    )(x, w1, w3, w2, g, b, win)


def _outproj_ffn_kernel(x_ref, ya_ref, yb_ref, yc_ref, yd_ref,
                        on_ref, wout_ref, g1_ref, b1_ref, w1_ref, w3_ref, w2_ref, g2_ref, b2_ref,
                        out_ref, acc_ref):
    y = jnp.zeros(x_ref.shape, F32)
    for gi, y_ref in enumerate((ya_ref, yb_ref, yc_ref, yd_ref)):
        lo = gi * GROUP_W
        yn = _rmsnorm(y_ref[...], on_ref[:, lo:lo + GROUP_W])
        y = y + _dot(yn.astype(BF16), wout_ref[lo:lo + GROUP_W, :])
    x2 = _layernorm(ALPHA * x_ref[...] + y, g1_ref[...], b1_ref[...])
    out_ref[...] = _ffn_ln(x2, w1_ref, w3_ref, w2_ref, acc_ref, g2_ref[...], b2_ref[...])


def _outproj_ffn(x, ys, on, wout, g1, b1, w1, w3, w2, g2, b2):
    n = x.shape[0]
    tm = TOKEN_TILE
    tile = lambda w: pl.BlockSpec((tm, w), lambda i: (i, 0))
    consts = (on, wout, g1, b1, w1, w3, w2, g2, b2)
    return pl.pallas_call(
        _outproj_ffn_kernel,
        grid=(n // tm,),
        in_specs=[tile(D_MODEL)] + [tile(GROUP_W)] * len(ys) + [_const_spec(c.shape) for c in consts],
        out_specs=tile(D_MODEL),
        out_shape=jax.ShapeDtypeStruct((n, D_MODEL), F32),
        scratch_shapes=[pltpu.VMEM((tm, D_MODEL), F32)],
        compiler_params=pltpu.CompilerParams(dimension_semantics=("arbitrary",),
                                             vmem_limit_bytes=_vmem_limit(56 * 1024 * 1024)),
        name="outproj_ffn",
    )(x, *ys, *consts)


def _flash_state_scratch():
    return [pltpu.VMEM((1, ATT_COLS), F32), pltpu.VMEM((1, ATT_COLS), F32), pltpu.VMEM((PAIR_W, ATT_COLS), F32)]


def _flash_reset(state):
    m_ref, l_ref, acc_ref = state
    m_ref[...] = jnp.full(m_ref.shape, NEG_INF, F32)
    l_ref[...] = jnp.zeros_like(l_ref)
    acc_ref[...] = jnp.zeros_like(acc_ref)


def _flash_pipe_scratch():
    return [pltpu.VMEM((2, ATT_T, ATT_COLS), F32), pltpu.VMEM((2, 1, ATT_COLS), F32)]


def _flash_attend(n_steps, qm, key_block, value_block, bias_block, state, pipe, groups):
    m_ref, l_ref, acc_ref = state
    s_ref, mx_ref = pipe
    _flash_reset(state)

    def scores(j, slot):
        bias = bias_block(j)
        s_t = _dot_nt(key_block(j), qm) + jnp.concatenate([bias] * (ATT_COLS // bias.shape[1]), axis=1)
        s_ref[slot] = s_t
        mx_ref[slot] = jnp.max(s_t, axis=0, keepdims=True)

    scores(0, 0)

    def step(j, carry):
        cur = lax.rem(j, 2)
        m = m_ref[...]
        m_new = jnp.maximum(m, mx_ref[cur])
        a = jnp.exp2(m - m_new)
        pr = jnp.exp2(s_ref[cur] - m_new)
        m_ref[...] = m_new
        l_ref[...] = a * l_ref[...] + jnp.sum(pr, axis=0, keepdims=True)
        prb = pr.astype(BF16)
        scores(jnp.minimum(j + 1, n_steps - 1), 1 - cur)
        vt_blk = value_block(j)
        for v_lo, c_lo, c_hi in groups:
            acc_ref[:, c_lo:c_hi] = (a[:, c_lo:c_hi] * acc_ref[:, c_lo:c_hi]
                                     + _dot(vt_blk[v_lo:v_lo + PAIR_W, :], prb[:, c_lo:c_hi]))
        return carry

    lax.fori_loop(0, n_steps, step, 0)


def _flash_result(state):
    _, l_ref, acc_ref = state
    return jnp.transpose(acc_ref[...] * (1.0 / l_ref[...]))


def _merge_pair(o_a, o_b):
    lane = lax.broadcasted_iota(jnp.int32, (1, PAIR_W), 1)
    return jnp.where(lane < HEAD_DIM, o_a, o_b)


def _store_transposed_values(v, vt_ref, j):
    vt_ref[j] = jnp.transpose(v).astype(BF16)


def _mla_kernel(za_ref, qn_ref, kvn_ref, wuq_ref, wukv_ref, cos_ref, sin_ref, o_ref,
                qm_scr, kp_scr, vt_scr, o_scr, bias_scr, m_scr, l_scr, acc_scr, s_scr, mx_scr):
    seq = za_ref.shape[0]
    state = (m_scr, l_scr, acc_scr)
    pipe = (s_scr, mx_scr)
    c_q = (MLA_NOPE + MLA_ROPE) ** -0.5 * LOG2E

    def prep(j, carry):
        rows = pl.ds(pl.multiple_of(j * ATT_T, ATT_T), ATT_T)
        cos, sin = cos_ref[rows, :], sin_ref[rows, :]

        def rope(r):
            return r * cos + pltpu.roll(r, V7X_LANES // 2, axis=1) * sin

        cq = _rmsnorm(za_ref[rows, 0:Q_LORA], qn_ref[...])
        q = _dot(cq.astype(BF16), wuq_ref[...]) * c_q
        for h in range(HEADS):
            lo = h * GROUP_W
            qm_scr[h, rows, 0:V7X_LANES] = q[:, lo:lo + V7X_LANES].astype(BF16)
            qm_scr[h, rows, V7X_LANES:] = rope(q[:, lo + V7X_LANES:lo + GROUP_W]).astype(BF16)
        ckv = _rmsnorm(za_ref[rows, Q_LORA:Q_LORA + KV_LORA], kvn_ref[...])
        kv = _dot(ckv.astype(BF16), wukv_ref[...])
        k_rope = rope(za_ref[rows, Q_LORA + KV_LORA:]).astype(BF16)
        for p in range(2):
            kp_scr[p, rows, 0:V7X_LANES] = kv[:, p * PAIR_W:(p + 1) * PAIR_W].astype(BF16)
            kp_scr[p, rows, V7X_LANES:] = k_rope
        _store_transposed_values(kv[:, GROUP_W:], vt_scr, j)
        return carry

    lax.fori_loop(0, seq // ATT_T, prep, 0)

    key = lax.broadcasted_iota(jnp.int32, (ATT_T, MLA_TQ), 0)
    qry = lax.broadcasted_iota(jnp.int32, (ATT_T, MLA_TQ), 1)
    n_diag = MLA_TQ // ATT_T
    bias_scr[0] = jnp.zeros((ATT_T, MLA_TQ), F32)
    for d in range(n_diag):
        bias_scr[1 + d] = jnp.where(key + d * ATT_T <= qry, 0.0, NEG_INF)
    groups = [(0, 0, ATT_COLS)]

    def q_block(i, carry):
        q0 = pl.multiple_of(i * MLA_TQ, MLA_TQ)
        q_rows = pl.ds(q0, MLA_TQ)
        first_diag = i * n_diag

        def pair(p, c):
            qm = jnp.concatenate([qm_scr[2 * p, q_rows, :], qm_scr[2 * p + 1, q_rows, :]], axis=0)
            v_lo = pl.multiple_of(p * PAIR_W, PAIR_W)
            _flash_attend(
                first_diag + n_diag, qm,
                lambda j: kp_scr[p, pl.ds(pl.multiple_of(j * ATT_T, ATT_T), ATT_T), :],
                lambda j: vt_scr.at[j, pl.ds(v_lo, PAIR_W), :],
                lambda j: bias_scr[jnp.maximum(j - first_diag + 1, 0)],
                state, pipe, groups)
            o_t = _flash_result(state)
            o_scr[p] = _merge_pair(o_t[:MLA_TQ], o_t[MLA_TQ:])
            return c

        lax.fori_loop(0, 2, pair, 0)
        for p in range(2):
            o_ref[q_rows, p * PAIR_W:(p + 1) * PAIR_W] = o_scr[p]
        return carry

    lax.fori_loop(0, seq // MLA_TQ, q_block, 0)


def _mla(za, qn, kvn, wuq, wukv, cos, sin):
    b, s, _ = za.shape
    consts = (qn, kvn, wuq, wukv, cos, sin)
    return pl.pallas_call(
        _mla_kernel,
        grid=(b,),
        in_specs=[pl.BlockSpec((None, s, Z_A), lambda i: (i, 0, 0))] + [_const_spec(c.shape) for c in consts],
        out_specs=pl.BlockSpec((None, s, GROUP_W), lambda i: (i, 0, 0)),
        out_shape=jax.ShapeDtypeStruct((b, s, GROUP_W), F32),
        scratch_shapes=([pltpu.VMEM((HEADS, s, GROUP_W), BF16), pltpu.VMEM((2, s, GROUP_W), BF16),
                         pltpu.VMEM((s // ATT_T, GROUP_W, ATT_T), BF16), pltpu.VMEM((2, MLA_TQ, PAIR_W), F32),
                         pltpu.VMEM((1 + MLA_TQ // ATT_T, ATT_T, MLA_TQ), F32)]
                        + _flash_state_scratch() + _flash_pipe_scratch()),
        compiler_params=pltpu.CompilerParams(dimension_semantics=("arbitrary",),
                                             vmem_limit_bytes=_vmem_limit(48 * 1024 * 1024)),
        name="mla",
    )(za, *consts)


def _swa_kernel(q_ref, k_ref, v_ref, bias_ref, o_ref, qm_scr, k_scr, vt_scr, m_scr, l_scr, acc_scr,
                s_scr, mx_scr):
    seq = q_ref.shape[0]
    state = (m_scr, l_scr, acc_scr)
    pipe = (s_scr, mx_scr)
    head = _head_of_lane(GROUP_W)
    c_q = HEAD_DIM ** -0.5 * LOG2E
    half = ATT_COLS // 2
    groups = [(0, 0, half), (PAIR_W, half, ATT_COLS)]

    def prep(j, carry):
        rows = pl.ds(pl.multiple_of(j * ATT_T, ATT_T), ATT_T)
        q = (q_ref[rows, :] * c_q).astype(BF16)
        for h in range(HEADS):
            qm_scr[h, rows, :] = jnp.where(head == h, q, jnp.zeros_like(q))
        k_scr[rows, :] = k_ref[rows, :].astype(BF16)
        _store_transposed_values(v_ref[rows, :], vt_scr, j)
        return carry

    lax.fori_loop(0, seq // ATT_T, prep, 0)

    def q_block(i, carry):
        q_rows = pl.ds(pl.multiple_of(i * ATT_T, ATT_T), ATT_T)
        qm = jnp.concatenate([qm_scr[h, q_rows, :] for h in range(HEADS)], axis=0)
        _flash_attend(
            i + 1, qm,
            lambda j: k_scr[pl.ds(pl.multiple_of(j * ATT_T, ATT_T), ATT_T), :],
            lambda j: vt_scr.at[j],
            lambda j: bias_ref[jnp.minimum(i - j, N_BIAS_TILES - 1)],
            state, pipe, groups)
        o_t = _flash_result(state)
        for p in range(2):
            lo = 2 * p * ATT_T
            o_ref[q_rows, p * PAIR_W:(p + 1) * PAIR_W] = _merge_pair(o_t[lo:lo + ATT_T], o_t[lo + ATT_T:lo + 2 * ATT_T])
        return carry

    lax.fori_loop(0, seq // ATT_T, q_block, 0)


def _swa_bias_tiles(seq):
    assert DILATED_PATTERNS[-1][0] >= seq - 1, "far tiles assume the widest window covers the sequence"
    assert all(w <= (N_BIAS_TILES - 2) * ATT_T for w, _ in DILATED_PATTERNS[:-1]), "narrow windows must end before the far tile"
    key = np.arange(ATT_T)[:, None]
    qry = np.arange(ATT_T)[None, :]
    tiles = []
    for dist in range(N_BIAS_TILES):
        d = dist * ATT_T + qry - key
        mult = sum(((d >= 0) & (d % dil == 0) & (d <= window)).astype(np.float32) for window, dil in DILATED_PATTERNS)
        with np.errstate(divide="ignore"):
            tiles.append(np.log2(mult))
    return jnp.asarray(np.stack(tiles), F32)


def _swa(q, k, v, bias):
    b, s, w = q.shape
    spec = pl.BlockSpec((None, s, w), lambda i: (i, 0, 0))
    return pl.pallas_call(
        _swa_kernel,
        grid=(b,),
        in_specs=[spec, spec, spec, _const_spec(bias.shape)],
        out_specs=spec,
        out_shape=jax.ShapeDtypeStruct((b, s, w), F32),
        scratch_shapes=([pltpu.VMEM((HEADS, s, w), BF16), pltpu.VMEM((s, w), BF16),
                         pltpu.VMEM((s // ATT_T, w, ATT_T), BF16)] + _flash_state_scratch()
                        + _flash_pipe_scratch()),
        compiler_params=pltpu.CompilerParams(dimension_semantics=("arbitrary",),
                                             vmem_limit_bytes=_vmem_limit(48 * 1024 * 1024)),
        name="swa",
    )(q, k, v, bias)


def _shift_rows(x, d, fill):
    row = lax.broadcasted_iota(jnp.int32, x.shape, 0)
    return jnp.where(row >= d, pltpu.roll(x, d, axis=0), fill)


def _softplus(x):
    return jnp.maximum(x, 0.0) + jnp.log1p(jnp.exp(-jnp.abs(x)))


def _gelu_tanh(x):
    c = np.sqrt(2.0 / np.pi).astype(np.float32)
    return 0.5 * x * (1.0 + jnp.tanh(c * (x + 0.044715 * (x * x * x))))


def _rglru_kernel(z_ref, cw_ref, cb_ref, wa_ref, ba_ref, wx_ref, bx_ref, lam_ref, o_ref):
    seq = z_ref.shape[0]
    x = z_ref[:, 0:GROUP_W]
    xc = cb_ref[...] + cw_ref[CONV_W - 1:CONV_W, :] * x
    for j in range(CONV_W - 1):
        xc = xc + cw_ref[j:j + 1, :] * _shift_rows(x, CONV_W - 1 - j, 0.0)
    xcb = xc.astype(BF16)
    r = jax.nn.sigmoid(_dot(xcb, wa_ref[...]) + ba_ref[...])
    i = jax.nn.sigmoid(_dot(xcb, wx_ref[...]) + bx_ref[...])
    log_a = -LRU_C * r * _softplus(-lam_ref[...])
    a = jnp.exp(log_a)
    u = jnp.sqrt(-jnp.tanh(log_a) * (a * a + 1.0)) * (i * xc)
    d = 1
    while d < seq:
        u = a * _shift_rows(u, d, 0.0) + u
        a = a * _shift_rows(a, d, 1.0)
        d *= 2
    o_ref[...] = u * _gelu_tanh(z_ref[:, GROUP_W:])


def _rglru(z, cw, cb, wa, ba, wx, bx, lam):
    b, s, _ = z.shape
    consts = (cw, cb, wa, ba, wx, bx, lam)
    return pl.pallas_call(
        _rglru_kernel,
        grid=(b,),
        in_specs=[pl.BlockSpec((None, s, Z_LRU), lambda i: (i, 0, 0))] + [_const_spec(c.shape) for c in consts],
        out_specs=pl.BlockSpec((None, s, GROUP_W), lambda i: (i, 0, 0)),
        out_shape=jax.ShapeDtypeStruct((b, s, GROUP_W), F32),
        compiler_params=pltpu.CompilerParams(dimension_semantics=("arbitrary",),
                                             vmem_limit_bytes=_vmem_limit(48 * 1024 * 1024)),
        name="rglru",
    )(z, *consts)


def _expand_heads(cols, head):
    out = jnp.zeros((cols.shape[0], GROUP_W), F32)
    for h in range(HEADS):
        out = out + jnp.where(head == h, cols[:, h:h + 1], 0.0)
    return out


def _mlstm_kernel(z_ref, g_ref, bi_ref, bf_ref, o_ref, c_scr, n_scr, m_scr):
    seq = z_ref.shape[0]
    lc = ML_CHUNK
    head = _head_of_lane(GROUP_W)
    row_head = lax.broadcasted_iota(jnp.int32, (GROUP_W, GROUP_W), 0) // HEAD_DIM
    same_head = row_head == head
    tri = lax.broadcasted_iota(jnp.int32, (lc, lc), 0) >= lax.broadcasted_iota(jnp.int32, (lc, lc), 1)
    c_scr[...] = jnp.zeros_like(c_scr)
    n_scr[...] = jnp.zeros_like(n_scr)
    m_scr[...] = jnp.zeros_like(m_scr)

    def chunk(ci, carry):
        t0 = pl.multiple_of(ci * lc, lc)
        rows = pl.ds(t0, lc)
        q = z_ref[rows, 0:GROUP_W]
        k = z_ref[rows, GROUP_W:2 * GROUP_W] * (HEAD_DIM ** -0.5)
        v = z_ref[rows, 2 * GROUP_W:3 * GROUP_W]
        og = z_ref[rows, 3 * GROUP_W:4 * GROUP_W]
        gates = g_ref[rows, :]
        log_i = _expand_heads(gates[:, 0:HEADS], head) + bi_ref[...]
        f_pre = _expand_heads(gates[:, HEADS:2 * HEADS], head) + bf_ref[...]
        log_f = jnp.minimum(f_pre, 0.0) - jnp.log1p(jnp.exp(-jnp.abs(f_pre)))
        bcum = log_f
        d = 1
        while d < lc:
            bcum = bcum + _shift_rows(bcum, d, 0.0)
            d *= 2
        g_tot = bcum[lc - 1:lc, :]
        m_prev = m_scr[...]
        c_prev = c_scr[...]
        n_prev = n_scr[...]
        qb, kb, vb = q.astype(BF16), k.astype(BF16), v.astype(BF16)
        qs = jnp.concatenate([jnp.where(head == h, qb, jnp.zeros_like(qb)) for h in range(HEADS)], axis=0)
        s_all = _dot_nt(qs, kb)
        q_c = _dot(qb, c_prev.astype(BF16))
        qn = q * n_prev
        key_row = jnp.transpose(log_i - bcum)
        inter = bcum + m_prev
        out = jnp.zeros((lc, GROUP_W), F32)
        for h in range(HEADS):
            lane0 = h * HEAD_DIM
            dmat = jnp.where(tri, bcum[:, lane0:lane0 + 1] + key_row[lane0:lane0 + 1, :], NEG_INF)
            inter_h = inter[:, lane0:lane0 + 1]
            mt = jnp.maximum(jnp.max(dmat, axis=-1, keepdims=True), inter_h)
            sw = s_all[h * lc:(h + 1) * lc] * jnp.exp(dmat - mt)
            w_inter = jnp.exp(inter_h - mt)
            num = _dot(sw.astype(BF16), vb) + w_inter * q_c
            qn_h = jnp.sum(jnp.where(head == h, qn, 0.0), axis=-1, keepdims=True)
            den = jnp.sum(sw, axis=-1, keepdims=True) + w_inter * qn_h
            hh = num / jnp.maximum(jnp.abs(den), jnp.exp(-mt))
            out = out + jnp.where(head == h, hh, 0.0)
        o_ref[rows, :] = jax.nn.sigmoid(og) * out
        wk = g_tot - bcum + log_i
        m_new = jnp.maximum(g_tot + m_prev, jnp.max(wk, axis=0, keepdims=True))
        decay = jnp.exp(g_tot + m_prev - m_new)
        kw = k * jnp.exp(wk - m_new)
        kv = _dot(jnp.transpose(kw).astype(BF16), vb)
        c_scr[...] = decay * c_prev + jnp.where(same_head, kv, 0.0)
        n_scr[...] = decay * n_prev + jnp.sum(kw, axis=0, keepdims=True)
        m_scr[...] = m_new
        return carry

    lax.fori_loop(0, seq // lc, chunk, 0)


def _mlstm(z, gates, bi, bf):
    b, s, _ = z.shape
    return pl.pallas_call(
        _mlstm_kernel,
        grid=(b,),
        in_specs=[pl.BlockSpec((None, s, Z_ML), lambda i: (i, 0, 0)),
                  pl.BlockSpec((None, s, Z_G), lambda i: (i, 0, 0)),
                  _const_spec(bi.shape), _const_spec(bf.shape)],
        out_specs=pl.BlockSpec((None, s, GROUP_W), lambda i: (i, 0, 0)),
        out_shape=jax.ShapeDtypeStruct((b, s, GROUP_W), F32),
        scratch_shapes=[pltpu.VMEM((GROUP_W, GROUP_W), F32), pltpu.VMEM((1, GROUP_W), F32),
                        pltpu.VMEM((1, GROUP_W), F32)],
        compiler_params=pltpu.CompilerParams(dimension_semantics=("arbitrary",),
                                             vmem_limit_bytes=_vmem_limit(48 * 1024 * 1024)),
        name="mlstm",
    )(z, gates, bi, bf)


def _chunk_cols(w):
    return w.reshape(w.shape[0], N_FF_CHUNKS, FF_CHUNK).transpose(1, 0, 2).astype(BF16)


def _chunk_rows(w):
    return w.reshape(N_FF_CHUNKS, FF_CHUNK, w.shape[1]).astype(BF16)


def _block_diag(w):
    out = jnp.zeros((GROUP_W, GROUP_W), w.dtype)
    for h in range(HEADS):
        out = out.at[h * HEAD_DIM:(h + 1) * HEAD_DIM, h * HEAD_DIM:(h + 1) * HEAD_DIM].set(w[h])
    return out


def _permute_w_in(w):
    kr = Q_LORA + KV_LORA
    gap = jnp.zeros((w.shape[0], HEAD_DIM - ROPE_HALF), w.dtype)
    body_lo = kr + MLA_ROPE
    body_hi = body_lo + 9 * GROUP_W
    pad = jnp.zeros((w.shape[0], Z_G - 2 * HEADS), w.dtype)
    return jnp.concatenate([w[:, :kr], w[:, kr:kr + ROPE_HALF], gap, w[:, kr + ROPE_HALF:kr + MLA_ROPE], gap,
                            w[:, body_lo:body_hi], w[:, body_hi:], pad], axis=1).astype(BF16)


def _mla_up_layout():
    per_q = MLA_NOPE + MLA_ROPE
    q_src, q_dst = [], []
    for h in range(HEADS):
        base = h * GROUP_W
        nope_lo = base + (h % 2) * HEAD_DIM
        q_src += [h * per_q + i for i in range(per_q)]
        q_dst += [nope_lo + i for i in range(MLA_NOPE)]
        q_dst += [base + V7X_LANES + i for i in range(ROPE_HALF)]
        q_dst += [base + V7X_LANES + HEAD_DIM + i for i in range(ROPE_HALF)]
    per_kv = MLA_NOPE + HEAD_DIM
    kn = [h * per_kv + i for h in range(HEADS) for i in range(MLA_NOPE)]
    vv = [h * per_kv + MLA_NOPE + i for h in range(HEADS) for i in range(HEAD_DIM)]
    return np.array(q_src), np.array(q_dst), np.array(kn + vv)


def _rope_tables(seq):
    freqs = ROPE_THETA ** (-jnp.arange(ROPE_HALF, dtype=F32) / ROPE_HALF)
    ang = jnp.arange(seq, dtype=F32)[:, None] * freqs[None, :]
    cos = jnp.tile(jnp.cos(ang), (1, 2 * HEADS))
    sin = jnp.tile(jnp.sin(ang), (1, HEADS))
    return cos, jnp.concatenate([-sin, sin], axis=1)


def kernel(x, ln_g, ln_b, ffn_w1, ffn_w3, ffn_w2, w_in, mla_q_norm, mla_kv_norm, mla_w_uq, mla_w_ukv,
           lru_conv_w, lru_conv_b, lru_w_a, lru_b_a, lru_w_x, lru_b_x, lru_lambda, ml_b_i, ml_b_f,
           out_norm, w_out):
    bsz, seq, d = x.shape
    n = bsz * seq
    row = lambda t: t.reshape(1, -1)
    q_src, q_dst, kv_perm = _mla_up_layout()
    cos, sin = _rope_tables(seq)
    swa_bias = _swa_bias_tiles(seq)
    xt = x.reshape(n, d)
    for l in range(DEPTH):
        ffn = lambda j: (_chunk_cols(ffn_w1[l, j]), _chunk_cols(ffn_w3[l, j]), _chunk_rows(ffn_w2[l, j]))
        x1, za, zlru, sq, sk, sv, zml, zg = _ffn_inproj(
            xt, *ffn(0), row(ln_g[l, 0]), row(ln_b[l, 0]), _permute_w_in(w_in[l]))
        seqv = lambda t: t.reshape(bsz, seq, t.shape[-1])
        wuq = jnp.zeros((Q_LORA, HEADS * GROUP_W), F32).at[:, q_dst].set(mla_w_uq[l][:, q_src]).astype(BF16)
        y_a = _mla(seqv(za), row(mla_q_norm[l]), row(mla_kv_norm[l]), wuq,
                   mla_w_ukv[l][:, kv_perm].astype(BF16), cos, sin)
        y_b = _rglru(seqv(zlru), lru_conv_w[l], row(lru_conv_b[l]),
                     _block_diag(lru_w_a[l]).astype(BF16), row(lru_b_a[l]),
                     _block_diag(lru_w_x[l]).astype(BF16), row(lru_b_x[l]), row(lru_lambda[l]))
        y_c = _swa(seqv(sq), seqv(sk), seqv(sv), swa_bias)
        y_d = _mlstm(seqv(zml), seqv(zg), row(jnp.repeat(ml_b_i[l], HEAD_DIM)), row(jnp.repeat(ml_b_f[l], HEAD_DIM)))
        ys = [t.reshape(n, GROUP_W) for t in (y_a, y_b, y_c, y_d)]
        xt = _outproj_ffn(x1, ys, row(out_norm[l]), w_out[l].astype(BF16), row(ln_g[l, 1]), row(ln_b[l, 1]),
                          *ffn(1), row(ln_g[l, 2]), row(ln_b[l, 2]))
    return xt.reshape(bsz, seq, d)
```

```python
import numpy as np
import jax
import jax.numpy as jnp
from jax import lax
from jax.experimental import pallas as pl
from jax.experimental.pallas import tpu as pltpu

F32 = jnp.float32
BF16 = jnp.bfloat16

D_MODEL = 1024
DEPTH = 2
GROUP_W = 256
HEAD_DIM = 64
HEADS = 4
D_FF = 2816
ALPHA = (2.0 * DEPTH) ** 0.25
Q_LORA = 256
KV_LORA = 128
MLA_NOPE = 64
MLA_ROPE = 32
ROPE_HALF = MLA_ROPE // 2
ROPE_THETA = 10000.0
CONV_W = 4
LRU_C = 8.0
DILATED_PATTERNS = ((128, 1), (512, 4), (2048, 16))

V7X_LANES = 128
V7X_VMEM_BYTES = 64 * 1024 * 1024

FF_CHUNK = 256
N_FF_CHUNKS = D_FF // FF_CHUNK
TOKEN_TILE = 512
ML_CHUNK = 128
ATT_T = 256
ATT_COLS = 1024
MLA_TQ = 512
PAIR_W = 2 * HEAD_DIM
N_BIAS_TILES = 4

Z_A = 512
Z_LRU = 512
Z_ML = 1024
Z_G = 256

NEG_INF = float("-inf")
LOG2E = float(np.log2(np.e))


def _vmem_limit(nbytes):
    return int(min(nbytes, V7X_VMEM_BYTES - 6 * 1024 * 1024))


def _const_spec(shape):
    nd = len(shape)
    return pl.BlockSpec(shape, lambda *_: (0,) * nd, pipeline_mode=pl.Buffered(1))


def _layernorm(v, g, b):
    mu = jnp.mean(v, axis=-1, keepdims=True)
    d = v - mu
    var = jnp.mean(d * d, axis=-1, keepdims=True)
    return d * lax.rsqrt(var + 1e-5) * g + b


def _rmsnorm(v, g, eps=1e-6):
    return v * lax.rsqrt(jnp.mean(v * v, axis=-1, keepdims=True) + eps) * g


def _dot(a, b):
    return jnp.dot(a, b, preferred_element_type=F32)


def _dot_nt(a, b):
    return lax.dot_general(a, b, (((1,), (1,)), ((), ())), preferred_element_type=F32)


def _head_of_lane(width, group=HEAD_DIM):
    return lax.broadcasted_iota(jnp.int32, (1, width), 1) // group


def _ffn_ln(x, w1_ref, w3_ref, w2_ref, acc_ref, g, b):
    xb = x.astype(BF16)
    acc_ref[...] = jnp.zeros_like(acc_ref)

    def body(c, carry):
        h1 = _dot(xb, w1_ref[c])
        h3 = _dot(xb, w3_ref[c])
        h = (h1 * jax.nn.sigmoid(h1)) * h3
        acc_ref[...] += _dot(h.astype(BF16), w2_ref[c])
        return carry

    lax.fori_loop(0, N_FF_CHUNKS, body, 0, unroll=True)
    return _layernorm(ALPHA * x + 0.5 * acc_ref[...], g, b)


def _ffn_inproj_kernel(x_ref, w1_ref, w3_ref, w2_ref, g_ref, b_ref, win_ref,
                       x1_ref, za_ref, zlru_ref, sq_ref, sk_ref, sv_ref, zml_ref, zg_ref, acc_ref):
    x1 = _ffn_ln(x_ref[...], w1_ref, w3_ref, w2_ref, acc_ref, g_ref[...], b_ref[...])
    x1_ref[...] = x1
    xb = x1.astype(BF16)
    off = 0
    for ref in (za_ref, zlru_ref, sq_ref, sk_ref, sv_ref, zml_ref, zg_ref):
        w = ref.shape[-1]
        ref[...] = _dot(xb, win_ref[:, off:off + w])
        off += w


def _ffn_inproj(x, w1, w3, w2, g, b, win):
    n = x.shape[0]
    tm = TOKEN_TILE
    tile = lambda w: pl.BlockSpec((tm, w), lambda i: (i, 0))
    widths = (D_MODEL, Z_A, Z_LRU, GROUP_W, GROUP_W, GROUP_W, Z_ML, Z_G)
    return pl.pallas_call(
        _ffn_inproj_kernel,
        grid=(n // tm,),
        in_specs=[tile(D_MODEL), _const_spec(w1.shape), _const_spec(w3.shape), _const_spec(w2.shape),
                  _const_spec(g.shape), _const_spec(b.shape), _const_spec(win.shape)],
        out_specs=[tile(w) for w in widths],
        out_shape=[jax.ShapeDtypeStruct((n, w), F32) for w in widths],
        scratch_shapes=[pltpu.VMEM((tm, D_MODEL), F32)],
        compiler_params=pltpu.CompilerParams(dimension_semantics=("arbitrary",),
                                             vmem_limit_bytes=_vmem_limit(56 * 1024 * 1024)),
        name="ffn_inproj",
    )(x, w1, w3, w2, g, b, win)


def _outproj_ffn_kernel(x_ref, ya_ref, yb_ref, yc_ref, yd_ref,
                        on_ref, wout_ref, g1_ref, b1_ref, w1_ref, w3_ref, w2_ref, g2_ref, b2_ref,
                        out_ref, acc_ref):
    y = jnp.zeros(x_ref.shape, F32)
    for gi, y_ref in enumerate((ya_ref, yb_ref, yc_ref, yd_ref)):
        lo = gi * GROUP_W
        yn = _rmsnorm(y_ref[...], on_ref[:, lo:lo + GROUP_W])
        y = y + _dot(yn.astype(BF16), wout_ref[lo:lo + GROUP_W, :])
    x2 = _layernorm(ALPHA * x_ref[...] + y, g1_ref[...], b1_ref[...])
    out_ref[...] = _ffn_ln(x2, w1_ref, w3_ref, w2_ref, acc_ref, g2_ref[...], b2_ref[...])


def _outproj_ffn(x, ys, on, wout, g1, b1, w1, w3, w2, g2, b2):
    n = x.shape[0]
    tm = TOKEN_TILE
    tile = lambda w: pl.BlockSpec((tm, w), lambda i: (i, 0))
    consts = (on, wout, g1, b1, w1, w3, w2, g2, b2)
    return pl.pallas_call(
        _outproj_ffn_kernel,
        grid=(n // tm,),
        in_specs=[tile(D_MODEL)] + [tile(GROUP_W)] * len(ys) + [_const_spec(c.shape) for c in consts],
        out_specs=tile(D_MODEL),
        out_shape=jax.ShapeDtypeStruct((n, D_MODEL), F32),
        scratch_shapes=[pltpu.VMEM((tm, D_MODEL), F32)],
        compiler_params=pltpu.CompilerParams(dimension_semantics=("arbitrary",),
                                             vmem_limit_bytes=_vmem_limit(56 * 1024 * 1024)),
        name="outproj_ffn",
    )(x, *ys, *consts)


def _flash_state_scratch():
    return [pltpu.VMEM((1, ATT_COLS), F32), pltpu.VMEM((1, ATT_COLS), F32), pltpu.VMEM((PAIR_W, ATT_COLS), F32)]


def _flash_reset(state):
    m_ref, l_ref, acc_ref = state
    m_ref[...] = jnp.full(m_ref.shape, NEG_INF, F32)
    l_ref[...] = jnp.zeros_like(l_ref)
    acc_ref[...] = jnp.zeros_like(acc_ref)


def _flash_pipe_scratch():
    return [pltpu.VMEM((2, ATT_T, ATT_COLS), F32), pltpu.VMEM((2, 1, ATT_COLS), F32)]


def _flash_attend(n_steps, qm, key_block, value_block, bias_block, state, pipe, groups):
    m_ref, l_ref, acc_ref = state
    s_ref, mx_ref = pipe
    _flash_reset(state)

    def scores(j, slot):
        bias = bias_block(j)
        s_t = _dot_nt(key_block(j), qm) + jnp.concatenate([bias] * (ATT_COLS // bias.shape[1]), axis=1)
        s_ref[slot] = s_t
        mx_ref[slot] = jnp.max(s_t, axis=0, keepdims=True)

    scores(0, 0)

    def step(j, carry):
        cur = lax.rem(j, 2)
        m = m_ref[...]
        m_new = jnp.maximum(m, mx_ref[cur])
        a = jnp.exp2(m - m_new)
        pr = jnp.exp2(s_ref[cur] - m_new)
        m_ref[...] = m_new
        l_ref[...] = a * l_ref[...] + jnp.sum(pr, axis=0, keepdims=True)
        prb = pr.astype(BF16)
        scores(jnp.minimum(j + 1, n_steps - 1), 1 - cur)
        vt_blk = value_block(j)
        for v_lo, c_lo, c_hi in groups:
            acc_ref[:, c_lo:c_hi] = (a[:, c_lo:c_hi] * acc_ref[:, c_lo:c_hi]
                                     + _dot(vt_blk[v_lo:v_lo + PAIR_W, :], prb[:, c_lo:c_hi]))
        return carry

    lax.fori_loop(0, n_steps, step, 0)


def _flash_result(state):
    _, l_ref, acc_ref = state
    return jnp.transpose(acc_ref[...] * (1.0 / l_ref[...]))


def _merge_pair(o_a, o_b):
    lane = lax.broadcasted_iota(jnp.int32, (1, PAIR_W), 1)
    return jnp.where(lane < HEAD_DIM, o_a, o_b)


def _store_transposed_values(v, vt_ref, j):
    vt_ref[j] = jnp.transpose(v).astype(BF16)


def _mla_kernel(za_ref, qn_ref, kvn_ref, wuq_ref, wukv_ref, cos_ref, sin_ref, o_ref,
                qm_scr, kp_scr, vt_scr, o_scr, bias_scr, m_scr, l_scr, acc_scr, s_scr, mx_scr):
    seq = za_ref.shape[0]
    state = (m_scr, l_scr, acc_scr)
    pipe = (s_scr, mx_scr)
    c_q = (MLA_NOPE + MLA_ROPE) ** -0.5 * LOG2E

    def prep(j, carry):
        rows = pl.ds(pl.multiple_of(j * ATT_T, ATT_T), ATT_T)
        cos, sin = cos_ref[rows, :], sin_ref[rows, :]

        def rope(r):
            return r * cos + pltpu.roll(r, V7X_LANES // 2, axis=1) * sin

        cq = _rmsnorm(za_ref[rows, 0:Q_LORA], qn_ref[...])
        q = _dot(cq.astype(BF16), wuq_ref[...]) * c_q
        for h in range(HEADS):
            lo = h * GROUP_W
            qm_scr[h, rows, 0:V7X_LANES] = q[:, lo:lo + V7X_LANES].astype(BF16)
            qm_scr[h, rows, V7X_LANES:] = rope(q[:, lo + V7X_LANES:lo + GROUP_W]).astype(BF16)
        ckv = _rmsnorm(za_ref[rows, Q_LORA:Q_LORA + KV_LORA], kvn_ref[...])
        kv = _dot(ckv.astype(BF16), wukv_ref[...])
        k_rope = rope(za_ref[rows, Q_LORA + KV_LORA:]).astype(BF16)
        for p in range(2):
            kp_scr[p, rows, 0:V7X_LANES] = kv[:, p * PAIR_W:(p + 1) * PAIR_W].astype(BF16)
            kp_scr[p, rows, V7X_LANES:] = k_rope
        _store_transposed_values(kv[:, GROUP_W:], vt_scr, j)
        return carry

    lax.fori_loop(0, seq // ATT_T, prep, 0)

    key = lax.broadcasted_iota(jnp.int32, (ATT_T, MLA_TQ), 0)
    qry = lax.broadcasted_iota(jnp.int32, (ATT_T, MLA_TQ), 1)
    n_diag = MLA_TQ // ATT_T
    bias_scr[0] = jnp.zeros((ATT_T, MLA_TQ), F32)
    for d in range(n_diag):
        bias_scr[1 + d] = jnp.where(key + d * ATT_T <= qry, 0.0, NEG_INF)
    groups = [(0, 0, ATT_COLS)]

    def q_block(i, carry):
        q0 = pl.multiple_of(i * MLA_TQ, MLA_TQ)
        q_rows = pl.ds(q0, MLA_TQ)
        first_diag = i * n_diag

        def pair(p, c):
            qm = jnp.concatenate([qm_scr[2 * p, q_rows, :], qm_scr[2 * p + 1, q_rows, :]], axis=0)
            v_lo = pl.multiple_of(p * PAIR_W, PAIR_W)
            _flash_attend(
                first_diag + n_diag, qm,
                lambda j: kp_scr[p, pl.ds(pl.multiple_of(j * ATT_T, ATT_T), ATT_T), :],
                lambda j: vt_scr.at[j, pl.ds(v_lo, PAIR_W), :],
                lambda j: bias_scr[jnp.maximum(j - first_diag + 1, 0)],
                state, pipe, groups)
            o_t = _flash_result(state)
            o_scr[p] = _merge_pair(o_t[:MLA_TQ], o_t[MLA_TQ:])
            return c

        lax.fori_loop(0, 2, pair, 0)
        for p in range(2):
            o_ref[q_rows, p * PAIR_W:(p + 1) * PAIR_W] = o_scr[p]
        return carry

    lax.fori_loop(0, seq // MLA_TQ, q_block, 0)


def _mla(za, qn, kvn, wuq, wukv, cos, sin):
    b, s, _ = za.shape
    consts = (qn, kvn, wuq, wukv, cos, sin)
    return pl.pallas_call(
        _mla_kernel,
        grid=(b,),
        in_specs=[pl.BlockSpec((None, s, Z_A), lambda i: (i, 0, 0))] + [_const_spec(c.shape) for c in consts],
        out_specs=pl.BlockSpec((None, s, GROUP_W), lambda i: (i, 0, 0)),
        out_shape=jax.ShapeDtypeStruct((b, s, GROUP_W), F32),
        scratch_shapes=([pltpu.VMEM((HEADS, s, GROUP_W), BF16), pltpu.VMEM((2, s, GROUP_W), BF16),
                         pltpu.VMEM((s // ATT_T, GROUP_W, ATT_T), BF16), pltpu.VMEM((2, MLA_TQ, PAIR_W), F32),
                         pltpu.VMEM((1 + MLA_TQ // ATT_T, ATT_T, MLA_TQ), F32)]
                        + _flash_state_scratch() + _flash_pipe_scratch()),
        compiler_params=pltpu.CompilerParams(dimension_semantics=("arbitrary",),
                                             vmem_limit_bytes=_vmem_limit(48 * 1024 * 1024)),
        name="mla",
    )(za, *consts)


def _swa_kernel(q_ref, k_ref, v_ref, bias_ref, o_ref, qm_scr, k_scr, vt_scr, m_scr, l_scr, acc_scr,
                s_scr, mx_scr):
    seq = q_ref.shape[0]
    state = (m_scr, l_scr, acc_scr)
    pipe = (s_scr, mx_scr)
    head = _head_of_lane(GROUP_W)
    c_q = HEAD_DIM ** -0.5 * LOG2E
    half = ATT_COLS // 2
    groups = [(0, 0, half), (PAIR_W, half, ATT_COLS)]

    def prep(j, carry):
        rows = pl.ds(pl.multiple_of(j * ATT_T, ATT_T), ATT_T)
        q = (q_ref[rows, :] * c_q).astype(BF16)
        for h in range(HEADS):
            qm_scr[h, rows, :] = jnp.where(head == h, q, jnp.zeros_like(q))
        k_scr[rows, :] = k_ref[rows, :].astype(BF16)
        _store_transposed_values(v_ref[rows, :], vt_scr, j)
        return carry

    lax.fori_loop(0, seq // ATT_T, prep, 0)

    def q_block(i, carry):
        q_rows = pl.ds(pl.multiple_of(i * ATT_T, ATT_T), ATT_T)
        qm = jnp.concatenate([qm_scr[h, q_rows, :] for h in range(HEADS)], axis=0)
        _flash_attend(
            i + 1, qm,
            lambda j: k_scr[pl.ds(pl.multiple_of(j * ATT_T, ATT_T), ATT_T), :],
            lambda j: vt_scr.at[j],
            lambda j: bias_ref[jnp.minimum(i - j, N_BIAS_TILES - 1)],
            state, pipe, groups)
        o_t = _flash_result(state)
        for p in range(2):
            lo = 2 * p * ATT_T
            o_ref[q_rows, p * PAIR_W:(p + 1) * PAIR_W] = _merge_pair(o_t[lo:lo + ATT_T], o_t[lo + ATT_T:lo + 2 * ATT_T])
        return carry

    lax.fori_loop(0, seq // ATT_T, q_block, 0)


def _swa_bias_tiles(seq):
    assert DILATED_PATTERNS[-1][0] >= seq - 1, "far tiles assume the widest window covers the sequence"
    assert all(w <= (N_BIAS_TILES - 2) * ATT_T for w, _ in DILATED_PATTERNS[:-1]), "narrow windows must end before the far tile"
    key = np.arange(ATT_T)[:, None]
    qry = np.arange(ATT_T)[None, :]
    tiles = []
    for dist in range(N_BIAS_TILES):
        d = dist * ATT_T + qry - key
        mult = sum(((d >= 0) & (d % dil == 0) & (d <= window)).astype(np.float32) for window, dil in DILATED_PATTERNS)
        with np.errstate(divide="ignore"):
            tiles.append(np.log2(mult))
    return jnp.asarray(np.stack(tiles), F32)


def _swa(q, k, v, bias):
    b, s, w = q.shape
    spec = pl.BlockSpec((None, s, w), lambda i: (i, 0, 0))
    return pl.pallas_call(
        _swa_kernel,
        grid=(b,),
        in_specs=[spec, spec, spec, _const_spec(bias.shape)],
        out_specs=spec,
        out_shape=jax.ShapeDtypeStruct((b, s, w), F32),
        scratch_shapes=([pltpu.VMEM((HEADS, s, w), BF16), pltpu.VMEM((s, w), BF16),
                         pltpu.VMEM((s // ATT_T, w, ATT_T), BF16)] + _flash_state_scratch()
                        + _flash_pipe_scratch()),
        compiler_params=pltpu.CompilerParams(dimension_semantics=("arbitrary",),
                                             vmem_limit_bytes=_vmem_limit(48 * 1024 * 1024)),
        name="swa",
    )(q, k, v, bias)


def _shift_rows(x, d, fill):
    row = lax.broadcasted_iota(jnp.int32, x.shape, 0)
    return jnp.where(row >= d, pltpu.roll(x, d, axis=0), fill)


def _softplus(x):
    return jnp.maximum(x, 0.0) + jnp.log1p(jnp.exp(-jnp.abs(x)))


def _gelu_tanh(x):
    c = np.sqrt(2.0 / np.pi).astype(np.float32)
    return 0.5 * x * (1.0 + jnp.tanh(c * (x + 0.044715 * (x * x * x))))


def _rglru_kernel(z_ref, cw_ref, cb_ref, wa_ref, ba_ref, wx_ref, bx_ref, lam_ref, o_ref):
    seq = z_ref.shape[0]
    x = z_ref[:, 0:GROUP_W]
    xc = cb_ref[...] + cw_ref[CONV_W - 1:CONV_W, :] * x
    for j in range(CONV_W - 1):
        xc = xc + cw_ref[j:j + 1, :] * _shift_rows(x, CONV_W - 1 - j, 0.0)
    xcb = xc.astype(BF16)
    r = jax.nn.sigmoid(_dot(xcb, wa_ref[...]) + ba_ref[...])
    i = jax.nn.sigmoid(_dot(xcb, wx_ref[...]) + bx_ref[...])
    log_a = -LRU_C * r * _softplus(-lam_ref[...])
    a = jnp.exp(log_a)
    u = jnp.sqrt(-jnp.tanh(log_a) * (a * a + 1.0)) * (i * xc)
    d = 1
    while d < seq:
        u = a * _shift_rows(u, d, 0.0) + u
        a = a * _shift_rows(a, d, 1.0)
        d *= 2
    o_ref[...] = u * _gelu_tanh(z_ref[:, GROUP_W:])


def _rglru(z, cw, cb, wa, ba, wx, bx, lam):
    b, s, _ = z.shape
    consts = (cw, cb, wa, ba, wx, bx, lam)
    return pl.pallas_call(
        _rglru_kernel,
        grid=(b,),
        in_specs=[pl.BlockSpec((None, s, Z_LRU), lambda i: (i, 0, 0))] + [_const_spec(c.shape) for c in consts],
        out_specs=pl.BlockSpec((None, s, GROUP_W), lambda i: (i, 0, 0)),
        out_shape=jax.ShapeDtypeStruct((b, s, GROUP_W), F32),
        compiler_params=pltpu.CompilerParams(dimension_semantics=("arbitrary",),
                                             vmem_limit_bytes=_vmem_limit(48 * 1024 * 1024)),
        name="rglru",
    )(z, *consts)


def _expand_heads(cols, head):
    out = jnp.zeros((cols.shape[0], GROUP_W), F32)
    for h in range(HEADS):
        out = out + jnp.where(head == h, cols[:, h:h + 1], 0.0)
    return out


def _mlstm_kernel(z_ref, g_ref, bi_ref, bf_ref, o_ref, c_scr, n_scr, m_scr):
    seq = z_ref.shape[0]
    lc = ML_CHUNK
    head = _head_of_lane(GROUP_W)
    row_head = lax.broadcasted_iota(jnp.int32, (GROUP_W, GROUP_W), 0) // HEAD_DIM
    same_head = row_head == head
    tri = lax.broadcasted_iota(jnp.int32, (lc, lc), 0) >= lax.broadcasted_iota(jnp.int32, (lc, lc), 1)
    c_scr[...] = jnp.zeros_like(c_scr)
    n_scr[...] = jnp.zeros_like(n_scr)
    m_scr[...] = jnp.zeros_like(m_scr)

    def chunk(ci, carry):
        t0 = pl.multiple_of(ci * lc, lc)
        rows = pl.ds(t0, lc)
        q = z_ref[rows, 0:GROUP_W]
        k = z_ref[rows, GROUP_W:2 * GROUP_W] * (HEAD_DIM ** -0.5)
        v = z_ref[rows, 2 * GROUP_W:3 * GROUP_W]
        og = z_ref[rows, 3 * GROUP_W:4 * GROUP_W]
        gates = g_ref[rows, :]
        log_i = _expand_heads(gates[:, 0:HEADS], head) + bi_ref[...]
        f_pre = _expand_heads(gates[:, HEADS:2 * HEADS], head) + bf_ref[...]
        log_f = jnp.minimum(f_pre, 0.0) - jnp.log1p(jnp.exp(-jnp.abs(f_pre)))
        bcum = log_f
        d = 1
        while d < lc:
            bcum = bcum + _shift_rows(bcum, d, 0.0)
            d *= 2
        g_tot = bcum[lc - 1:lc, :]
        m_prev = m_scr[...]
        c_prev = c_scr[...]
        n_prev = n_scr[...]
        qb, kb, vb = q.astype(BF16), k.astype(BF16), v.astype(BF16)
        qs = jnp.concatenate([jnp.where(head == h, qb, jnp.zeros_like(qb)) for h in range(HEADS)], axis=0)
        s_all = _dot_nt(qs, kb)
        q_c = _dot(qb, c_prev.astype(BF16))
        qn = q * n_prev
        key_row = jnp.transpose(log_i - bcum)
        inter = bcum + m_prev
        out = jnp.zeros((lc, GROUP_W), F32)
        for h in range(HEADS):
            lane0 = h * HEAD_DIM
            dmat = jnp.where(tri, bcum[:, lane0:lane0 + 1] + key_row[lane0:lane0 + 1, :], NEG_INF)
            inter_h = inter[:, lane0:lane0 + 1]
            mt = jnp.maximum(jnp.max(dmat, axis=-1, keepdims=True), inter_h)
            sw = s_all[h * lc:(h + 1) * lc] * jnp.exp(dmat - mt)
            w_inter = jnp.exp(inter_h - mt)
            num = _dot(sw.astype(BF16), vb) + w_inter * q_c
            qn_h = jnp.sum(jnp.where(head == h, qn, 0.0), axis=-1, keepdims=True)
            den = jnp.sum(sw, axis=-1, keepdims=True) + w_inter * qn_h
            hh = num / jnp.maximum(jnp.abs(den), jnp.exp(-mt))
            out = out + jnp.where(head == h, hh, 0.0)
        o_ref[rows, :] = jax.nn.sigmoid(og) * out
        wk = g_tot - bcum + log_i
        m_new = jnp.maximum(g_tot + m_prev, jnp.max(wk, axis=0, keepdims=True))
        decay = jnp.exp(g_tot + m_prev - m_new)
        kw = k * jnp.exp(wk - m_new)
        kv = _dot(jnp.transpose(kw).astype(BF16), vb)
        c_scr[...] = decay * c_prev + jnp.where(same_head, kv, 0.0)
        n_scr[...] = decay * n_prev + jnp.sum(kw, axis=0, keepdims=True)
        m_scr[...] = m_new
        return carry

    lax.fori_loop(0, seq // lc, chunk, 0)


def _mlstm(z, gates, bi, bf):
    b, s, _ = z.shape
    return pl.pallas_call(
        _mlstm_kernel,
        grid=(b,),
        in_specs=[pl.BlockSpec((None, s, Z_ML), lambda i: (i, 0, 0)),
                  pl.BlockSpec((None, s, Z_G), lambda i: (i, 0, 0)),
                  _const_spec(bi.shape), _const_spec(bf.shape)],
        out_specs=pl.BlockSpec((None, s, GROUP_W), lambda i: (i, 0, 0)),
        out_shape=jax.ShapeDtypeStruct((b, s, GROUP_W), F32),
        scratch_shapes=[pltpu.VMEM((GROUP_W, GROUP_W), F32), pltpu.VMEM((1, GROUP_W), F32),
                        pltpu.VMEM((1, GROUP_W), F32)],
        compiler_params=pltpu.CompilerParams(dimension_semantics=("arbitrary",),
                                             vmem_limit_bytes=_vmem_limit(48 * 1024 * 1024)),
        name="mlstm",
    )(z, gates, bi, bf)


def _dot_exact(a, b):
    return jnp.dot(a, b, precision=lax.Precision.HIGHEST, preferred_element_type=F32)


def _mlstm_kernel_t(z_ref, g_ref, bi_ref, bf_ref, o_ref, ct_scr, n_scr, m_scr):
    seq = z_ref.shape[0]
    lc = ML_CHUNK
    sub = lax.broadcasted_iota(jnp.int32, (lc, lc), 0)
    lan = lax.broadcasted_iota(jnp.int32, (lc, lc), 1)
    tri_incl = jnp.where(lan <= sub, 1.0, 0.0)
    visible = sub <= lan
    head = _head_of_lane(GROUP_W)
    expand = jnp.where(lax.broadcasted_iota(jnp.int32, (V7X_LANES, GROUP_W), 0) == head, 1.0, 0.0)
    same_head = lax.broadcasted_iota(jnp.int32, (GROUP_W, GROUP_W), 0) // HEAD_DIM == head
    n_rows = lax.broadcasted_iota(jnp.int32, (8, GROUP_W), 0) == head
    ct_scr[...] = jnp.zeros_like(ct_scr)
    n_scr[...] = jnp.zeros_like(n_scr)
    m_scr[...] = jnp.zeros_like(m_scr)

    def chunk(ci, carry):
        rows = pl.ds(pl.multiple_of(ci * lc, lc), lc)
        q = z_ref[rows, 0:GROUP_W].astype(BF16)
        k = z_ref[rows, GROUP_W:2 * GROUP_W] * (HEAD_DIM ** -0.5)
        v_t = jnp.transpose(z_ref[rows, 2 * GROUP_W:3 * GROUP_W]).astype(BF16)
        og = z_ref[rows, 3 * GROUP_W:4 * GROUP_W]
        log_i = g_ref[rows, 0:V7X_LANES] + bi_ref[...]
        f_pre = g_ref[rows, V7X_LANES:] + bf_ref[...]
        log_f = jnp.minimum(f_pre, 0.0) - jnp.log1p(jnp.exp(-jnp.abs(f_pre)))
        b_col = _dot_exact(tri_incl, log_f)
        key_col = log_i - b_col
        b_row = jnp.transpose(b_col)
        m_prev = m_scr[...]
        ct_prev = ct_scr[...]
        n_prev = n_scr[...]

        qs = jnp.concatenate([jnp.where(head == h, q, jnp.zeros_like(q)) for h in range(HEADS)], axis=0)
        s_all = _dot_nt(k.astype(BF16), qs)
        qc_t = _dot_nt(ct_prev.astype(BF16), q)
        qn = _dot_nt(jnp.where(n_rows, n_prev, 0.0).astype(BF16), q)
        outs = []
        for h in range(HEADS):
            b_t = b_row[h:h + 1, :]
            dmat = jnp.where(visible, b_t + key_col[:, h:h + 1], NEG_INF)
            inter = b_t + m_prev[:, h:h + 1]
            mt = jnp.maximum(jnp.max(dmat, axis=0, keepdims=True), inter)
            w_t = s_all[:, h * lc:(h + 1) * lc] * jnp.exp(dmat - mt)
            w_inter = jnp.exp(inter - mt)
            lanes = slice(h * HEAD_DIM, (h + 1) * HEAD_DIM)
            num = _dot(v_t[lanes, :], w_t.astype(BF16)) + w_inter * qc_t[lanes, :]
            den = jnp.sum(w_t, axis=0, keepdims=True) + w_inter * qn[h:h + 1, :]
            outs.append(num * (1.0 / jnp.maximum(jnp.abs(den), jnp.exp(-mt))))
        o_ref[rows, :] = jax.nn.sigmoid(og) * jnp.transpose(jnp.concatenate(outs, axis=0))

        g_tot = b_col[lc - 1:lc, :]
        wk = g_tot + key_col
        m_new = jnp.maximum(g_tot + m_prev, jnp.max(wk, axis=0, keepdims=True))
        decay = jnp.exp(g_tot + m_prev - m_new)
        wide = _dot_exact(jnp.concatenate([jnp.exp(wk - m_new), jnp.broadcast_to(decay, (8, V7X_LANES))], axis=0),
                          expand)
        kw = k * wide[:lc]
        decay_w = wide[lc:lc + 1]
        ct_scr[...] = decay_w * ct_prev + jnp.where(same_head, _dot(v_t, kw.astype(BF16)), 0.0)
        n_scr[...] = decay_w * n_prev + jnp.sum(kw, axis=0, keepdims=True)
        m_scr[...] = m_new
        return carry

    lax.fori_loop(0, seq // lc, chunk, 0, unroll=2)


def _mlstm_t(z, gates, bi, bf):
    b, s, _ = z.shape
    return pl.pallas_call(
        _mlstm_kernel_t,
        grid=(b,),
        in_specs=[pl.BlockSpec((None, s, Z_ML), lambda i: (i, 0, 0)),
                  pl.BlockSpec((None, s, Z_G), lambda i: (i, 0, 0)),
                  _const_spec(bi.shape), _const_spec(bf.shape)],
        out_specs=pl.BlockSpec((None, s, GROUP_W), lambda i: (i, 0, 0)),
        out_shape=jax.ShapeDtypeStruct((b, s, GROUP_W), F32),
        scratch_shapes=[pltpu.VMEM((GROUP_W, GROUP_W), F32), pltpu.VMEM((1, GROUP_W), F32),
                        pltpu.VMEM((1, V7X_LANES), F32)],
        compiler_params=pltpu.CompilerParams(dimension_semantics=("arbitrary",),
                                             vmem_limit_bytes=_vmem_limit(48 * 1024 * 1024)),
        name="mlstm",
    )(z, gates, bi, bf)


def _chunk_cols(w):
    return w.reshape(w.shape[0], N_FF_CHUNKS, FF_CHUNK).transpose(1, 0, 2).astype(BF16)


def _chunk_rows(w):
    return w.reshape(N_FF_CHUNKS, FF_CHUNK, w.shape[1]).astype(BF16)


def _block_diag(w):
    out = jnp.zeros((GROUP_W, GROUP_W), w.dtype)
    for h in range(HEADS):
        out = out.at[h * HEAD_DIM:(h + 1) * HEAD_DIM, h * HEAD_DIM:(h + 1) * HEAD_DIM].set(w[h])
    return out


def _permute_w_in(w):
    kr = Q_LORA + KV_LORA
    gap = jnp.zeros((w.shape[0], HEAD_DIM - ROPE_HALF), w.dtype)
    body_lo = kr + MLA_ROPE
    body_hi = body_lo + 9 * GROUP_W
    pad = jnp.zeros((w.shape[0], V7X_LANES - HEADS), w.dtype)
    return jnp.concatenate([w[:, :kr], w[:, kr:kr + ROPE_HALF], gap, w[:, kr + ROPE_HALF:kr + MLA_ROPE], gap,
                            w[:, body_lo:body_hi], w[:, body_hi:body_hi + HEADS], pad,
                            w[:, body_hi + HEADS:], pad], axis=1).astype(BF16)


def _mla_up_layout():
    per_q = MLA_NOPE + MLA_ROPE
    q_src, q_dst = [], []
    for h in range(HEADS):
        base = h * GROUP_W
        nope_lo = base + (h % 2) * HEAD_DIM
        q_src += [h * per_q + i for i in range(per_q)]
        q_dst += [nope_lo + i for i in range(MLA_NOPE)]
        q_dst += [base + V7X_LANES + i for i in range(ROPE_HALF)]
        q_dst += [base + V7X_LANES + HEAD_DIM + i for i in range(ROPE_HALF)]
    per_kv = MLA_NOPE + HEAD_DIM
    kn = [h * per_kv + i for h in range(HEADS) for i in range(MLA_NOPE)]
    vv = [h * per_kv + MLA_NOPE + i for h in range(HEADS) for i in range(HEAD_DIM)]
    return np.array(q_src), np.array(q_dst), np.array(kn + vv)


def _rope_tables(seq):
    freqs = ROPE_THETA ** (-jnp.arange(ROPE_HALF, dtype=F32) / ROPE_HALF)
    ang = jnp.arange(seq, dtype=F32)[:, None] * freqs[None, :]
    cos = jnp.tile(jnp.cos(ang), (1, 2 * HEADS))
    sin = jnp.tile(jnp.sin(ang), (1, HEADS))
    return cos, jnp.concatenate([-sin, sin], axis=1)


def kernel(x, ln_g, ln_b, ffn_w1, ffn_w3, ffn_w2, w_in, mla_q_norm, mla_kv_norm, mla_w_uq, mla_w_ukv,
           lru_conv_w, lru_conv_b, lru_w_a, lru_b_a, lru_w_x, lru_b_x, lru_lambda, ml_b_i, ml_b_f,
           out_norm, w_out):
    bsz, seq, d = x.shape
    n = bsz * seq
    row = lambda t: t.reshape(1, -1)
    q_src, q_dst, kv_perm = _mla_up_layout()
    cos, sin = _rope_tables(seq)
    swa_bias = _swa_bias_tiles(seq)
    xt = x.reshape(n, d)
    for l in range(DEPTH):
        ffn = lambda j: (_chunk_cols(ffn_w1[l, j]), _chunk_cols(ffn_w3[l, j]), _chunk_rows(ffn_w2[l, j]))
        x1, za, zlru, sq, sk, sv, zml, zg = _ffn_inproj(
            xt, *ffn(0), row(ln_g[l, 0]), row(ln_b[l, 0]), _permute_w_in(w_in[l]))
        seqv = lambda t: t.reshape(bsz, seq, t.shape[-1])
        wuq = jnp.zeros((Q_LORA, HEADS * GROUP_W), F32).at[:, q_dst].set(mla_w_uq[l][:, q_src]).astype(BF16)
        y_a = _mla(seqv(za), row(mla_q_norm[l]), row(mla_kv_norm[l]), wuq,
                   mla_w_ukv[l][:, kv_perm].astype(BF16), cos, sin)
        y_b = _rglru(seqv(zlru), lru_conv_w[l], row(lru_conv_b[l]),
                     _block_diag(lru_w_a[l]).astype(BF16), row(lru_b_a[l]),
                     _block_diag(lru_w_x[l]).astype(BF16), row(lru_b_x[l]), row(lru_lambda[l]))
        y_c = _swa(seqv(sq), seqv(sk), seqv(sv), swa_bias)
        lane_pad = lambda t: row(jnp.pad(t, (0, V7X_LANES - HEADS)))
        y_d = _mlstm_t(seqv(zml), seqv(zg), lane_pad(ml_b_i[l]), lane_pad(ml_b_f[l]))
        ys = [t.reshape(n, GROUP_W) for t in (y_a, y_b, y_c, y_d)]
        xt = _outproj_ffn(x1, ys, row(out_norm[l]), w_out[l].astype(BF16), row(ln_g[l, 1]), row(ln_b[l, 1]),
                          *ffn(1), row(ln_g[l, 2]), row(ln_b[l, 2]))
    return xt.reshape(bsz, seq, d)
```

```python
import numpy as np
import jax
import jax.numpy as jnp
from jax import lax
from jax.experimental import pallas as pl
from jax.experimental.pallas import tpu as pltpu

F32 = jnp.float32
BF16 = jnp.bfloat16

D_MODEL = 1024
DEPTH = 2
GROUP_W = 256
HEAD_DIM = 64
HEADS = 4
D_FF = 2816
ALPHA = (2.0 * DEPTH) ** 0.25
Q_LORA = 256
KV_LORA = 128
MLA_NOPE = 64
MLA_ROPE = 32
ROPE_HALF = MLA_ROPE // 2
ROPE_THETA = 10000.0
CONV_W = 4
LRU_C = 8.0
DILATED_PATTERNS = ((128, 1), (512, 4), (2048, 16))

V7X_LANES = 128
V7X_VMEM_BYTES = 64 * 1024 * 1024

FF_CHUNK = 256
N_FF_CHUNKS = D_FF // FF_CHUNK
TOKEN_TILE = 512
ML_CHUNK = 128
ATT_T = 256
ATT_COLS = 1024
MLA_TQ = 512
PAIR_W = 2 * HEAD_DIM
N_BIAS_TILES = 4

Z_A = 512
Z_LRU = 512
Z_ML = 1024
Z_G = 256

NEG_INF = float("-inf")
LOG2E = float(np.log2(np.e))


def _vmem_limit(nbytes):
    return int(min(nbytes, V7X_VMEM_BYTES - 6 * 1024 * 1024))


def _const_spec(shape):
    nd = len(shape)
    return pl.BlockSpec(shape, lambda *_: (0,) * nd, pipeline_mode=pl.Buffered(1))


def _stacked_spec(shape, lead):
    tail = shape[len(lead):]
    return pl.BlockSpec((None,) * len(lead) + tuple(tail), lambda *_: tuple(lead) + (0,) * len(tail),
                        pipeline_mode=pl.Buffered(1))


def _layernorm(v, g, b):
    mu = jnp.mean(v, axis=-1, keepdims=True)
    d = v - mu
    var = jnp.mean(d * d, axis=-1, keepdims=True)
    return d * lax.rsqrt(var + 1e-5) * g + b


def _rmsnorm(v, g, eps=1e-6):
    return v * lax.rsqrt(jnp.mean(v * v, axis=-1, keepdims=True) + eps) * g


def _dot(a, b):
    return jnp.dot(a, b, preferred_element_type=F32)


def _dot_nt(a, b):
    return lax.dot_general(a, b, (((1,), (1,)), ((), ())), preferred_element_type=F32)


def _head_of_lane(width, group=HEAD_DIM):
    return lax.broadcasted_iota(jnp.int32, (1, width), 1) // group


def _ffn_ln(x, w1_ref, w3_ref, w2_ref, acc_ref, g, b):
    xb = x.astype(BF16)
    for c in range(N_FF_CHUNKS):
        cols = slice(c * FF_CHUNK, (c + 1) * FF_CHUNK)
        h1 = _dot(xb, w1_ref[:, cols])
        h3 = _dot(xb, w3_ref[:, cols])
        h = (h1 * jax.nn.sigmoid(h1)) * h3
        part = _dot(h.astype(BF16), w2_ref[cols, :])
        if c == 0:
            acc_ref[...] = part
        else:
            acc_ref[...] += part
    return _layernorm(ALPHA * x + 0.5 * acc_ref[...], g, b)


def _ffn_inproj_kernel(x_ref, w1_ref, w3_ref, w2_ref, g_ref, b_ref, win_ref,
                       x1_ref, za_ref, zlru_ref, sq_ref, sk_ref, sv_ref, zml_ref, zg_ref, acc_ref):
    x1 = _ffn_ln(x_ref[...], w1_ref, w3_ref, w2_ref, acc_ref, g_ref[...], b_ref[...])
    x1_ref[...] = x1
    xb = x1.astype(BF16)
    off = 0
    for ref in (za_ref, zlru_ref, sq_ref, sk_ref, sv_ref, zml_ref, zg_ref):
        w = ref.shape[-1]
        ref[...] = _dot(xb, win_ref[:, off:off + w])
        off += w


def _ffn_inproj(x, w1, w3, w2, lj, g, b, win):
    n = x.shape[0]
    tm = TOKEN_TILE
    tile = lambda w: pl.BlockSpec((tm, w), lambda i: (i, 0))
    widths = (D_MODEL, Z_A, Z_LRU, GROUP_W, GROUP_W, GROUP_W, Z_ML, Z_G)
    return pl.pallas_call(
        _ffn_inproj_kernel,
        grid=(n // tm,),
        in_specs=[tile(D_MODEL), _stacked_spec(w1.shape, lj), _stacked_spec(w3.shape, lj), _stacked_spec(w2.shape, lj),
                  _const_spec(g.shape), _const_spec(b.shape), _stacked_spec(win.shape, lj[:1])],
        out_specs=[tile(w) for w in widths],
        out_shape=[jax.ShapeDtypeStruct((n, w), F32) for w in widths],
        scratch_shapes=[pltpu.VMEM((tm, D_MODEL), F32)],
        compiler_params=pltpu.CompilerParams(dimension_semantics=("arbitrary",),
                                             vmem_limit_bytes=_vmem_limit(56 * 1024 * 1024)),
        name="ffn_inproj",
    )(x, w1, w3, w2, g, b, win)


def _outproj_ffn_kernel(x_ref, ya_ref, yb_ref, yc_ref, yd_ref,
                        on_ref, wout_ref, g1_ref, b1_ref, w1_ref, w3_ref, w2_ref, g2_ref, b2_ref,
                        out_ref, acc_ref):
    y = jnp.zeros(x_ref.shape, F32)
    for gi, y_ref in enumerate((ya_ref, yb_ref, yc_ref, yd_ref)):
        lo = gi * GROUP_W
        yn = _rmsnorm(y_ref[...], on_ref[:, lo:lo + GROUP_W])
        y = y + _dot(yn.astype(BF16), wout_ref[lo:lo + GROUP_W, :])
    x2 = _layernorm(ALPHA * x_ref[...] + y, g1_ref[...], b1_ref[...])
    out_ref[...] = _ffn_ln(x2, w1_ref, w3_ref, w2_ref, acc_ref, g2_ref[...], b2_ref[...])


def _outproj_ffn(x, ys, on, wout, g1, b1, w1, w3, w2, lj, g2, b2):
    n = x.shape[0]
    tm = TOKEN_TILE
    tile = lambda w: pl.BlockSpec((tm, w), lambda i: (i, 0))
    consts = (on, wout, g1, b1, w1, w3, w2, g2, b2)
    const_specs = [_stacked_spec(c.shape, lj) if c.ndim == 4 else _const_spec(c.shape) for c in consts]
    return pl.pallas_call(
        _outproj_ffn_kernel,
        grid=(n // tm,),
        in_specs=[tile(D_MODEL)] + [tile(GROUP_W)] * len(ys) + const_specs,
        out_specs=tile(D_MODEL),
        out_shape=jax.ShapeDtypeStruct((n, D_MODEL), F32),
        scratch_shapes=[pltpu.VMEM((tm, D_MODEL), F32)],
        compiler_params=pltpu.CompilerParams(dimension_semantics=("arbitrary",),
                                             vmem_limit_bytes=_vmem_limit(56 * 1024 * 1024)),
        name="outproj_ffn",
    )(x, *ys, *consts)


VT_ROWS = PAIR_W + 16


def _flash_scratch():
    return [pltpu.VMEM((1, ATT_COLS), F32),
            pltpu.VMEM((1, ATT_COLS), F32),
            pltpu.VMEM((VT_ROWS, ATT_COLS), F32),
            pltpu.VMEM((ATT_T, ATT_COLS), BF16),
            pltpu.VMEM((2, ATT_T, ATT_COLS), F32),
            pltpu.VMEM((2, 1, ATT_COLS), F32)]


def _flash_attend(n_steps, qm, key_block, value_block, bias_block, scratch, groups):
    m_ref, a_ref, acc_ref, p_ref, s_ref, mx_ref = scratch
    m_ref[...] = jnp.full(m_ref.shape, NEG_INF, F32)
    a_ref[...] = jnp.ones_like(a_ref)
    acc_ref[...] = jnp.zeros_like(acc_ref)
    p_ref[...] = jnp.zeros_like(p_ref)

    def scores(j, slot):
        bias = bias_block(j)
        s_t = _dot_nt(key_block(j), qm) + jnp.concatenate([bias] * (ATT_COLS // bias.shape[1]), axis=1)
        s_ref[slot] = s_t
        mx_ref[slot] = jnp.max(s_t, axis=0, keepdims=True)

    def accumulate(j):
        a = a_ref[...]
        for g, (c_lo, c_hi) in enumerate(groups):
            acc_ref[:, c_lo:c_hi] = (a[:, c_lo:c_hi] * acc_ref[:, c_lo:c_hi]
                                     + _dot(value_block(j, g), p_ref[:, c_lo:c_hi]))

    scores(0, 0)

    def step(j, carry):
        cur = lax.rem(j, 2)
        accumulate(jnp.maximum(j - 1, 0))
        m = m_ref[...]
        m_new = jnp.maximum(m, mx_ref[cur])
        m_ref[...] = m_new
        a_ref[...] = jnp.exp2(m - m_new)
        p_ref[...] = jnp.exp2((s_ref[cur] - m_new).astype(BF16))
        scores(jnp.minimum(j + 1, n_steps - 1), 1 - cur)
        return carry

    lax.fori_loop(0, n_steps, step, 0)
    accumulate(n_steps - 1)
    return jnp.transpose(acc_ref[0:PAIR_W, :] * (1.0 / acc_ref[PAIR_W:PAIR_W + 1, :]))


def _merge_pair(o_a, o_b):
    lane = lax.broadcasted_iota(jnp.int32, (1, PAIR_W), 1)
    return jnp.where(lane < HEAD_DIM, o_a, o_b)


def _store_transposed_values(v, vt_ref, j):
    v_t = jnp.transpose(v).astype(BF16)
    for p in range(2):
        vt_ref[j, p, 0:PAIR_W, :] = v_t[p * PAIR_W:(p + 1) * PAIR_W]
        vt_ref[j, p, PAIR_W:, :] = jnp.ones((VT_ROWS - PAIR_W, ATT_T), BF16)


def _mla_kernel(za_ref, qn_ref, kvn_ref, wuq_ref, wukv_ref, cos_ref, sin_ref, o_ref,
                qm_scr, kp_scr, vt_scr, o_scr, bias_scr, *flash_scr):
    seq = za_ref.shape[0]
    c_q = (MLA_NOPE + MLA_ROPE) ** -0.5 * LOG2E

    def prep(j, carry):
        rows = pl.ds(pl.multiple_of(j * ATT_T, ATT_T), ATT_T)
        cos, sin = cos_ref[rows, :], sin_ref[rows, :]

        def rope(r):
            return r * cos + pltpu.roll(r, V7X_LANES // 2, axis=1) * sin

        cq = _rmsnorm(za_ref[rows, 0:Q_LORA], qn_ref[...])
        q = _dot(cq.astype(BF16), wuq_ref[...]) * c_q
        for h in range(HEADS):
            lo = h * GROUP_W
            qm_scr[h, rows, 0:V7X_LANES] = q[:, lo:lo + V7X_LANES].astype(BF16)
            qm_scr[h, rows, V7X_LANES:] = rope(q[:, lo + V7X_LANES:lo + GROUP_W]).astype(BF16)
        ckv = _rmsnorm(za_ref[rows, Q_LORA:Q_LORA + KV_LORA], kvn_ref[...])
        kv = _dot(ckv.astype(BF16), wukv_ref[...])
        k_rope = rope(za_ref[rows, Q_LORA + KV_LORA:]).astype(BF16)
        for p in range(2):
            kp_scr[p, rows, 0:V7X_LANES] = kv[:, p * PAIR_W:(p + 1) * PAIR_W].astype(BF16)
            kp_scr[p, rows, V7X_LANES:] = k_rope
        _store_transposed_values(kv[:, GROUP_W:], vt_scr, j)
        return carry

    lax.fori_loop(0, seq // ATT_T, prep, 0)

    key = lax.broadcasted_iota(jnp.int32, (ATT_T, MLA_TQ), 0)
    qry = lax.broadcasted_iota(jnp.int32, (ATT_T, MLA_TQ), 1)
    n_diag = MLA_TQ // ATT_T
    bias_scr[0] = jnp.zeros((ATT_T, MLA_TQ), F32)
    for d in range(n_diag):
        bias_scr[1 + d] = jnp.where(key + d * ATT_T <= qry, 0.0, NEG_INF)
    groups = [(0, ATT_COLS)]

    def q_block(i, carry):
        q0 = pl.multiple_of(i * MLA_TQ, MLA_TQ)
        q_rows = pl.ds(q0, MLA_TQ)
        first_diag = i * n_diag

        def pair(p, c):
            qm = jnp.concatenate([qm_scr[2 * p, q_rows, :], qm_scr[2 * p + 1, q_rows, :]], axis=0)
            o_t = _flash_attend(
                first_diag + n_diag, qm,
                lambda j: kp_scr[p, pl.ds(pl.multiple_of(j * ATT_T, ATT_T), ATT_T), :],
                lambda j, g: vt_scr[j, p],
                lambda j: bias_scr[jnp.maximum(j - first_diag + 1, 0)],
                flash_scr, groups)
            o_scr[p] = _merge_pair(o_t[:MLA_TQ], o_t[MLA_TQ:])
            return c

        lax.fori_loop(0, 2, pair, 0)
        for p in range(2):
            o_ref[q_rows, p * PAIR_W:(p + 1) * PAIR_W] = o_scr[p]
        return carry

    lax.fori_loop(0, seq // MLA_TQ, q_block, 0)


def _mla(za, qn, kvn, wuq, wukv, cos, sin):
    b, s, _ = za.shape
    consts = (qn, kvn, wuq, wukv, cos, sin)
    return pl.pallas_call(
        _mla_kernel,
        grid=(b,),
        in_specs=[pl.BlockSpec((None, s, Z_A), lambda i: (i, 0, 0))] + [_const_spec(c.shape) for c in consts],
        out_specs=pl.BlockSpec((None, s, GROUP_W), lambda i: (i, 0, 0)),
        out_shape=jax.ShapeDtypeStruct((b, s, GROUP_W), F32),
        scratch_shapes=([pltpu.VMEM((HEADS, s, GROUP_W), BF16), pltpu.VMEM((2, s, GROUP_W), BF16),
                         pltpu.VMEM((s // ATT_T, 2, VT_ROWS, ATT_T), BF16), pltpu.VMEM((2, MLA_TQ, PAIR_W), F32),
                         pltpu.VMEM((1 + MLA_TQ // ATT_T, ATT_T, MLA_TQ), F32)] + _flash_scratch()),
        compiler_params=pltpu.CompilerParams(dimension_semantics=("arbitrary",),
                                             vmem_limit_bytes=_vmem_limit(48 * 1024 * 1024)),
        name="mla",
    )(za, *consts)


def _swa_kernel(q_ref, k_ref, v_ref, bias_ref, o_ref, qm_scr, k_scr, vt_scr, *flash_scr):
    seq = q_ref.shape[0]
    head = _head_of_lane(GROUP_W)
    c_q = HEAD_DIM ** -0.5 * LOG2E
    half = ATT_COLS // 2
    groups = [(0, half), (half, ATT_COLS)]

    def prep(j, carry):
        rows = pl.ds(pl.multiple_of(j * ATT_T, ATT_T), ATT_T)
        q = (q_ref[rows, :] * c_q).astype(BF16)
        for h in range(HEADS):
            qm_scr[h, rows, :] = jnp.where(head == h, q, jnp.zeros_like(q))
        k_scr[rows, :] = k_ref[rows, :].astype(BF16)
        _store_transposed_values(v_ref[rows, :], vt_scr, j)
        return carry

    lax.fori_loop(0, seq // ATT_T, prep, 0)

    def q_block(i, carry):
        q_rows = pl.ds(pl.multiple_of(i * ATT_T, ATT_T), ATT_T)
        qm = jnp.concatenate([qm_scr[h, q_rows, :] for h in range(HEADS)], axis=0)
        o_t = _flash_attend(
            i + 1, qm,
            lambda j: k_scr[pl.ds(pl.multiple_of(j * ATT_T, ATT_T), ATT_T), :],
            lambda j, g: vt_scr[j, g],
            lambda j: bias_ref[jnp.minimum(i - j, N_BIAS_TILES - 1)],
            flash_scr, groups)
        for p in range(2):
            lo = 2 * p * ATT_T
            o_ref[q_rows, p * PAIR_W:(p + 1) * PAIR_W] = _merge_pair(o_t[lo:lo + ATT_T], o_t[lo + ATT_T:lo + 2 * ATT_T])
        return carry

    lax.fori_loop(0, seq // ATT_T, q_block, 0)


def _swa_bias_tiles(seq):
    assert DILATED_PATTERNS[-1][0] >= seq - 1, "far tiles assume the widest window covers the sequence"
    assert all(w <= (N_BIAS_TILES - 2) * ATT_T for w, _ in DILATED_PATTERNS[:-1]), "narrow windows must end before the far tile"
    key = np.arange(ATT_T)[:, None]
    qry = np.arange(ATT_T)[None, :]
    tiles = []
    for dist in range(N_BIAS_TILES):
        d = dist * ATT_T + qry - key
        mult = sum(((d >= 0) & (d % dil == 0) & (d <= window)).astype(np.float32) for window, dil in DILATED_PATTERNS)
        with np.errstate(divide="ignore"):
            tiles.append(np.log2(mult))
    return jnp.asarray(np.stack(tiles), F32)


def _swa(q, k, v, bias):
    b, s, w = q.shape
    spec = pl.BlockSpec((None, s, w), lambda i: (i, 0, 0))
    return pl.pallas_call(
        _swa_kernel,
        grid=(b,),
        in_specs=[spec, spec, spec, _const_spec(bias.shape)],
        out_specs=spec,
        out_shape=jax.ShapeDtypeStruct((b, s, w), F32),
        scratch_shapes=([pltpu.VMEM((HEADS, s, w), BF16), pltpu.VMEM((s, w), BF16),
                         pltpu.VMEM((s // ATT_T, 2, VT_ROWS, ATT_T), BF16)] + _flash_scratch()),
        compiler_params=pltpu.CompilerParams(dimension_semantics=("arbitrary",),
                                             vmem_limit_bytes=_vmem_limit(48 * 1024 * 1024)),
        name="swa",
    )(q, k, v, bias)


def _shift_rows(x, d, fill):
    row = lax.broadcasted_iota(jnp.int32, x.shape, 0)
    return jnp.where(row >= d, pltpu.roll(x, d, axis=0), fill)


def _softplus(x):
    return jnp.maximum(x, 0.0) + jnp.log1p(jnp.exp(-jnp.abs(x)))


def _gelu_tanh(x):
    c = np.sqrt(2.0 / np.pi).astype(np.float32)
    return 0.5 * x * (1.0 + jnp.tanh(c * (x + 0.044715 * (x * x * x))))


def _rglru_kernel(z_ref, cw_ref, cb_ref, wa_ref, ba_ref, wx_ref, bx_ref, lam_ref, o_ref):
    seq = z_ref.shape[0]
    x = z_ref[:, 0:GROUP_W]
    xc = cb_ref[...] + cw_ref[CONV_W - 1:CONV_W, :] * x
    for j in range(CONV_W - 1):
        xc = xc + cw_ref[j:j + 1, :] * _shift_rows(x, CONV_W - 1 - j, 0.0)
    xcb = xc.astype(BF16)
    r = jax.nn.sigmoid(_dot(xcb, wa_ref[...]) + ba_ref[...])
    i = jax.nn.sigmoid(_dot(xcb, wx_ref[...]) + bx_ref[...])
    log_a = -LRU_C * r * _softplus(-lam_ref[...])
    a = jnp.exp(log_a)
    u = jnp.sqrt(-jnp.tanh(log_a) * (a * a + 1.0)) * (i * xc)
    d = 1
    while d < seq:
        u = a * _shift_rows(u, d, 0.0) + u
        a = a * _shift_rows(a, d, 1.0)
        d *= 2
    o_ref[...] = u * _gelu_tanh(z_ref[:, GROUP_W:])


def _rglru(z, cw, cb, wa, ba, wx, bx, lam):
    b, s, _ = z.shape
    consts = (cw, cb, wa, ba, wx, bx, lam)
    return pl.pallas_call(
        _rglru_kernel,
        grid=(b,),
        in_specs=[pl.BlockSpec((None, s, Z_LRU), lambda i: (i, 0, 0))] + [_const_spec(c.shape) for c in consts],
        out_specs=pl.BlockSpec((None, s, GROUP_W), lambda i: (i, 0, 0)),
        out_shape=jax.ShapeDtypeStruct((b, s, GROUP_W), F32),
        compiler_params=pltpu.CompilerParams(dimension_semantics=("arbitrary",),
                                             vmem_limit_bytes=_vmem_limit(48 * 1024 * 1024)),
        name="rglru",
    )(z, *consts)


def _expand_heads(cols, head):
    out = jnp.zeros((cols.shape[0], GROUP_W), F32)
    for h in range(HEADS):
        out = out + jnp.where(head == h, cols[:, h:h + 1], 0.0)
    return out


def _mlstm_kernel(z_ref, g_ref, bi_ref, bf_ref, o_ref, c_scr, n_scr, m_scr):
    seq = z_ref.shape[0]
    lc = ML_CHUNK
    head = _head_of_lane(GROUP_W)
    row_head = lax.broadcasted_iota(jnp.int32, (GROUP_W, GROUP_W), 0) // HEAD_DIM
    same_head = row_head == head
    tri = lax.broadcasted_iota(jnp.int32, (lc, lc), 0) >= lax.broadcasted_iota(jnp.int32, (lc, lc), 1)
    c_scr[...] = jnp.zeros_like(c_scr)
    n_scr[...] = jnp.zeros_like(n_scr)
    m_scr[...] = jnp.zeros_like(m_scr)

    def chunk(ci, carry):
        t0 = pl.multiple_of(ci * lc, lc)
        rows = pl.ds(t0, lc)
        q = z_ref[rows, 0:GROUP_W]
        k = z_ref[rows, GROUP_W:2 * GROUP_W] * (HEAD_DIM ** -0.5)
        v = z_ref[rows, 2 * GROUP_W:3 * GROUP_W]
        og = z_ref[rows, 3 * GROUP_W:4 * GROUP_W]
        gates = g_ref[rows, :]
        log_i = _expand_heads(gates[:, 0:HEADS], head) + bi_ref[...]
        f_pre = _expand_heads(gates[:, HEADS:2 * HEADS], head) + bf_ref[...]
        log_f = jnp.minimum(f_pre, 0.0) - jnp.log1p(jnp.exp(-jnp.abs(f_pre)))
        bcum = log_f
        d = 1
        while d < lc:
            bcum = bcum + _shift_rows(bcum, d, 0.0)
            d *= 2
        g_tot = bcum[lc - 1:lc, :]
        m_prev = m_scr[...]
        c_prev = c_scr[...]
        n_prev = n_scr[...]
        qb, kb, vb = q.astype(BF16), k.astype(BF16), v.astype(BF16)
        qs = jnp.concatenate([jnp.where(head == h, qb, jnp.zeros_like(qb)) for h in range(HEADS)], axis=0)
        s_all = _dot_nt(qs, kb)
        q_c = _dot(qb, c_prev.astype(BF16))
        qn = q * n_prev
        key_row = jnp.transpose(log_i - bcum)
        inter = bcum + m_prev
        out = jnp.zeros((lc, GROUP_W), F32)
        for h in range(HEADS):
            lane0 = h * HEAD_DIM
            dmat = jnp.where(tri, bcum[:, lane0:lane0 + 1] + key_row[lane0:lane0 + 1, :], NEG_INF)
            inter_h = inter[:, lane0:lane0 + 1]
            mt = jnp.maximum(jnp.max(dmat, axis=-1, keepdims=True), inter_h)
            sw = s_all[h * lc:(h + 1) * lc] * jnp.exp(dmat - mt)
            w_inter = jnp.exp(inter_h - mt)
            num = _dot(sw.astype(BF16), vb) + w_inter * q_c
            qn_h = jnp.sum(jnp.where(head == h, qn, 0.0), axis=-1, keepdims=True)
            den = jnp.sum(sw, axis=-1, keepdims=True) + w_inter * qn_h
            hh = num / jnp.maximum(jnp.abs(den), jnp.exp(-mt))
            out = out + jnp.where(head == h, hh, 0.0)
        o_ref[rows, :] = jax.nn.sigmoid(og) * out
        wk = g_tot - bcum + log_i
        m_new = jnp.maximum(g_tot + m_prev, jnp.max(wk, axis=0, keepdims=True))
        decay = jnp.exp(g_tot + m_prev - m_new)
        kw = k * jnp.exp(wk - m_new)
        kv = _dot(jnp.transpose(kw).astype(BF16), vb)
        c_scr[...] = decay * c_prev + jnp.where(same_head, kv, 0.0)
        n_scr[...] = decay * n_prev + jnp.sum(kw, axis=0, keepdims=True)
        m_scr[...] = m_new
        return carry

    lax.fori_loop(0, seq // lc, chunk, 0)


def _mlstm(z, gates, bi, bf):
    b, s, _ = z.shape
    return pl.pallas_call(
        _mlstm_kernel,
        grid=(b,),
        in_specs=[pl.BlockSpec((None, s, Z_ML), lambda i: (i, 0, 0)),
                  pl.BlockSpec((None, s, Z_G), lambda i: (i, 0, 0)),
                  _const_spec(bi.shape), _const_spec(bf.shape)],
        out_specs=pl.BlockSpec((None, s, GROUP_W), lambda i: (i, 0, 0)),
        out_shape=jax.ShapeDtypeStruct((b, s, GROUP_W), F32),
        scratch_shapes=[pltpu.VMEM((GROUP_W, GROUP_W), F32), pltpu.VMEM((1, GROUP_W), F32),
                        pltpu.VMEM((1, GROUP_W), F32)],
        compiler_params=pltpu.CompilerParams(dimension_semantics=("arbitrary",),
                                             vmem_limit_bytes=_vmem_limit(48 * 1024 * 1024)),
        name="mlstm",
    )(z, gates, bi, bf)


def _dot_exact(a, b):
    return jnp.dot(a, b, precision=lax.Precision.HIGHEST, preferred_element_type=F32)


def _mlstm_kernel_t(z_ref, g_ref, bi_ref, bf_ref, o_ref, ct_scr, n_scr, m_scr):
    seq = z_ref.shape[0]
    lc = ML_CHUNK
    sub = lax.broadcasted_iota(jnp.int32, (lc, lc), 0)
    lan = lax.broadcasted_iota(jnp.int32, (lc, lc), 1)
    tri_incl = jnp.where(lan <= sub, 1.0, 0.0)
    visible = sub <= lan
    head = _head_of_lane(GROUP_W)
    expand = jnp.where(lax.broadcasted_iota(jnp.int32, (V7X_LANES, GROUP_W), 0) == head, 1.0, 0.0)
    same_head = lax.broadcasted_iota(jnp.int32, (GROUP_W, GROUP_W), 0) // HEAD_DIM == head
    n_rows = lax.broadcasted_iota(jnp.int32, (8, GROUP_W), 0) == head
    ct_scr[...] = jnp.zeros_like(ct_scr)
    n_scr[...] = jnp.zeros_like(n_scr)
    m_scr[...] = jnp.zeros_like(m_scr)

    def chunk(ci, carry):
        rows = pl.ds(pl.multiple_of(ci * lc, lc), lc)
        q = z_ref[rows, 0:GROUP_W].astype(BF16)
        k = z_ref[rows, GROUP_W:2 * GROUP_W] * (HEAD_DIM ** -0.5)
        v_t = jnp.transpose(z_ref[rows, 2 * GROUP_W:3 * GROUP_W]).astype(BF16)
        og = z_ref[rows, 3 * GROUP_W:4 * GROUP_W]
        log_i = g_ref[rows, 0:V7X_LANES] + bi_ref[...]
        f_pre = g_ref[rows, V7X_LANES:] + bf_ref[...]
        log_f = jnp.minimum(f_pre, 0.0) - jnp.log1p(jnp.exp(-jnp.abs(f_pre)))
        b_col = _dot_exact(tri_incl, log_f)
        key_col = log_i - b_col
        b_row = jnp.transpose(b_col)
        m_prev = m_scr[...]
        ct_prev = ct_scr[...]
        n_prev = n_scr[...]

        qs = jnp.concatenate([jnp.where(head == h, q, jnp.zeros_like(q)) for h in range(HEADS)], axis=0)
        s_all = _dot_nt(k.astype(BF16), qs)
        qc_t = _dot_nt(ct_prev.astype(BF16), q)
        qn = _dot_nt(jnp.where(n_rows, n_prev, 0.0).astype(BF16), q)
        outs = []
        for h in range(HEADS):
            b_t = b_row[h:h + 1, :]
            dmat = jnp.where(visible, b_t + key_col[:, h:h + 1], NEG_INF)
            inter = b_t + m_prev[:, h:h + 1]
            mt = jnp.maximum(jnp.max(dmat, axis=0, keepdims=True), inter)
            w_t = s_all[:, h * lc:(h + 1) * lc] * jnp.exp(dmat - mt)
            w_inter = jnp.exp(inter - mt)
            lanes = slice(h * HEAD_DIM, (h + 1) * HEAD_DIM)
            num = _dot(v_t[lanes, :], w_t.astype(BF16)) + w_inter * qc_t[lanes, :]
            den = jnp.sum(w_t, axis=0, keepdims=True) + w_inter * qn[h:h + 1, :]
            outs.append(num * (1.0 / jnp.maximum(jnp.abs(den), jnp.exp(-mt))))
        o_ref[rows, :] = jax.nn.sigmoid(og) * jnp.transpose(jnp.concatenate(outs, axis=0))

        g_tot = b_col[lc - 1:lc, :]
        wk = g_tot + key_col
        m_new = jnp.maximum(g_tot + m_prev, jnp.max(wk, axis=0, keepdims=True))
        decay = jnp.exp(g_tot + m_prev - m_new)
        wide = _dot_exact(jnp.concatenate([jnp.exp(wk - m_new), jnp.broadcast_to(decay, (8, V7X_LANES))], axis=0),
                          expand)
        kw = k * wide[:lc]
        decay_w = wide[lc:lc + 1]
        ct_scr[...] = decay_w * ct_prev + jnp.where(same_head, _dot(v_t, kw.astype(BF16)), 0.0)
        n_scr[...] = decay_w * n_prev + jnp.sum(kw, axis=0, keepdims=True)
        m_scr[...] = m_new
        return carry

    lax.fori_loop(0, seq // lc, chunk, 0, unroll=2)


def _mlstm_t(z, gates, bi, bf):
    b, s, _ = z.shape
    return pl.pallas_call(
        _mlstm_kernel_t,
        grid=(b,),
        in_specs=[pl.BlockSpec((None, s, Z_ML), lambda i: (i, 0, 0)),
                  pl.BlockSpec((None, s, Z_G), lambda i: (i, 0, 0)),
                  _const_spec(bi.shape), _const_spec(bf.shape)],
        out_specs=pl.BlockSpec((None, s, GROUP_W), lambda i: (i, 0, 0)),
        out_shape=jax.ShapeDtypeStruct((b, s, GROUP_W), F32),
        scratch_shapes=[pltpu.VMEM((GROUP_W, GROUP_W), F32), pltpu.VMEM((1, GROUP_W), F32),
                        pltpu.VMEM((1, V7X_LANES), F32)],
        compiler_params=pltpu.CompilerParams(dimension_semantics=("arbitrary",),
                                             vmem_limit_bytes=_vmem_limit(48 * 1024 * 1024)),
        name="mlstm",
    )(z, gates, bi, bf)


def _block_diag(w):
    out = jnp.zeros((GROUP_W, GROUP_W), w.dtype)
    for h in range(HEADS):
        out = out.at[h * HEAD_DIM:(h + 1) * HEAD_DIM, h * HEAD_DIM:(h + 1) * HEAD_DIM].set(w[h])
    return out


def _permute_w_in(w):
    kr = Q_LORA + KV_LORA
    zeros = lambda width: jnp.zeros(w.shape[:2] + (width,), w.dtype)
    gap = zeros(HEAD_DIM - ROPE_HALF)
    body_lo = kr + MLA_ROPE
    body_hi = body_lo + 9 * GROUP_W
    pad = zeros(V7X_LANES - HEADS)
    return jnp.concatenate([w[..., :kr], w[..., kr:kr + ROPE_HALF], gap, w[..., kr + ROPE_HALF:kr + MLA_ROPE], gap,
                            w[..., body_lo:body_hi], w[..., body_hi:body_hi + HEADS], pad,
                            w[..., body_hi + HEADS:], pad], axis=-1).astype(BF16)


def _mla_up_layout():
    per_q = MLA_NOPE + MLA_ROPE
    q_src, q_dst = [], []
    for h in range(HEADS):
        base = h * GROUP_W
        nope_lo = base + (h % 2) * HEAD_DIM
        q_src += [h * per_q + i for i in range(per_q)]
        q_dst += [nope_lo + i for i in range(MLA_NOPE)]
        q_dst += [base + V7X_LANES + i for i in range(ROPE_HALF)]
        q_dst += [base + V7X_LANES + HEAD_DIM + i for i in range(ROPE_HALF)]
    per_kv = MLA_NOPE + HEAD_DIM
    kn = [h * per_kv + i for h in range(HEADS) for i in range(MLA_NOPE)]
    vv = [h * per_kv + MLA_NOPE + i for h in range(HEADS) for i in range(HEAD_DIM)]
    return np.array(q_src), np.array(q_dst), np.array(kn + vv)


def _rope_tables(seq):
    freqs = ROPE_THETA ** (-jnp.arange(ROPE_HALF, dtype=F32) / ROPE_HALF)
    ang = jnp.arange(seq, dtype=F32)[:, None] * freqs[None, :]
    cos = jnp.tile(jnp.cos(ang), (1, 2 * HEADS))
    sin = jnp.tile(jnp.sin(ang), (1, HEADS))
    return cos, jnp.concatenate([-sin, sin], axis=1)


def kernel(x, ln_g, ln_b, ffn_w1, ffn_w3, ffn_w2, w_in, mla_q_norm, mla_kv_norm, mla_w_uq, mla_w_ukv,
           lru_conv_w, lru_conv_b, lru_w_a, lru_b_a, lru_w_x, lru_b_x, lru_lambda, ml_b_i, ml_b_f,
           out_norm, w_out):
    bsz, seq, d = x.shape
    n = bsz * seq
    row = lambda t: t.reshape(1, -1)
    q_src, q_dst, kv_perm = _mla_up_layout()
    cos, sin = _rope_tables(seq)
    swa_bias = _swa_bias_tiles(seq)
    xt = x.reshape(n, d)
    ffn_w = (ffn_w1.astype(BF16), ffn_w3.astype(BF16), ffn_w2.astype(BF16))
    w_in_p = _permute_w_in(w_in)
    for l in range(DEPTH):
        x1, za, zlru, sq, sk, sv, zml, zg = _ffn_inproj(
            xt, *ffn_w, (l, 0), row(ln_g[l, 0]), row(ln_b[l, 0]), w_in_p)
        seqv = lambda t: t.reshape(bsz, seq, t.shape[-1])
        wuq = jnp.zeros((Q_LORA, HEADS * GROUP_W), F32).at[:, q_dst].set(mla_w_uq[l][:, q_src]).astype(BF16)
        y_a = _mla(seqv(za), row(mla_q_norm[l]), row(mla_kv_norm[l]), wuq,
                   mla_w_ukv[l][:, kv_perm].astype(BF16), cos, sin)
        y_b = _rglru(seqv(zlru), lru_conv_w[l], row(lru_conv_b[l]),
                     _block_diag(lru_w_a[l]).astype(BF16), row(lru_b_a[l]),
                     _block_diag(lru_w_x[l]).astype(BF16), row(lru_b_x[l]), row(lru_lambda[l]))
        y_c = _swa(seqv(sq), seqv(sk), seqv(sv), swa_bias)
        lane_pad = lambda t: row(jnp.pad(t, (0, V7X_LANES - HEADS)))
        y_d = _mlstm_t(seqv(zml), seqv(zg), lane_pad(ml_b_i[l]), lane_pad(ml_b_f[l]))
        ys = [t.reshape(n, GROUP_W) for t in (y_a, y_b, y_c, y_d)]
        xt = _outproj_ffn(x1, ys, row(out_norm[l]), w_out[l].astype(BF16), row(ln_g[l, 1]), row(ln_b[l, 1]),
                          *ffn_w, (l, 1), row(ln_g[l, 2]), row(ln_b[l, 2]))
    return xt.reshape(bsz, seq, d)
```

```python
import numpy as np
import jax
import jax.numpy as jnp
from jax import lax
from jax.experimental import pallas as pl
from jax.experimental.pallas import tpu as pltpu

F32 = jnp.float32
BF16 = jnp.bfloat16

D_MODEL = 1024
DEPTH = 2
GROUP_W = 256
HEAD_DIM = 64
HEADS = 4
D_FF = 2816
ALPHA = (2.0 * DEPTH) ** 0.25
Q_LORA = 256
KV_LORA = 128
MLA_NOPE = 64
MLA_ROPE = 32
ROPE_HALF = MLA_ROPE // 2
ROPE_THETA = 10000.0
CONV_W = 4
LRU_C = 8.0
DILATED_PATTERNS = ((128, 1), (512, 4), (2048, 16))

V7X_LANES = 128
V7X_VMEM_BYTES = 64 * 1024 * 1024

FF_CHUNK = 256
N_FF_CHUNKS = D_FF // FF_CHUNK
TOKEN_TILE = 512
ML_CHUNK = 128
ATT_T = 256
ATT_COLS = 1024
MLA_TQ = 512
PAIR_W = 2 * HEAD_DIM
N_BIAS_TILES = 4

Z_A = 512
Z_LRU = 512
Z_ML = 1024
Z_G = 256

NEG_INF = float("-inf")
LOG2E = float(np.log2(np.e))


def _vmem_limit(nbytes):
    return int(min(nbytes, V7X_VMEM_BYTES - 6 * 1024 * 1024))


def _const_spec(shape):
    nd = len(shape)
    return pl.BlockSpec(shape, lambda *_: (0,) * nd, pipeline_mode=pl.Buffered(1))


def _stacked_spec(shape, lead):
    tail = shape[len(lead):]
    return pl.BlockSpec((None,) * len(lead) + tuple(tail), lambda *_: tuple(lead) + (0,) * len(tail),
                        pipeline_mode=pl.Buffered(1))


def _layernorm(v, g, b):
    mu = jnp.mean(v, axis=-1, keepdims=True)
    d = v - mu
    var = jnp.mean(d * d, axis=-1, keepdims=True)
    return d * lax.rsqrt(var + 1e-5) * g + b


def _rmsnorm(v, g, eps=1e-6):
    return v * lax.rsqrt(jnp.mean(v * v, axis=-1, keepdims=True) + eps) * g


def _dot(a, b):
    return jnp.dot(a, b, preferred_element_type=F32)


def _dot_nt(a, b):
    return lax.dot_general(a, b, (((1,), (1,)), ((), ())), preferred_element_type=F32)


def _head_of_lane(width, group=HEAD_DIM):
    return lax.broadcasted_iota(jnp.int32, (1, width), 1) // group


def _ffn_ln(x, w1_ref, w3_ref, w2_ref, acc_ref, g, b):
    xb = x.astype(BF16)
    for c in range(N_FF_CHUNKS):
        cols = slice(c * FF_CHUNK, (c + 1) * FF_CHUNK)
        h1 = _dot(xb, w1_ref[:, cols])
        h3 = _dot(xb, w3_ref[:, cols])
        h = (h1 * jax.nn.sigmoid(h1)) * h3
        part = _dot(h.astype(BF16), w2_ref[cols, :])
        if c == 0:
            acc_ref[...] = part
        else:
            acc_ref[...] += part
    return _layernorm(ALPHA * x + 0.5 * acc_ref[...], g, b)


def _ffn_inproj_kernel(x_ref, w1_ref, w3_ref, w2_ref, g_ref, b_ref, win_ref,
                       x1_ref, za_ref, zlru_ref, sq_ref, sk_ref, sv_ref, zml_ref, zg_ref, acc_ref):
    x1 = _ffn_ln(x_ref[...], w1_ref, w3_ref, w2_ref, acc_ref, g_ref[...], b_ref[...])
    x1_ref[...] = x1
    xb = x1.astype(BF16)
    off = 0
    for ref in (za_ref, zlru_ref, sq_ref, sk_ref, sv_ref, zml_ref, zg_ref):
        w = ref.shape[-1]
        ref[...] = _dot(xb, win_ref[:, off:off + w])
        off += w


def _ffn_inproj(x, w1, w3, w2, lj, g, b, win):
    n = x.shape[0]
    tm = TOKEN_TILE
    tile = lambda w: pl.BlockSpec((tm, w), lambda i: (i, 0))
    widths = (D_MODEL, Z_A, Z_LRU, GROUP_W, GROUP_W, GROUP_W, Z_ML, Z_G)
    return pl.pallas_call(
        _ffn_inproj_kernel,
        grid=(n // tm,),
        in_specs=[tile(D_MODEL), _stacked_spec(w1.shape, lj), _stacked_spec(w3.shape, lj), _stacked_spec(w2.shape, lj),
                  _const_spec(g.shape), _const_spec(b.shape), _stacked_spec(win.shape, lj[:1])],
        out_specs=[tile(w) for w in widths],
        out_shape=[jax.ShapeDtypeStruct((n, w), F32) for w in widths],
        scratch_shapes=[pltpu.VMEM((tm, D_MODEL), F32)],
        compiler_params=pltpu.CompilerParams(dimension_semantics=("arbitrary",),
                                             vmem_limit_bytes=_vmem_limit(56 * 1024 * 1024)),
        name="ffn_inproj",
    )(x, w1, w3, w2, g, b, win)


def _outproj_ffn_kernel(x_ref, ya_ref, yb_ref, yc_ref, yd_ref,
                        on_ref, wout_ref, g1_ref, b1_ref, w1_ref, w3_ref, w2_ref, g2_ref, b2_ref,
                        out_ref, acc_ref):
    y = jnp.zeros(x_ref.shape, F32)
    for gi, y_ref in enumerate((ya_ref, yb_ref, yc_ref, yd_ref)):
        lo = gi * GROUP_W
        yn = _rmsnorm(y_ref[...], on_ref[:, lo:lo + GROUP_W])
        y = y + _dot(yn.astype(BF16), wout_ref[lo:lo + GROUP_W, :])
    x2 = _layernorm(ALPHA * x_ref[...] + y, g1_ref[...], b1_ref[...])
    out_ref[...] = _ffn_ln(x2, w1_ref, w3_ref, w2_ref, acc_ref, g2_ref[...], b2_ref[...])


def _outproj_ffn(x, ys, on, wout, g1, b1, w1, w3, w2, lj, g2, b2):
    n = x.shape[0]
    tm = TOKEN_TILE
    tile = lambda w: pl.BlockSpec((tm, w), lambda i: (i, 0))
    consts = (on, wout, g1, b1, w1, w3, w2, g2, b2)
    const_specs = [_stacked_spec(c.shape, lj) if c.ndim == 4 else _const_spec(c.shape) for c in consts]
    return pl.pallas_call(
        _outproj_ffn_kernel,
        grid=(n // tm,),
        in_specs=[tile(D_MODEL)] + [tile(GROUP_W)] * len(ys) + const_specs,
        out_specs=tile(D_MODEL),
        out_shape=jax.ShapeDtypeStruct((n, D_MODEL), F32),
        scratch_shapes=[pltpu.VMEM((tm, D_MODEL), F32)],
        compiler_params=pltpu.CompilerParams(dimension_semantics=("arbitrary",),
                                             vmem_limit_bytes=_vmem_limit(56 * 1024 * 1024)),
        name="outproj_ffn",
    )(x, *ys, *consts)


def _flash_state_scratch():
    return [pltpu.VMEM((1, ATT_COLS), F32), pltpu.VMEM((1, ATT_COLS), F32), pltpu.VMEM((PAIR_W, ATT_COLS), F32)]


def _flash_reset(state):
    m_ref, l_ref, acc_ref = state
    m_ref[...] = jnp.full(m_ref.shape, NEG_INF, F32)
    l_ref[...] = jnp.zeros_like(l_ref)
    acc_ref[...] = jnp.zeros_like(acc_ref)


def _flash_pipe_scratch():
    return [pltpu.VMEM((2, ATT_T, ATT_COLS), F32), pltpu.VMEM((2, 1, ATT_COLS), F32)]


def _flash_attend(n_steps, qm, key_block, value_block, bias_block, state, pipe, groups):
    m_ref, l_ref, acc_ref = state
    s_ref, mx_ref = pipe
    _flash_reset(state)

    def scores(j, slot):
        bias = bias_block(j)
        s_t = _dot_nt(key_block(j), qm) + jnp.concatenate([bias] * (ATT_COLS // bias.shape[1]), axis=1)
        s_ref[slot] = s_t
        mx_ref[slot] = jnp.max(s_t, axis=0, keepdims=True)

    scores(0, 0)

    def step(j, carry):
        cur = lax.rem(j, 2)
        m = m_ref[...]
        m_new = jnp.maximum(m, mx_ref[cur])
        a = jnp.exp2(m - m_new)
        pr = jnp.exp2(s_ref[cur] - m_new)
        m_ref[...] = m_new
        l_ref[...] = a * l_ref[...] + jnp.sum(pr, axis=0, keepdims=True)
        prb = pr.astype(BF16)
        scores(jnp.minimum(j + 1, n_steps - 1), 1 - cur)
        vt_blk = value_block(j)
        for v_lo, c_lo, c_hi in groups:
            acc_ref[:, c_lo:c_hi] = (a[:, c_lo:c_hi] * acc_ref[:, c_lo:c_hi]
                                     + _dot(vt_blk[v_lo:v_lo + PAIR_W, :], prb[:, c_lo:c_hi]))
        return carry

    lax.fori_loop(0, n_steps, step, 0)


def _flash_result(state):
    _, l_ref, acc_ref = state
    return jnp.transpose(acc_ref[...] * (1.0 / l_ref[...]))


def _merge_pair(o_a, o_b):
    lane = lax.broadcasted_iota(jnp.int32, (1, PAIR_W), 1)
    return jnp.where(lane < HEAD_DIM, o_a, o_b)


def _store_transposed_values(v, vt_ref, j):
    vt_ref[j] = jnp.transpose(v).astype(BF16)


def _mla_kernel(za_ref, qn_ref, kvn_ref, wuq_ref, wukv_ref, cos_ref, sin_ref, o_ref,
                qm_scr, kp_scr, vt_scr, o_scr, bias_scr, m_scr, l_scr, acc_scr, s_scr, mx_scr):
    seq = za_ref.shape[0]
    state = (m_scr, l_scr, acc_scr)
    pipe = (s_scr, mx_scr)
    c_q = (MLA_NOPE + MLA_ROPE) ** -0.5 * LOG2E

    def prep(j, carry):
        rows = pl.ds(pl.multiple_of(j * ATT_T, ATT_T), ATT_T)
        cos, sin = cos_ref[rows, :], sin_ref[rows, :]

        def rope(r):
            return r * cos + pltpu.roll(r, V7X_LANES // 2, axis=1) * sin

        cq = _rmsnorm(za_ref[rows, 0:Q_LORA], qn_ref[...])
        q = _dot(cq.astype(BF16), wuq_ref[...]) * c_q
        for h in range(HEADS):
            lo = h * GROUP_W
            qm_scr[h, rows, 0:V7X_LANES] = q[:, lo:lo + V7X_LANES].astype(BF16)
            qm_scr[h, rows, V7X_LANES:] = rope(q[:, lo + V7X_LANES:lo + GROUP_W]).astype(BF16)
        ckv = _rmsnorm(za_ref[rows, Q_LORA:Q_LORA + KV_LORA], kvn_ref[...])
        kv = _dot(ckv.astype(BF16), wukv_ref[...])
        k_rope = rope(za_ref[rows, Q_LORA + KV_LORA:]).astype(BF16)
        for p in range(2):
            kp_scr[p, rows, 0:V7X_LANES] = kv[:, p * PAIR_W:(p + 1) * PAIR_W].astype(BF16)
            kp_scr[p, rows, V7X_LANES:] = k_rope
        _store_transposed_values(kv[:, GROUP_W:], vt_scr, j)
        return carry

    lax.fori_loop(0, seq // ATT_T, prep, 0)

    key = lax.broadcasted_iota(jnp.int32, (ATT_T, MLA_TQ), 0)
    qry = lax.broadcasted_iota(jnp.int32, (ATT_T, MLA_TQ), 1)
    n_diag = MLA_TQ // ATT_T
    bias_scr[0] = jnp.zeros((ATT_T, MLA_TQ), F32)
    for d in range(n_diag):
        bias_scr[1 + d] = jnp.where(key + d * ATT_T <= qry, 0.0, NEG_INF)
    groups = [(0, 0, ATT_COLS)]

    def q_block(i, carry):
        q0 = pl.multiple_of(i * MLA_TQ, MLA_TQ)
        q_rows = pl.ds(q0, MLA_TQ)
        first_diag = i * n_diag

        def pair(p, c):
            qm = jnp.concatenate([qm_scr[2 * p, q_rows, :], qm_scr[2 * p + 1, q_rows, :]], axis=0)
            v_lo = pl.multiple_of(p * PAIR_W, PAIR_W)
            _flash_attend(
                first_diag + n_diag, qm,
                lambda j: kp_scr[p, pl.ds(pl.multiple_of(j * ATT_T, ATT_T), ATT_T), :],
                lambda j: vt_scr.at[j, pl.ds(v_lo, PAIR_W), :],
                lambda j: bias_scr[jnp.maximum(j - first_diag + 1, 0)],
                state, pipe, groups)
            o_t = _flash_result(state)
            o_scr[p] = _merge_pair(o_t[:MLA_TQ], o_t[MLA_TQ:])
            return c

        lax.fori_loop(0, 2, pair, 0)
        for p in range(2):
            o_ref[q_rows, p * PAIR_W:(p + 1) * PAIR_W] = o_scr[p]
        return carry

    lax.fori_loop(0, seq // MLA_TQ, q_block, 0)


def _mla(za, qn, kvn, wuq, wukv, cos, sin):
    b, s, _ = za.shape
    consts = (qn, kvn, wuq, wukv, cos, sin)
    return pl.pallas_call(
        _mla_kernel,
        grid=(b,),
        in_specs=[pl.BlockSpec((None, s, Z_A), lambda i: (i, 0, 0))] + [_const_spec(c.shape) for c in consts],
        out_specs=pl.BlockSpec((None, s, GROUP_W), lambda i: (i, 0, 0)),
        out_shape=jax.ShapeDtypeStruct((b, s, GROUP_W), F32),
        scratch_shapes=([pltpu.VMEM((HEADS, s, GROUP_W), BF16), pltpu.VMEM((2, s, GROUP_W), BF16),
                         pltpu.VMEM((s // ATT_T, GROUP_W, ATT_T), BF16), pltpu.VMEM((2, MLA_TQ, PAIR_W), F32),
                         pltpu.VMEM((1 + MLA_TQ // ATT_T, ATT_T, MLA_TQ), F32)]
                        + _flash_state_scratch() + _flash_pipe_scratch()),
        compiler_params=pltpu.CompilerParams(dimension_semantics=("arbitrary",),
                                             vmem_limit_bytes=_vmem_limit(48 * 1024 * 1024)),
        name="mla",
    )(za, *consts)


def _swa_kernel(q_ref, k_ref, v_ref, bias_ref, o_ref, qm_scr, k_scr, vt_scr, m_scr, l_scr, acc_scr,
                s_scr, mx_scr):
    seq = q_ref.shape[0]
    state = (m_scr, l_scr, acc_scr)
    pipe = (s_scr, mx_scr)
    head = _head_of_lane(GROUP_W)
    c_q = HEAD_DIM ** -0.5 * LOG2E
    half = ATT_COLS // 2
    groups = [(0, 0, half), (PAIR_W, half, ATT_COLS)]

    def prep(j, carry):
        rows = pl.ds(pl.multiple_of(j * ATT_T, ATT_T), ATT_T)
        q = (q_ref[rows, :] * c_q).astype(BF16)
        for h in range(HEADS):
            qm_scr[h, rows, :] = jnp.where(head == h, q, jnp.zeros_like(q))
        k_scr[rows, :] = k_ref[rows, :].astype(BF16)
        _store_transposed_values(v_ref[rows, :], vt_scr, j)
        return carry

    lax.fori_loop(0, seq // ATT_T, prep, 0)

    def q_block(i, carry):
        q_rows = pl.ds(pl.multiple_of(i * ATT_T, ATT_T), ATT_T)
        qm = jnp.concatenate([qm_scr[h, q_rows, :] for h in range(HEADS)], axis=0)
        _flash_attend(
            i + 1, qm,
            lambda j: k_scr[pl.ds(pl.multiple_of(j * ATT_T, ATT_T), ATT_T), :],
            lambda j: vt_scr.at[j],
            lambda j: bias_ref[jnp.minimum(i - j, N_BIAS_TILES - 1)],
            state, pipe, groups)
        o_t = _flash_result(state)
        for p in range(2):
            lo = 2 * p * ATT_T
            o_ref[q_rows, p * PAIR_W:(p + 1) * PAIR_W] = _merge_pair(o_t[lo:lo + ATT_T], o_t[lo + ATT_T:lo + 2 * ATT_T])
        return carry

    lax.fori_loop(0, seq // ATT_T, q_block, 0)


def _swa_bias_tiles(seq):
    assert DILATED_PATTERNS[-1][0] >= seq - 1, "far tiles assume the widest window covers the sequence"
    assert all(w <= (N_BIAS_TILES - 2) * ATT_T for w, _ in DILATED_PATTERNS[:-1]), "narrow windows must end before the far tile"
    key = np.arange(ATT_T)[:, None]
    qry = np.arange(ATT_T)[None, :]
    tiles = []
    for dist in range(N_BIAS_TILES):
        d = dist * ATT_T + qry - key
        mult = sum(((d >= 0) & (d % dil == 0) & (d <= window)).astype(np.float32) for window, dil in DILATED_PATTERNS)
        with np.errstate(divide="ignore"):
            tiles.append(np.log2(mult))
    return jnp.asarray(np.stack(tiles), F32)


def _swa(q, k, v, bias):
    b, s, w = q.shape
    spec = pl.BlockSpec((None, s, w), lambda i: (i, 0, 0))
    return pl.pallas_call(
        _swa_kernel,
        grid=(b,),
        in_specs=[spec, spec, spec, _const_spec(bias.shape)],
        out_specs=spec,
        out_shape=jax.ShapeDtypeStruct((b, s, w), F32),
        scratch_shapes=([pltpu.VMEM((HEADS, s, w), BF16), pltpu.VMEM((s, w), BF16),
                         pltpu.VMEM((s // ATT_T, w, ATT_T), BF16)] + _flash_state_scratch()
                        + _flash_pipe_scratch()),
        compiler_params=pltpu.CompilerParams(dimension_semantics=("arbitrary",),
                                             vmem_limit_bytes=_vmem_limit(48 * 1024 * 1024)),
        name="swa",
    )(q, k, v, bias)


def _shift_rows(x, d, fill):
    row = lax.broadcasted_iota(jnp.int32, x.shape, 0)
    return jnp.where(row >= d, pltpu.roll(x, d, axis=0), fill)


def _softplus(x):
    return jnp.maximum(x, 0.0) + jnp.log1p(jnp.exp(-jnp.abs(x)))


def _gelu_tanh(x):
    c = np.sqrt(2.0 / np.pi).astype(np.float32)
    return 0.5 * x * (1.0 + jnp.tanh(c * (x + 0.044715 * (x * x * x))))


def _rglru_kernel(z_ref, cw_ref, cb_ref, wa_ref, ba_ref, wx_ref, bx_ref, lam_ref, o_ref):
    seq = z_ref.shape[0]
    x = z_ref[:, 0:GROUP_W]
    xc = cb_ref[...] + cw_ref[CONV_W - 1:CONV_W, :] * x
    for j in range(CONV_W - 1):
        xc = xc + cw_ref[j:j + 1, :] * _shift_rows(x, CONV_W - 1 - j, 0.0)
    xcb = xc.astype(BF16)
    r = jax.nn.sigmoid(_dot(xcb, wa_ref[...]) + ba_ref[...])
    i = jax.nn.sigmoid(_dot(xcb, wx_ref[...]) + bx_ref[...])
    log_a = -LRU_C * r * _softplus(-lam_ref[...])
    a = jnp.exp(log_a)
    u = jnp.sqrt(-jnp.tanh(log_a) * (a * a + 1.0)) * (i * xc)
    d = 1
    while d < seq:
        u = a * _shift_rows(u, d, 0.0) + u
        a = a * _shift_rows(a, d, 1.0)
        d *= 2
    o_ref[...] = u * _gelu_tanh(z_ref[:, GROUP_W:])


def _rglru(z, cw, cb, wa, ba, wx, bx, lam):
    b, s, _ = z.shape
    consts = (cw, cb, wa, ba, wx, bx, lam)
    return pl.pallas_call(
        _rglru_kernel,
        grid=(b,),
        in_specs=[pl.BlockSpec((None, s, Z_LRU), lambda i: (i, 0, 0))] + [_const_spec(c.shape) for c in consts],
        out_specs=pl.BlockSpec((None, s, GROUP_W), lambda i: (i, 0, 0)),
        out_shape=jax.ShapeDtypeStruct((b, s, GROUP_W), F32),
        compiler_params=pltpu.CompilerParams(dimension_semantics=("arbitrary",),
                                             vmem_limit_bytes=_vmem_limit(48 * 1024 * 1024)),
        name="rglru",
    )(z, *consts)


def _expand_heads(cols, head):
    out = jnp.zeros((cols.shape[0], GROUP_W), F32)
    for h in range(HEADS):
        out = out + jnp.where(head == h, cols[:, h:h + 1], 0.0)
    return out


def _mlstm_kernel(z_ref, g_ref, bi_ref, bf_ref, o_ref, c_scr, n_scr, m_scr):
    seq = z_ref.shape[0]
    lc = ML_CHUNK
    head = _head_of_lane(GROUP_W)
    row_head = lax.broadcasted_iota(jnp.int32, (GROUP_W, GROUP_W), 0) // HEAD_DIM
    same_head = row_head == head
    tri = lax.broadcasted_iota(jnp.int32, (lc, lc), 0) >= lax.broadcasted_iota(jnp.int32, (lc, lc), 1)
    c_scr[...] = jnp.zeros_like(c_scr)
    n_scr[...] = jnp.zeros_like(n_scr)
    m_scr[...] = jnp.zeros_like(m_scr)

    def chunk(ci, carry):
        t0 = pl.multiple_of(ci * lc, lc)
        rows = pl.ds(t0, lc)
        q = z_ref[rows, 0:GROUP_W]
        k = z_ref[rows, GROUP_W:2 * GROUP_W] * (HEAD_DIM ** -0.5)
        v = z_ref[rows, 2 * GROUP_W:3 * GROUP_W]
        og = z_ref[rows, 3 * GROUP_W:4 * GROUP_W]
        gates = g_ref[rows, :]
        log_i = _expand_heads(gates[:, 0:HEADS], head) + bi_ref[...]
        f_pre = _expand_heads(gates[:, HEADS:2 * HEADS], head) + bf_ref[...]
        log_f = jnp.minimum(f_pre, 0.0) - jnp.log1p(jnp.exp(-jnp.abs(f_pre)))
        bcum = log_f
        d = 1
        while d < lc:
            bcum = bcum + _shift_rows(bcum, d, 0.0)
            d *= 2
        g_tot = bcum[lc - 1:lc, :]
        m_prev = m_scr[...]
        c_prev = c_scr[...]
        n_prev = n_scr[...]
        qb, kb, vb = q.astype(BF16), k.astype(BF16), v.astype(BF16)
        qs = jnp.concatenate([jnp.where(head == h, qb, jnp.zeros_like(qb)) for h in range(HEADS)], axis=0)
        s_all = _dot_nt(qs, kb)
        q_c = _dot(qb, c_prev.astype(BF16))
        qn = q * n_prev
        key_row = jnp.transpose(log_i - bcum)
        inter = bcum + m_prev
        out = jnp.zeros((lc, GROUP_W), F32)
        for h in range(HEADS):
            lane0 = h * HEAD_DIM
            dmat = jnp.where(tri, bcum[:, lane0:lane0 + 1] + key_row[lane0:lane0 + 1, :], NEG_INF)
            inter_h = inter[:, lane0:lane0 + 1]
            mt = jnp.maximum(jnp.max(dmat, axis=-1, keepdims=True), inter_h)
            sw = s_all[h * lc:(h + 1) * lc] * jnp.exp(dmat - mt)
            w_inter = jnp.exp(inter_h - mt)
            num = _dot(sw.astype(BF16), vb) + w_inter * q_c
            qn_h = jnp.sum(jnp.where(head == h, qn, 0.0), axis=-1, keepdims=True)
            den = jnp.sum(sw, axis=-1, keepdims=True) + w_inter * qn_h
            hh = num / jnp.maximum(jnp.abs(den), jnp.exp(-mt))
            out = out + jnp.where(head == h, hh, 0.0)
        o_ref[rows, :] = jax.nn.sigmoid(og) * out
        wk = g_tot - bcum + log_i
        m_new = jnp.maximum(g_tot + m_prev, jnp.max(wk, axis=0, keepdims=True))
        decay = jnp.exp(g_tot + m_prev - m_new)
        kw = k * jnp.exp(wk - m_new)
        kv = _dot(jnp.transpose(kw).astype(BF16), vb)
        c_scr[...] = decay * c_prev + jnp.where(same_head, kv, 0.0)
        n_scr[...] = decay * n_prev + jnp.sum(kw, axis=0, keepdims=True)
        m_scr[...] = m_new
        return carry

    lax.fori_loop(0, seq // lc, chunk, 0)


def _mlstm(z, gates, bi, bf):
    b, s, _ = z.shape
    return pl.pallas_call(
        _mlstm_kernel,
        grid=(b,),
        in_specs=[pl.BlockSpec((None, s, Z_ML), lambda i: (i, 0, 0)),
                  pl.BlockSpec((None, s, Z_G), lambda i: (i, 0, 0)),
                  _const_spec(bi.shape), _const_spec(bf.shape)],
        out_specs=pl.BlockSpec((None, s, GROUP_W), lambda i: (i, 0, 0)),
        out_shape=jax.ShapeDtypeStruct((b, s, GROUP_W), F32),
        scratch_shapes=[pltpu.VMEM((GROUP_W, GROUP_W), F32), pltpu.VMEM((1, GROUP_W), F32),
                        pltpu.VMEM((1, GROUP_W), F32)],
        compiler_params=pltpu.CompilerParams(dimension_semantics=("arbitrary",),
                                             vmem_limit_bytes=_vmem_limit(48 * 1024 * 1024)),
        name="mlstm",
    )(z, gates, bi, bf)


def _dot_exact(a, b):
    return jnp.dot(a, b, precision=lax.Precision.HIGHEST, preferred_element_type=F32)


def _mlstm_kernel_t(z_ref, g_ref, bi_ref, bf_ref, o_ref, ct_scr, n_scr, m_scr):
    seq = z_ref.shape[0]
    lc = ML_CHUNK
    sub = lax.broadcasted_iota(jnp.int32, (lc, lc), 0)
    lan = lax.broadcasted_iota(jnp.int32, (lc, lc), 1)
    tri_incl = jnp.where(lan <= sub, 1.0, 0.0)
    visible = sub <= lan
    head = _head_of_lane(GROUP_W)
    expand = jnp.where(lax.broadcasted_iota(jnp.int32, (V7X_LANES, GROUP_W), 0) == head, 1.0, 0.0)
    same_head = lax.broadcasted_iota(jnp.int32, (GROUP_W, GROUP_W), 0) // HEAD_DIM == head
    n_rows = lax.broadcasted_iota(jnp.int32, (8, GROUP_W), 0) == head
    ct_scr[...] = jnp.zeros_like(ct_scr)
    n_scr[...] = jnp.zeros_like(n_scr)
    m_scr[...] = jnp.zeros_like(m_scr)

    def chunk(ci, carry):
        rows = pl.ds(pl.multiple_of(ci * lc, lc), lc)
        q = z_ref[rows, 0:GROUP_W].astype(BF16)
        k = z_ref[rows, GROUP_W:2 * GROUP_W] * (HEAD_DIM ** -0.5)
        v_t = jnp.transpose(z_ref[rows, 2 * GROUP_W:3 * GROUP_W]).astype(BF16)
        og = z_ref[rows, 3 * GROUP_W:4 * GROUP_W]
        log_i = g_ref[rows, 0:V7X_LANES] + bi_ref[...]
        f_pre = g_ref[rows, V7X_LANES:] + bf_ref[...]
        log_f = jnp.minimum(f_pre, 0.0) - jnp.log1p(jnp.exp(-jnp.abs(f_pre)))
        b_col = _dot_exact(tri_incl, log_f)
        key_col = log_i - b_col
        b_row = jnp.transpose(b_col)
        m_prev = m_scr[...]
        ct_prev = ct_scr[...]
        n_prev = n_scr[...]

        qs = jnp.concatenate([jnp.where(head == h, q, jnp.zeros_like(q)) for h in range(HEADS)], axis=0)
        s_all = _dot_nt(k.astype(BF16), qs)
        qc_t = _dot_nt(ct_prev.astype(BF16), q)
        qn = _dot_nt(jnp.where(n_rows, n_prev, 0.0).astype(BF16), q)
        outs = []
        for h in range(HEADS):
            b_t = b_row[h:h + 1, :]
            dmat = jnp.where(visible, b_t + key_col[:, h:h + 1], NEG_INF)
            inter = b_t + m_prev[:, h:h + 1]
            mt = jnp.maximum(jnp.max(dmat, axis=0, keepdims=True), inter)
            w_t = s_all[:, h * lc:(h + 1) * lc] * jnp.exp(dmat - mt)
            w_inter = jnp.exp(inter - mt)
            lanes = slice(h * HEAD_DIM, (h + 1) * HEAD_DIM)
            num = _dot(v_t[lanes, :], w_t.astype(BF16)) + w_inter * qc_t[lanes, :]
            den = jnp.sum(w_t, axis=0, keepdims=True) + w_inter * qn[h:h + 1, :]
            outs.append(num * (1.0 / jnp.maximum(jnp.abs(den), jnp.exp(-mt))))
        o_ref[rows, :] = jax.nn.sigmoid(og) * jnp.transpose(jnp.concatenate(outs, axis=0))

        g_tot = b_col[lc - 1:lc, :]
        wk = g_tot + key_col
        m_new = jnp.maximum(g_tot + m_prev, jnp.max(wk, axis=0, keepdims=True))
        decay = jnp.exp(g_tot + m_prev - m_new)
        wide = _dot_exact(jnp.concatenate([jnp.exp(wk - m_new), jnp.broadcast_to(decay, (8, V7X_LANES))], axis=0),
                          expand)
        kw = k * wide[:lc]
        decay_w = wide[lc:lc + 1]
        ct_scr[...] = decay_w * ct_prev + jnp.where(same_head, _dot(v_t, kw.astype(BF16)), 0.0)
        n_scr[...] = decay_w * n_prev + jnp.sum(kw, axis=0, keepdims=True)
        m_scr[...] = m_new
        return carry

    lax.fori_loop(0, seq // lc, chunk, 0, unroll=2)


def _mlstm_t(z, gates, bi, bf):
    b, s, _ = z.shape
    return pl.pallas_call(
        _mlstm_kernel_t,
        grid=(b,),
        in_specs=[pl.BlockSpec((None, s, Z_ML), lambda i: (i, 0, 0)),
                  pl.BlockSpec((None, s, Z_G), lambda i: (i, 0, 0)),
                  _const_spec(bi.shape), _const_spec(bf.shape)],
        out_specs=pl.BlockSpec((None, s, GROUP_W), lambda i: (i, 0, 0)),
        out_shape=jax.ShapeDtypeStruct((b, s, GROUP_W), F32),
        scratch_shapes=[pltpu.VMEM((GROUP_W, GROUP_W), F32), pltpu.VMEM((1, GROUP_W), F32),
                        pltpu.VMEM((1, V7X_LANES), F32)],
        compiler_params=pltpu.CompilerParams(dimension_semantics=("arbitrary",),
                                             vmem_limit_bytes=_vmem_limit(48 * 1024 * 1024)),
        name="mlstm",
    )(z, gates, bi, bf)


def _block_diag(w):
    out = jnp.zeros((GROUP_W, GROUP_W), w.dtype)
    for h in range(HEADS):
        out = out.at[h * HEAD_DIM:(h + 1) * HEAD_DIM, h * HEAD_DIM:(h + 1) * HEAD_DIM].set(w[h])
    return out


def _permute_w_in(w):
    kr = Q_LORA + KV_LORA
    zeros = lambda width: jnp.zeros(w.shape[:2] + (width,), w.dtype)
    gap = zeros(HEAD_DIM - ROPE_HALF)
    body_lo = kr + MLA_ROPE
    body_hi = body_lo + 9 * GROUP_W
    pad = zeros(V7X_LANES - HEADS)
    return jnp.concatenate([w[..., :kr], w[..., kr:kr + ROPE_HALF], gap, w[..., kr + ROPE_HALF:kr + MLA_ROPE], gap,
                            w[..., body_lo:body_hi], w[..., body_hi:body_hi + HEADS], pad,
                            w[..., body_hi + HEADS:], pad], axis=-1).astype(BF16)


def _mla_up_layout():
    per_q = MLA_NOPE + MLA_ROPE
    q_src, q_dst = [], []
    for h in range(HEADS):
        base = h * GROUP_W
        nope_lo = base + (h % 2) * HEAD_DIM
        q_src += [h * per_q + i for i in range(per_q)]
        q_dst += [nope_lo + i for i in range(MLA_NOPE)]
        q_dst += [base + V7X_LANES + i for i in range(ROPE_HALF)]
        q_dst += [base + V7X_LANES + HEAD_DIM + i for i in range(ROPE_HALF)]
    per_kv = MLA_NOPE + HEAD_DIM
    kn = [h * per_kv + i for h in range(HEADS) for i in range(MLA_NOPE)]
    vv = [h * per_kv + MLA_NOPE + i for h in range(HEADS) for i in range(HEAD_DIM)]
    return np.array(q_src), np.array(q_dst), np.array(kn + vv)


def _rope_tables(seq):
    freqs = ROPE_THETA ** (-jnp.arange(ROPE_HALF, dtype=F32) / ROPE_HALF)
    ang = jnp.arange(seq, dtype=F32)[:, None] * freqs[None, :]
    cos = jnp.tile(jnp.cos(ang), (1, 2 * HEADS))
    sin = jnp.tile(jnp.sin(ang), (1, HEADS))
    return cos, jnp.concatenate([-sin, sin], axis=1)


def kernel(x, ln_g, ln_b, ffn_w1, ffn_w3, ffn_w2, w_in, mla_q_norm, mla_kv_norm, mla_w_uq, mla_w_ukv,
           lru_conv_w, lru_conv_b, lru_w_a, lru_b_a, lru_w_x, lru_b_x, lru_lambda, ml_b_i, ml_b_f,
           out_norm, w_out):
    bsz, seq, d = x.shape
    n = bsz * seq
    row = lambda t: t.reshape(1, -1)
    q_src, q_dst, kv_perm = _mla_up_layout()
    cos, sin = _rope_tables(seq)
    swa_bias = _swa_bias_tiles(seq)
    xt = x.reshape(n, d)
    ffn_w = (ffn_w1.astype(BF16), ffn_w3.astype(BF16), ffn_w2.astype(BF16))
    w_in_p = _permute_w_in(w_in)
    for l in range(DEPTH):
        x1, za, zlru, sq, sk, sv, zml, zg = _ffn_inproj(
            xt, *ffn_w, (l, 0), row(ln_g[l, 0]), row(ln_b[l, 0]), w_in_p)
        seqv = lambda t: t.reshape(bsz, seq, t.shape[-1])
        wuq = jnp.zeros((Q_LORA, HEADS * GROUP_W), F32).at[:, q_dst].set(mla_w_uq[l][:, q_src]).astype(BF16)
        y_a = _mla(seqv(za), row(mla_q_norm[l]), row(mla_kv_norm[l]), wuq,
                   mla_w_ukv[l][:, kv_perm].astype(BF16), cos, sin)
        y_b = _rglru(seqv(zlru), lru_conv_w[l], row(lru_conv_b[l]),
                     _block_diag(lru_w_a[l]).astype(BF16), row(lru_b_a[l]),
                     _block_diag(lru_w_x[l]).astype(BF16), row(lru_b_x[l]), row(lru_lambda[l]))
        y_c = _swa(seqv(sq), seqv(sk), seqv(sv), swa_bias)
        lane_pad = lambda t: row(jnp.pad(t, (0, V7X_LANES - HEADS)))
        y_d = _mlstm_t(seqv(zml), seqv(zg), lane_pad(ml_b_i[l]), lane_pad(ml_b_f[l]))
        ys = [t.reshape(n, GROUP_W) for t in (y_a, y_b, y_c, y_d)]
        xt = _outproj_ffn(x1, ys, row(out_norm[l]), w_out[l].astype(BF16), row(ln_g[l, 1]), row(ln_b[l, 1]),
                          *ffn_w, (l, 1), row(ln_g[l, 2]), row(ln_b[l, 2]))
    return xt.reshape(bsz, seq, d)
```

```python
import numpy as np
import jax
import jax.numpy as jnp
from jax import lax
from jax.experimental import pallas as pl
from jax.experimental.pallas import tpu as pltpu

F32 = jnp.float32
BF16 = jnp.bfloat16

D_MODEL = 1024
DEPTH = 2
GROUP_W = 256
HEAD_DIM = 64
HEADS = 4
D_FF = 2816
ALPHA = (2.0 * DEPTH) ** 0.25
Q_LORA = 256
KV_LORA = 128
MLA_NOPE = 64
MLA_ROPE = 32
ROPE_HALF = MLA_ROPE // 2
ROPE_THETA = 10000.0
CONV_W = 4
LRU_C = 8.0
DILATED_PATTERNS = ((128, 1), (512, 4), (2048, 16))

V7X_LANES = 128
V7X_SUBLANES = 8
V7X_VMEM_BYTES = 64 * 1024 * 1024

FF_CHUNK = 256
N_FF_CHUNKS = D_FF // FF_CHUNK
TOKEN_TILE = 512
ML_CHUNK = 128
ATT_T = 256
ATT_COLS = 1024
MLA_TQ = 512
PAIR_W = 2 * HEAD_DIM
N_BIAS_TILES = 4

Z_A = 512
Z_LRU = 512
Z_ML = 1024
Z_G = 256

NEG_INF = float("-inf")
LOG2E = float(np.log2(np.e))


def _vmem_limit(nbytes):
    return int(min(nbytes, V7X_VMEM_BYTES - 6 * 1024 * 1024))


def _const_spec(shape):
    nd = len(shape)
    return pl.BlockSpec(shape, lambda *_: (0,) * nd, pipeline_mode=pl.Buffered(1))


def _stacked_spec(shape, lead):
    tail = shape[len(lead):]
    return pl.BlockSpec((None,) * len(lead) + tuple(tail), lambda *_: tuple(lead) + (0,) * len(tail),
                        pipeline_mode=pl.Buffered(1))


def _layernorm(v, g, b):
    mu = jnp.mean(v, axis=-1, keepdims=True)
    d = v - mu
    var = jnp.mean(d * d, axis=-1, keepdims=True)
    return d * lax.rsqrt(var + 1e-5) * g + b


def _rmsnorm(v, g, eps=1e-6):
    return v * lax.rsqrt(jnp.mean(v * v, axis=-1, keepdims=True) + eps) * g


def _dot(a, b):
    return jnp.dot(a, b, preferred_element_type=F32)


def _dot_nt(a, b):
    return lax.dot_general(a, b, (((1,), (1,)), ((), ())), preferred_element_type=F32)


def _head_of_lane(width, group=HEAD_DIM):
    return lax.broadcasted_iota(jnp.int32, (1, width), 1) // group


def _ffn_ln(x, w1_ref, w3_ref, w2_ref, acc_ref, g, b):
    xb = x.astype(BF16)
    for c in range(N_FF_CHUNKS):
        cols = slice(c * FF_CHUNK, (c + 1) * FF_CHUNK)
        h1 = _dot(xb, w1_ref[:, cols])
        h3 = _dot(xb, w3_ref[:, cols])
        h = (h1 * jax.nn.sigmoid(h1)) * h3
        part = _dot(h.astype(BF16), w2_ref[cols, :])
        if c == 0:
            acc_ref[...] = part
        else:
            acc_ref[...] += part
    return _layernorm(ALPHA * x + 0.5 * acc_ref[...], g, b)


def _ffn_inproj_kernel(x_ref, w1_ref, w3_ref, w2_ref, g_ref, b_ref, win_ref,
                       x1_ref, za_ref, zlru_ref, sq_ref, sk_ref, sv_ref, zml_ref, zg_ref, acc_ref):
    x1 = _ffn_ln(x_ref[...], w1_ref, w3_ref, w2_ref, acc_ref, g_ref[...], b_ref[...])
    x1_ref[...] = x1
    xb = x1.astype(BF16)
    off = 0
    for ref in (za_ref, zlru_ref, sq_ref, sk_ref, sv_ref, zml_ref, zg_ref):
        w = ref.shape[-1]
        ref[...] = _dot(xb, win_ref[:, off:off + w])
        off += w


def _ffn_inproj(x, w1, w3, w2, lj, g, b, win):
    n = x.shape[0]
    tm = TOKEN_TILE
    tile = lambda w: pl.BlockSpec((tm, w), lambda i: (i, 0))
    widths = (D_MODEL, Z_A, Z_LRU, GROUP_W, GROUP_W, GROUP_W, Z_ML, Z_G)
    return pl.pallas_call(
        _ffn_inproj_kernel,
        grid=(n // tm,),
        in_specs=[tile(D_MODEL), _stacked_spec(w1.shape, lj), _stacked_spec(w3.shape, lj), _stacked_spec(w2.shape, lj),
                  _const_spec(g.shape), _const_spec(b.shape), _stacked_spec(win.shape, lj[:1])],
        out_specs=[tile(w) for w in widths],
        out_shape=[jax.ShapeDtypeStruct((n, w), F32) for w in widths],
        scratch_shapes=[pltpu.VMEM((tm, D_MODEL), F32)],
        compiler_params=pltpu.CompilerParams(dimension_semantics=("arbitrary",),
                                             vmem_limit_bytes=_vmem_limit(56 * 1024 * 1024)),
        name="ffn_inproj",
    )(x, w1, w3, w2, g, b, win)


def _outproj_ffn_kernel(x_ref, ya_ref, yb_ref, yc_ref, yd_ref,
                        on_ref, wout_ref, g1_ref, b1_ref, w1_ref, w3_ref, w2_ref, g2_ref, b2_ref,
                        out_ref, acc_ref):
    y = jnp.zeros(x_ref.shape, F32)
    for gi, y_ref in enumerate((ya_ref, yb_ref, yc_ref, yd_ref)):
        lo = gi * GROUP_W
        yn = _rmsnorm(y_ref[...], on_ref[:, lo:lo + GROUP_W])
        y = y + _dot(yn.astype(BF16), wout_ref[lo:lo + GROUP_W, :])
    x2 = _layernorm(ALPHA * x_ref[...] + y, g1_ref[...], b1_ref[...])
    out_ref[...] = _ffn_ln(x2, w1_ref, w3_ref, w2_ref, acc_ref, g2_ref[...], b2_ref[...])


def _outproj_ffn(x, ys, on, wout, g1, b1, w1, w3, w2, lj, g2, b2):
    n = x.shape[0]
    tm = TOKEN_TILE
    tile = lambda w: pl.BlockSpec((tm, w), lambda i: (i, 0))
    consts = (on, wout, g1, b1, w1, w3, w2, g2, b2)
    const_specs = [_stacked_spec(c.shape, lj) if c.ndim == 4 else _const_spec(c.shape) for c in consts]
    return pl.pallas_call(
        _outproj_ffn_kernel,
        grid=(n // tm,),
        in_specs=[tile(D_MODEL)] + [tile(GROUP_W)] * len(ys) + const_specs,
        out_specs=tile(D_MODEL),
        out_shape=jax.ShapeDtypeStruct((n, D_MODEL), F32),
        scratch_shapes=[pltpu.VMEM((tm, D_MODEL), F32)],
        compiler_params=pltpu.CompilerParams(dimension_semantics=("arbitrary",),
                                             vmem_limit_bytes=_vmem_limit(56 * 1024 * 1024)),
        name="outproj_ffn",
    )(x, *ys, *consts)


def _flash_state_scratch():
    return [pltpu.VMEM((1, ATT_COLS), F32), pltpu.VMEM((1, ATT_COLS), F32), pltpu.VMEM((PAIR_W, ATT_COLS), F32)]


def _flash_reset(state):
    m_ref, l_ref, acc_ref = state
    m_ref[...] = jnp.full(m_ref.shape, NEG_INF, F32)
    l_ref[...] = jnp.zeros_like(l_ref)
    acc_ref[...] = jnp.zeros_like(acc_ref)


def _flash_pipe_scratch():
    return [pltpu.VMEM((2, ATT_T, ATT_COLS), F32), pltpu.VMEM((2, 1, ATT_COLS), F32)]


def _flash_attend(n_steps, qm, key_block, value_block, bias_block, state, pipe, groups):
    m_ref, l_ref, acc_ref = state
    s_ref, mx_ref = pipe
    _flash_reset(state)

    def scores(j, slot):
        bias = bias_block(j)
        s_t = _dot_nt(key_block(j), qm) + jnp.concatenate([bias] * (ATT_COLS // bias.shape[1]), axis=1)
        s_ref[slot] = s_t
        mx_ref[slot] = jnp.max(s_t, axis=0, keepdims=True)

    scores(0, 0)

    def consume(j, with_next):
        cur = lax.rem(j, 2)
        m = m_ref[...]
        m_new = jnp.maximum(m, mx_ref[cur])
        a = jnp.exp2(m - m_new)
        pr = jnp.exp2(s_ref[cur] - m_new)
        m_ref[...] = m_new
        l_ref[...] = a * l_ref[...] + jnp.sum(pr, axis=0, keepdims=True)
        prb = pr.astype(BF16)
        if with_next:
            scores(j + 1, 1 - cur)
        vt_blk = value_block(j)
        for v_lo, c_lo, c_hi in groups:
            acc_ref[:, c_lo:c_hi] = (a[:, c_lo:c_hi] * acc_ref[:, c_lo:c_hi]
                                     + _dot(vt_blk[v_lo:v_lo + PAIR_W, :], prb[:, c_lo:c_hi]))

    def step(j, carry):
        consume(j, True)
        return carry

    lax.fori_loop(0, n_steps - 1, step, 0)
    consume(n_steps - 1, False)


def _flash_result(state):
    _, l_ref, acc_ref = state
    return jnp.transpose(acc_ref[...] * (1.0 / l_ref[...]))


def _merge_pair(o_a, o_b):
    lane = lax.broadcasted_iota(jnp.int32, (1, PAIR_W), 1)
    return jnp.where(lane < HEAD_DIM, o_a, o_b)


def _store_transposed_values(v, vt_ref, j):
    vt_ref[j] = jnp.transpose(v).astype(BF16)


def _mla_kernel(za_ref, qn_ref, kvn_ref, wuq_ref, wukv_ref, cos_ref, sin_ref, o_ref,
                qm_scr, kp_scr, vt_scr, o_scr, bias_scr, m_scr, l_scr, acc_scr, s_scr, mx_scr):
    seq = za_ref.shape[0]
    state = (m_scr, l_scr, acc_scr)
    pipe = (s_scr, mx_scr)
    c_q = (MLA_NOPE + MLA_ROPE) ** -0.5 * LOG2E

    def prep(j, carry):
        rows = pl.ds(pl.multiple_of(j * ATT_T, ATT_T), ATT_T)
        cos, sin = cos_ref[rows, :], sin_ref[rows, :]

        def rope(r):
            return r * cos + pltpu.roll(r, V7X_LANES // 2, axis=1) * sin

        cq = _rmsnorm(za_ref[rows, 0:Q_LORA], qn_ref[...])
        q = _dot(cq.astype(BF16), wuq_ref[...]) * c_q
        for h in range(HEADS):
            lo = h * GROUP_W
            qm_scr[h, rows, 0:V7X_LANES] = q[:, lo:lo + V7X_LANES].astype(BF16)
            qm_scr[h, rows, V7X_LANES:] = rope(q[:, lo + V7X_LANES:lo + GROUP_W]).astype(BF16)
        ckv = _rmsnorm(za_ref[rows, Q_LORA:Q_LORA + KV_LORA], kvn_ref[...])
        kv = _dot(ckv.astype(BF16), wukv_ref[...])
        k_rope = rope(za_ref[rows, Q_LORA + KV_LORA:]).astype(BF16)
        for p in range(2):
            kp_scr[p, rows, 0:V7X_LANES] = kv[:, p * PAIR_W:(p + 1) * PAIR_W].astype(BF16)
            kp_scr[p, rows, V7X_LANES:] = k_rope
        _store_transposed_values(kv[:, GROUP_W:], vt_scr, j)
        return carry

    lax.fori_loop(0, seq // ATT_T, prep, 0)

    key = lax.broadcasted_iota(jnp.int32, (ATT_T, MLA_TQ), 0)
    qry = lax.broadcasted_iota(jnp.int32, (ATT_T, MLA_TQ), 1)
    n_diag = MLA_TQ // ATT_T
    bias_scr[0] = jnp.zeros((ATT_T, MLA_TQ), F32)
    for d in range(n_diag):
        bias_scr[1 + d] = jnp.where(key + d * ATT_T <= qry, 0.0, NEG_INF)
    groups = [(0, 0, ATT_COLS)]

    def q_block(i, carry):
        q0 = pl.multiple_of(i * MLA_TQ, MLA_TQ)
        q_rows = pl.ds(q0, MLA_TQ)
        first_diag = i * n_diag

        def pair(p, c):
            qm = jnp.concatenate([qm_scr[2 * p, q_rows, :], qm_scr[2 * p + 1, q_rows, :]], axis=0)
            v_lo = pl.multiple_of(p * PAIR_W, PAIR_W)
            _flash_attend(
                first_diag + n_diag, qm,
                lambda j: kp_scr[p, pl.ds(pl.multiple_of(j * ATT_T, ATT_T), ATT_T), :],
                lambda j: vt_scr.at[j, pl.ds(v_lo, PAIR_W), :],
                lambda j: bias_scr[jnp.maximum(j - first_diag + 1, 0)],
                state, pipe, groups)
            o_t = _flash_result(state)
            o_scr[p] = _merge_pair(o_t[:MLA_TQ], o_t[MLA_TQ:])
            return c

        lax.fori_loop(0, 2, pair, 0)
        for p in range(2):
            o_ref[q_rows, p * PAIR_W:(p + 1) * PAIR_W] = o_scr[p]
        return carry

    lax.fori_loop(0, seq // MLA_TQ, q_block, 0)


def _mla(za, qn, kvn, wuq, wukv, cos, sin):
    b, s, _ = za.shape
    consts = (qn, kvn, wuq, wukv, cos, sin)
    return pl.pallas_call(
        _mla_kernel,
        grid=(b,),
        in_specs=[pl.BlockSpec((None, s, Z_A), lambda i: (i, 0, 0))] + [_const_spec(c.shape) for c in consts],
        out_specs=pl.BlockSpec((None, s, GROUP_W), lambda i: (i, 0, 0)),
        out_shape=jax.ShapeDtypeStruct((b, s, GROUP_W), F32),
        scratch_shapes=([pltpu.VMEM((HEADS, s, GROUP_W), BF16), pltpu.VMEM((2, s, GROUP_W), BF16),
                         pltpu.VMEM((s // ATT_T, GROUP_W, ATT_T), BF16), pltpu.VMEM((2, MLA_TQ, PAIR_W), F32),
                         pltpu.VMEM((1 + MLA_TQ // ATT_T, ATT_T, MLA_TQ), F32)]
                        + _flash_state_scratch() + _flash_pipe_scratch()),
        compiler_params=pltpu.CompilerParams(dimension_semantics=("arbitrary",),
                                             vmem_limit_bytes=_vmem_limit(48 * 1024 * 1024)),
        name="mla",
    )(za, *consts)


def _swa_kernel(q_ref, k_ref, v_ref, bias_ref, o_ref, qm_scr, k_scr, vt_scr, m_scr, l_scr, acc_scr,
                s_scr, mx_scr):
    seq = q_ref.shape[0]
    state = (m_scr, l_scr, acc_scr)
    pipe = (s_scr, mx_scr)
    head = _head_of_lane(GROUP_W)
    c_q = HEAD_DIM ** -0.5 * LOG2E
    half = ATT_COLS // 2
    groups = [(0, 0, half), (PAIR_W, half, ATT_COLS)]

    def prep(j, carry):
        rows = pl.ds(pl.multiple_of(j * ATT_T, ATT_T), ATT_T)
        q = (q_ref[rows, :] * c_q).astype(BF16)
        for h in range(HEADS):
            qm_scr[h, rows, :] = jnp.where(head == h, q, jnp.zeros_like(q))
        k_scr[rows, :] = k_ref[rows, :].astype(BF16)
        _store_transposed_values(v_ref[rows, :], vt_scr, j)
        return carry

    lax.fori_loop(0, seq // ATT_T, prep, 0)

    def q_block(i, carry):
        q_rows = pl.ds(pl.multiple_of(i * ATT_T, ATT_T), ATT_T)
        qm = jnp.concatenate([qm_scr[h, q_rows, :] for h in range(HEADS)], axis=0)
        _flash_attend(
            i + 1, qm,
            lambda j: k_scr[pl.ds(pl.multiple_of(j * ATT_T, ATT_T), ATT_T), :],
            lambda j: vt_scr.at[j],
            lambda j: bias_ref[jnp.minimum(i - j, N_BIAS_TILES - 1)],
            state, pipe, groups)
        o_t = _flash_result(state)
        for p in range(2):
            lo = 2 * p * ATT_T
            o_ref[q_rows, p * PAIR_W:(p + 1) * PAIR_W] = _merge_pair(o_t[lo:lo + ATT_T], o_t[lo + ATT_T:lo + 2 * ATT_T])
        return carry

    lax.fori_loop(0, seq // ATT_T, q_block, 0)


def _swa_bias_tiles(seq):
    assert DILATED_PATTERNS[-1][0] >= seq - 1, "far tiles assume the widest window covers the sequence"
    assert all(w <= (N_BIAS_TILES - 2) * ATT_T for w, _ in DILATED_PATTERNS[:-1]), "narrow windows must end before the far tile"
    key = np.arange(ATT_T)[:, None]
    qry = np.arange(ATT_T)[None, :]
    tiles = []
    for dist in range(N_BIAS_TILES):
        d = dist * ATT_T + qry - key
        mult = sum(((d >= 0) & (d % dil == 0) & (d <= window)).astype(np.float32) for window, dil in DILATED_PATTERNS)
        with np.errstate(divide="ignore"):
            tiles.append(np.log2(mult))
    return jnp.asarray(np.stack(tiles), F32)


def _swa(q, k, v, bias):
    b, s, w = q.shape
    spec = pl.BlockSpec((None, s, w), lambda i: (i, 0, 0))
    return pl.pallas_call(
        _swa_kernel,
        grid=(b,),
        in_specs=[spec, spec, spec, _const_spec(bias.shape)],
        out_specs=spec,
        out_shape=jax.ShapeDtypeStruct((b, s, w), F32),
        scratch_shapes=([pltpu.VMEM((HEADS, s, w), BF16), pltpu.VMEM((s, w), BF16),
                         pltpu.VMEM((s // ATT_T, w, ATT_T), BF16)] + _flash_state_scratch()
                        + _flash_pipe_scratch()),
        compiler_params=pltpu.CompilerParams(dimension_semantics=("arbitrary",),
                                             vmem_limit_bytes=_vmem_limit(48 * 1024 * 1024)),
        name="swa",
    )(q, k, v, bias)


def _shift_rows(x, d, fill):
    row = lax.broadcasted_iota(jnp.int32, x.shape, 0)
    return jnp.where(row >= d, pltpu.roll(x, d, axis=0), fill)


def _softplus(x):
    return jnp.maximum(x, 0.0) + jnp.log1p(jnp.exp(-jnp.abs(x)))


def _gelu_tanh(x):
    c = np.sqrt(2.0 / np.pi).astype(np.float32)
    return 0.5 * x * (1.0 + jnp.tanh(c * (x + 0.044715 * (x * x * x))))


def _rglru_kernel(z_ref, cw_ref, cb_ref, wa_ref, ba_ref, wx_ref, bx_ref, lam_ref, o_ref, a_scr, h_scr):
    seq = z_ref.shape[0]
    h_scr[0:V7X_SUBLANES, :] = jnp.zeros((V7X_SUBLANES, GROUP_W), F32)
    h_scr[V7X_SUBLANES:V7X_SUBLANES + seq, :] = z_ref[:, 0:GROUP_W]
    xc = cb_ref[...]
    for j in range(CONV_W):
        start = V7X_SUBLANES - (CONV_W - 1 - j)
        xc = xc + cw_ref[j:j + 1, :] * h_scr[start:start + seq, :]
    xcb = xc.astype(BF16)
    r = jax.nn.sigmoid(_dot(xcb, wa_ref[...]) + ba_ref[...])
    i = jax.nn.sigmoid(_dot(xcb, wx_ref[...]) + bx_ref[...])
    log_a = -LRU_C * r * _softplus(-lam_ref[...])
    a = jnp.exp(log_a)
    gain2 = -jnp.tanh(log_a) * (a * a + 1.0)
    gain = jnp.where(gain2 > 0.0, gain2 * lax.rsqrt(gain2), 0.0)
    u = gain * (i * xc)
    groups = (seq // V7X_SUBLANES, V7X_SUBLANES, GROUP_W)
    a, u = a.reshape(groups), u.reshape(groups)
    in_group = lax.broadcasted_iota(jnp.int32, groups, 1)
    d = 1
    while d < V7X_SUBLANES:
        keep = in_group >= d
        u = a * jnp.where(keep, pltpu.roll(u, d, axis=1), 0.0) + u
        a = a * jnp.where(keep, pltpu.roll(a, d, axis=1), 1.0)
        d *= 2
    a_scr[...] = a.reshape(seq, GROUP_W)
    h_scr[0:seq, :] = u.reshape(seq, GROUP_W)

    def group(g, h_prev):
        rows = pl.ds(pl.multiple_of(g * V7X_SUBLANES, V7X_SUBLANES), V7X_SUBLANES)
        h = h_scr[rows, :] + a_scr[rows, :] * h_prev
        h_scr[rows, :] = h
        return jnp.broadcast_to(h[V7X_SUBLANES - 1:, :], h.shape)

    lax.fori_loop(0, seq // V7X_SUBLANES, group, jnp.zeros((V7X_SUBLANES, GROUP_W), F32), unroll=8)
    o_ref[...] = h_scr[0:seq, :] * _gelu_tanh(z_ref[:, GROUP_W:])


def _rglru(z, cw, cb, wa, ba, wx, bx, lam):
    b, s, _ = z.shape
    consts = (cw, cb, wa, ba, wx, bx, lam)
    return pl.pallas_call(
        _rglru_kernel,
        grid=(b,),
        in_specs=[pl.BlockSpec((None, s, Z_LRU), lambda i: (i, 0, 0))] + [_const_spec(c.shape) for c in consts],
        out_specs=pl.BlockSpec((None, s, GROUP_W), lambda i: (i, 0, 0)),
        out_shape=jax.ShapeDtypeStruct((b, s, GROUP_W), F32),
        scratch_shapes=[pltpu.VMEM((s, GROUP_W), F32), pltpu.VMEM((s + V7X_SUBLANES, GROUP_W), F32)],
        compiler_params=pltpu.CompilerParams(dimension_semantics=("arbitrary",),
                                             vmem_limit_bytes=_vmem_limit(48 * 1024 * 1024)),
        name="rglru",
    )(z, *consts)


def _expand_heads(cols, head):
    out = jnp.zeros((cols.shape[0], GROUP_W), F32)
    for h in range(HEADS):
        out = out + jnp.where(head == h, cols[:, h:h + 1], 0.0)
    return out


def _mlstm_kernel(z_ref, g_ref, bi_ref, bf_ref, o_ref, c_scr, n_scr, m_scr):
    seq = z_ref.shape[0]
    lc = ML_CHUNK
    head = _head_of_lane(GROUP_W)
    row_head = lax.broadcasted_iota(jnp.int32, (GROUP_W, GROUP_W), 0) // HEAD_DIM
    same_head = row_head == head
    tri = lax.broadcasted_iota(jnp.int32, (lc, lc), 0) >= lax.broadcasted_iota(jnp.int32, (lc, lc), 1)
    c_scr[...] = jnp.zeros_like(c_scr)
    n_scr[...] = jnp.zeros_like(n_scr)
    m_scr[...] = jnp.zeros_like(m_scr)

    def chunk(ci, carry):
        t0 = pl.multiple_of(ci * lc, lc)
        rows = pl.ds(t0, lc)
        q = z_ref[rows, 0:GROUP_W]
        k = z_ref[rows, GROUP_W:2 * GROUP_W] * (HEAD_DIM ** -0.5)
        v = z_ref[rows, 2 * GROUP_W:3 * GROUP_W]
        og = z_ref[rows, 3 * GROUP_W:4 * GROUP_W]
        gates = g_ref[rows, :]
        log_i = _expand_heads(gates[:, 0:HEADS], head) + bi_ref[...]
        f_pre = _expand_heads(gates[:, HEADS:2 * HEADS], head) + bf_ref[...]
        log_f = jnp.minimum(f_pre, 0.0) - jnp.log1p(jnp.exp(-jnp.abs(f_pre)))
        bcum = log_f
        d = 1
        while d < lc:
            bcum = bcum + _shift_rows(bcum, d, 0.0)
            d *= 2
        g_tot = bcum[lc - 1:lc, :]
        m_prev = m_scr[...]
        c_prev = c_scr[...]
        n_prev = n_scr[...]
        qb, kb, vb = q.astype(BF16), k.astype(BF16), v.astype(BF16)
        qs = jnp.concatenate([jnp.where(head == h, qb, jnp.zeros_like(qb)) for h in range(HEADS)], axis=0)
        s_all = _dot_nt(qs, kb)
        q_c = _dot(qb, c_prev.astype(BF16))
        qn = q * n_prev
        key_row = jnp.transpose(log_i - bcum)
        inter = bcum + m_prev
        out = jnp.zeros((lc, GROUP_W), F32)
        for h in range(HEADS):
            lane0 = h * HEAD_DIM
            dmat = jnp.where(tri, bcum[:, lane0:lane0 + 1] + key_row[lane0:lane0 + 1, :], NEG_INF)
            inter_h = inter[:, lane0:lane0 + 1]
            mt = jnp.maximum(jnp.max(dmat, axis=-1, keepdims=True), inter_h)
            sw = s_all[h * lc:(h + 1) * lc] * jnp.exp(dmat - mt)
            w_inter = jnp.exp(inter_h - mt)
            num = _dot(sw.astype(BF16), vb) + w_inter * q_c
            qn_h = jnp.sum(jnp.where(head == h, qn, 0.0), axis=-1, keepdims=True)
            den = jnp.sum(sw, axis=-1, keepdims=True) + w_inter * qn_h
            hh = num / jnp.maximum(jnp.abs(den), jnp.exp(-mt))
            out = out + jnp.where(head == h, hh, 0.0)
        o_ref[rows, :] = jax.nn.sigmoid(og) * out
        wk = g_tot - bcum + log_i
        m_new = jnp.maximum(g_tot + m_prev, jnp.max(wk, axis=0, keepdims=True))
        decay = jnp.exp(g_tot + m_prev - m_new)
        kw = k * jnp.exp(wk - m_new)
        kv = _dot(jnp.transpose(kw).astype(BF16), vb)
        c_scr[...] = decay * c_prev + jnp.where(same_head, kv, 0.0)
        n_scr[...] = decay * n_prev + jnp.sum(kw, axis=0, keepdims=True)
        m_scr[...] = m_new
        return carry

    lax.fori_loop(0, seq // lc, chunk, 0)


def _mlstm(z, gates, bi, bf):
    b, s, _ = z.shape
    return pl.pallas_call(
        _mlstm_kernel,
        grid=(b,),
        in_specs=[pl.BlockSpec((None, s, Z_ML), lambda i: (i, 0, 0)),
                  pl.BlockSpec((None, s, Z_G), lambda i: (i, 0, 0)),
                  _const_spec(bi.shape), _const_spec(bf.shape)],
        out_specs=pl.BlockSpec((None, s, GROUP_W), lambda i: (i, 0, 0)),
        out_shape=jax.ShapeDtypeStruct((b, s, GROUP_W), F32),
        scratch_shapes=[pltpu.VMEM((GROUP_W, GROUP_W), F32), pltpu.VMEM((1, GROUP_W), F32),
                        pltpu.VMEM((1, GROUP_W), F32)],
        compiler_params=pltpu.CompilerParams(dimension_semantics=("arbitrary",),
                                             vmem_limit_bytes=_vmem_limit(48 * 1024 * 1024)),
        name="mlstm",
    )(z, gates, bi, bf)


def _dot_exact(a, b):
    return jnp.dot(a, b, precision=lax.Precision.HIGHEST, preferred_element_type=F32)


def _mlstm_kernel_t(z_ref, g_ref, bi_ref, bf_ref, o_ref, ct_scr, n_scr, m_scr):
    seq = z_ref.shape[0]
    lc = ML_CHUNK
    sub = lax.broadcasted_iota(jnp.int32, (lc, lc), 0)
    lan = lax.broadcasted_iota(jnp.int32, (lc, lc), 1)
    tri_incl = jnp.where(lan <= sub, 1.0, 0.0)
    visible = sub <= lan
    head = _head_of_lane(GROUP_W)
    expand = jnp.where(lax.broadcasted_iota(jnp.int32, (V7X_LANES, GROUP_W), 0) == head, 1.0, 0.0)
    same_head = lax.broadcasted_iota(jnp.int32, (GROUP_W, GROUP_W), 0) // HEAD_DIM == head
    n_rows = lax.broadcasted_iota(jnp.int32, (8, GROUP_W), 0) == head
    ct_scr[...] = jnp.zeros_like(ct_scr)
    n_scr[...] = jnp.zeros_like(n_scr)
    m_scr[...] = jnp.zeros_like(m_scr)

    def chunk(ci, carry):
        rows = pl.ds(pl.multiple_of(ci * lc, lc), lc)
        q = z_ref[rows, 0:GROUP_W].astype(BF16)
        k = z_ref[rows, GROUP_W:2 * GROUP_W] * (HEAD_DIM ** -0.5)
        v_t = jnp.transpose(z_ref[rows, 2 * GROUP_W:3 * GROUP_W]).astype(BF16)
        og = z_ref[rows, 3 * GROUP_W:4 * GROUP_W]
        log_i = g_ref[rows, 0:V7X_LANES] + bi_ref[...]
        f_pre = g_ref[rows, V7X_LANES:] + bf_ref[...]
        log_f = jnp.minimum(f_pre, 0.0) - jnp.log1p(jnp.exp(-jnp.abs(f_pre)))
        b_col = _dot_exact(tri_incl, log_f)
        key_col = log_i - b_col
        b_row = jnp.transpose(b_col)
        m_prev = m_scr[...]
        ct_prev = ct_scr[...]
        n_prev = n_scr[...]

        qs = jnp.concatenate([jnp.where(head == h, q, jnp.zeros_like(q)) for h in range(HEADS)], axis=0)
        s_all = _dot_nt(k.astype(BF16), qs)
        qc_t = _dot_nt(ct_prev.astype(BF16), q)
        qn = _dot_nt(jnp.where(n_rows, n_prev, 0.0).astype(BF16), q)
        outs = []
        for h in range(HEADS):
            b_t = b_row[h:h + 1, :]
            dmat = jnp.where(visible, b_t + key_col[:, h:h + 1], NEG_INF)
            inter = b_t + m_prev[:, h:h + 1]
            mt = jnp.maximum(jnp.max(dmat, axis=0, keepdims=True), inter)
            w_t = s_all[:, h * lc:(h + 1) * lc] * jnp.exp(dmat - mt)
            w_inter = jnp.exp(inter - mt)
            lanes = slice(h * HEAD_DIM, (h + 1) * HEAD_DIM)
            num = _dot(v_t[lanes, :], w_t.astype(BF16)) + w_inter * qc_t[lanes, :]
            den = jnp.sum(w_t, axis=0, keepdims=True) + w_inter * qn[h:h + 1, :]
            outs.append(num * (1.0 / jnp.maximum(jnp.abs(den), jnp.exp(-mt))))
        o_ref[rows, :] = jax.nn.sigmoid(og) * jnp.transpose(jnp.concatenate(outs, axis=0))

        g_tot = b_col[lc - 1:lc, :]
        wk = g_tot + key_col
        m_new = jnp.maximum(g_tot + m_prev, jnp.max(wk, axis=0, keepdims=True))
        decay = jnp.exp(g_tot + m_prev - m_new)
        wide = _dot_exact(jnp.concatenate([jnp.exp(wk - m_new), jnp.broadcast_to(decay, (8, V7X_LANES))], axis=0),
                          expand)
        kw = k * wide[:lc]
        decay_w = wide[lc:lc + 1]
        ct_scr[...] = decay_w * ct_prev + jnp.where(same_head, _dot(v_t, kw.astype(BF16)), 0.0)
        n_scr[...] = decay_w * n_prev + jnp.sum(kw, axis=0, keepdims=True)
        m_scr[...] = m_new
        return carry

    lax.fori_loop(0, seq // lc, chunk, 0, unroll=2)


def _mlstm_t(z, gates, bi, bf):
    b, s, _ = z.shape
    return pl.pallas_call(
        _mlstm_kernel_t,
        grid=(b,),
        in_specs=[pl.BlockSpec((None, s, Z_ML), lambda i: (i, 0, 0)),
                  pl.BlockSpec((None, s, Z_G), lambda i: (i, 0, 0)),
                  _const_spec(bi.shape), _const_spec(bf.shape)],
        out_specs=pl.BlockSpec((None, s, GROUP_W), lambda i: (i, 0, 0)),
        out_shape=jax.ShapeDtypeStruct((b, s, GROUP_W), F32),
        scratch_shapes=[pltpu.VMEM((GROUP_W, GROUP_W), F32), pltpu.VMEM((1, GROUP_W), F32),
                        pltpu.VMEM((1, V7X_LANES), F32)],
        compiler_params=pltpu.CompilerParams(dimension_semantics=("arbitrary",),
                                             vmem_limit_bytes=_vmem_limit(48 * 1024 * 1024)),
        name="mlstm",
    )(z, gates, bi, bf)


def _block_diag(w):
    out = jnp.zeros((GROUP_W, GROUP_W), w.dtype)
    for h in range(HEADS):
        out = out.at[h * HEAD_DIM:(h + 1) * HEAD_DIM, h * HEAD_DIM:(h + 1) * HEAD_DIM].set(w[h])
    return out


def _permute_w_in(w):
    kr = Q_LORA + KV_LORA
    zeros = lambda width: jnp.zeros(w.shape[:2] + (width,), w.dtype)
    gap = zeros(HEAD_DIM - ROPE_HALF)
    body_lo = kr + MLA_ROPE
    body_hi = body_lo + 9 * GROUP_W
    pad = zeros(V7X_LANES - HEADS)
    return jnp.concatenate([w[..., :kr], w[..., kr:kr + ROPE_HALF], gap, w[..., kr + ROPE_HALF:kr + MLA_ROPE], gap,
                            w[..., body_lo:body_hi], w[..., body_hi:body_hi + HEADS], pad,
                            w[..., body_hi + HEADS:], pad], axis=-1).astype(BF16)


def _mla_up_layout():
    per_q = MLA_NOPE + MLA_ROPE
    q_src, q_dst = [], []
    for h in range(HEADS):
        base = h * GROUP_W
        nope_lo = base + (h % 2) * HEAD_DIM
        q_src += [h * per_q + i for i in range(per_q)]
        q_dst += [nope_lo + i for i in range(MLA_NOPE)]
        q_dst += [base + V7X_LANES + i for i in range(ROPE_HALF)]
        q_dst += [base + V7X_LANES + HEAD_DIM + i for i in range(ROPE_HALF)]
    per_kv = MLA_NOPE + HEAD_DIM
    kn = [h * per_kv + i for h in range(HEADS) for i in range(MLA_NOPE)]
    vv = [h * per_kv + MLA_NOPE + i for h in range(HEADS) for i in range(HEAD_DIM)]
    return np.array(q_src), np.array(q_dst), np.array(kn + vv)


def _rope_tables(seq):
    freqs = ROPE_THETA ** (-jnp.arange(ROPE_HALF, dtype=F32) / ROPE_HALF)
    ang = jnp.arange(seq, dtype=F32)[:, None] * freqs[None, :]
    cos = jnp.tile(jnp.cos(ang), (1, 2 * HEADS))
    sin = jnp.tile(jnp.sin(ang), (1, HEADS))
    return cos, jnp.concatenate([-sin, sin], axis=1)


def kernel(x, ln_g, ln_b, ffn_w1, ffn_w3, ffn_w2, w_in, mla_q_norm, mla_kv_norm, mla_w_uq, mla_w_ukv,
           lru_conv_w, lru_conv_b, lru_w_a, lru_b_a, lru_w_x, lru_b_x, lru_lambda, ml_b_i, ml_b_f,
           out_norm, w_out):
    bsz, seq, d = x.shape
    n = bsz * seq
    row = lambda t: t.reshape(1, -1)
    q_src, q_dst, kv_perm = _mla_up_layout()
    cos, sin = _rope_tables(seq)
    swa_bias = _swa_bias_tiles(seq)
    xt = x.reshape(n, d)
    ffn_w = (ffn_w1.astype(BF16), ffn_w3.astype(BF16), ffn_w2.astype(BF16))
    w_in_p = _permute_w_in(w_in)
    for l in range(DEPTH):
        x1, za, zlru, sq, sk, sv, zml, zg = _ffn_inproj(
            xt, *ffn_w, (l, 0), row(ln_g[l, 0]), row(ln_b[l, 0]), w_in_p)
        seqv = lambda t: t.reshape(bsz, seq, t.shape[-1])
        wuq = jnp.zeros((Q_LORA, HEADS * GROUP_W), F32).at[:, q_dst].set(mla_w_uq[l][:, q_src]).astype(BF16)
        y_a = _mla(seqv(za), row(mla_q_norm[l]), row(mla_kv_norm[l]), wuq,
                   mla_w_ukv[l][:, kv_perm].astype(BF16), cos, sin)
        y_b = _rglru(seqv(zlru), lru_conv_w[l], row(lru_conv_b[l]),
                     _block_diag(lru_w_a[l]).astype(BF16), row(lru_b_a[l]),
                     _block_diag(lru_w_x[l]).astype(BF16), row(lru_b_x[l]), row(lru_lambda[l]))
        y_c = _swa(seqv(sq), seqv(sk), seqv(sv), swa_bias)
        lane_pad = lambda t: row(jnp.pad(t, (0, V7X_LANES - HEADS)))
        y_d = _mlstm_t(seqv(zml), seqv(zg), lane_pad(ml_b_i[l]), lane_pad(ml_b_f[l]))
        ys = [t.reshape(n, GROUP_W) for t in (y_a, y_b, y_c, y_d)]
        xt = _outproj_ffn(x1, ys, row(out_norm[l]), w_out[l].astype(BF16), row(ln_g[l, 1]), row(ln_b[l, 1]),
                          *ffn_w, (l, 1), row(ln_g[l, 2]), row(ln_b[l, 2]))
    return xt.reshape(bsz, seq, d)
```

```python
import numpy as np
import jax
import jax.numpy as jnp
from jax import lax
from jax.experimental import pallas as pl
from jax.experimental.pallas import tpu as pltpu

F32 = jnp.float32
BF16 = jnp.bfloat16

D_MODEL = 1024
DEPTH = 2
GROUP_W = 256
HEAD_DIM = 64
HEADS = 4
D_FF = 2816
ALPHA = (2.0 * DEPTH) ** 0.25
Q_LORA = 256
KV_LORA = 128
MLA_NOPE = 64
MLA_ROPE = 32
ROPE_HALF = MLA_ROPE // 2
ROPE_THETA = 10000.0
CONV_W = 4
LRU_C = 8.0
DILATED_PATTERNS = ((128, 1), (512, 4), (2048, 16))

V7X_LANES = 128
V7X_SUBLANES = 8
V7X_VMEM_BYTES = 64 * 1024 * 1024

FF_CHUNK = 256
N_FF_CHUNKS = D_FF // FF_CHUNK
TOKEN_TILE = 512
ML_CHUNK = 128
ATT_T = 256
ATT_COLS = 1024
MLA_TQ = 512
PAIR_W = 2 * HEAD_DIM
N_BIAS_TILES = 4

Z_A = 512
Z_LRU = 512
Z_ML = 1024
Z_G = 128

NEG_INF = float("-inf")
LOG2E = float(np.log2(np.e))


def _vmem_limit(nbytes):
    return int(min(nbytes, V7X_VMEM_BYTES - 6 * 1024 * 1024))


def _const_spec(shape):
    nd = len(shape)
    return pl.BlockSpec(shape, lambda *_: (0,) * nd, pipeline_mode=pl.Buffered(1))


def _stacked_spec(shape, lead):
    tail = shape[len(lead):]
    return pl.BlockSpec((None,) * len(lead) + tuple(tail), lambda *_: tuple(lead) + (0,) * len(tail),
                        pipeline_mode=pl.Buffered(1))


def _layernorm(v, g, b):
    mu = jnp.mean(v, axis=-1, keepdims=True)
    d = v - mu
    var = jnp.mean(d * d, axis=-1, keepdims=True)
    return d * lax.rsqrt(var + 1e-5) * g + b


def _rmsnorm(v, g, eps=1e-6):
    return v * lax.rsqrt(jnp.mean(v * v, axis=-1, keepdims=True) + eps) * g


def _dot(a, b):
    return jnp.dot(a, b, preferred_element_type=F32)


def _dot_nt(a, b):
    return lax.dot_general(a, b, (((1,), (1,)), ((), ())), preferred_element_type=F32)


def _head_of_lane(width, group=HEAD_DIM):
    return lax.broadcasted_iota(jnp.int32, (1, width), 1) // group


def _ffn_ln(x, w1_ref, w3_ref, w2_ref, acc_ref, g, b):
    xb = x.astype(BF16)
    for c in range(N_FF_CHUNKS):
        cols = slice(c * FF_CHUNK, (c + 1) * FF_CHUNK)
        h1 = _dot(xb, w1_ref[:, cols])
        h3 = _dot(xb, w3_ref[:, cols])
        h = (h1 * jax.nn.sigmoid(h1)) * h3
        part = _dot(h.astype(BF16), w2_ref[cols, :])
        if c == 0:
            acc_ref[...] = part
        else:
            acc_ref[...] += part
    return _layernorm(ALPHA * x + 0.5 * acc_ref[...], g, b)


def _ffn_inproj_kernel(x_ref, w1_ref, w3_ref, w2_ref, g_ref, b_ref, win_ref,
                       x1_ref, za_ref, zlru_ref, sq_ref, sk_ref, sv_ref, zml_ref, zg_ref, acc_ref):
    x1 = _ffn_ln(x_ref[...], w1_ref, w3_ref, w2_ref, acc_ref, g_ref[...], b_ref[...])
    x1_ref[...] = x1
    xb = x1.astype(BF16)
    off = 0
    for ref in (za_ref, zlru_ref, sq_ref, sk_ref, sv_ref, zml_ref, zg_ref):
        w = ref.shape[-1]
        ref[...] = _dot(xb, win_ref[:, off:off + w])
        off += w


def _ffn_inproj(x, w1, w3, w2, lj, g, b, win):
    n = x.shape[0]
    tm = TOKEN_TILE
    tile = lambda w: pl.BlockSpec((tm, w), lambda i: (i, 0))
    widths = (D_MODEL, Z_A, Z_LRU, GROUP_W, GROUP_W, GROUP_W, Z_ML, Z_G)
    return pl.pallas_call(
        _ffn_inproj_kernel,
        grid=(n // tm,),
        in_specs=[tile(D_MODEL), _stacked_spec(w1.shape, lj), _stacked_spec(w3.shape, lj), _stacked_spec(w2.shape, lj),
                  _const_spec(g.shape), _const_spec(b.shape), _stacked_spec(win.shape, lj[:1])],
        out_specs=[tile(w) for w in widths],
        out_shape=[jax.ShapeDtypeStruct((n, w), F32) for w in widths],
        scratch_shapes=[pltpu.VMEM((tm, D_MODEL), F32)],
        compiler_params=pltpu.CompilerParams(dimension_semantics=("arbitrary",),
                                             vmem_limit_bytes=_vmem_limit(56 * 1024 * 1024)),
        name="ffn_inproj",
    )(x, w1, w3, w2, g, b, win)


def _outproj_ffn_kernel(x_ref, ya_ref, yb_ref, yc_ref, yd_ref,
                        on_ref, wout_ref, g1_ref, b1_ref, w1_ref, w3_ref, w2_ref, g2_ref, b2_ref,
                        out_ref, acc_ref):
    y = jnp.zeros(x_ref.shape, F32)
    for gi, y_ref in enumerate((ya_ref, yb_ref, yc_ref, yd_ref)):
        lo = gi * GROUP_W
        yn = _rmsnorm(y_ref[...], on_ref[:, lo:lo + GROUP_W])
        y = y + _dot(yn.astype(BF16), wout_ref[lo:lo + GROUP_W, :])
    x2 = _layernorm(ALPHA * x_ref[...] + y, g1_ref[...], b1_ref[...])
    out_ref[...] = _ffn_ln(x2, w1_ref, w3_ref, w2_ref, acc_ref, g2_ref[...], b2_ref[...])


def _outproj_ffn(x, ys, on, wout, g1, b1, w1, w3, w2, lj, g2, b2):
    n = x.shape[0]
    tm = TOKEN_TILE
    tile = lambda w: pl.BlockSpec((tm, w), lambda i: (i, 0))
    consts = (on, wout, g1, b1, w1, w3, w2, g2, b2)
    const_specs = [_stacked_spec(c.shape, lj) if c.ndim == 4 else _const_spec(c.shape) for c in consts]
    return pl.pallas_call(
        _outproj_ffn_kernel,
        grid=(n // tm,),
        in_specs=[tile(D_MODEL)] + [tile(GROUP_W)] * len(ys) + const_specs,
        out_specs=tile(D_MODEL),
        out_shape=jax.ShapeDtypeStruct((n, D_MODEL), F32),
        scratch_shapes=[pltpu.VMEM((tm, D_MODEL), F32)],
        compiler_params=pltpu.CompilerParams(dimension_semantics=("arbitrary",),
                                             vmem_limit_bytes=_vmem_limit(56 * 1024 * 1024)),
        name="outproj_ffn",
    )(x, *ys, *consts)


def _flash_state_scratch():
    return [pltpu.VMEM((1, ATT_COLS), F32), pltpu.VMEM((1, ATT_COLS), F32), pltpu.VMEM((PAIR_W, ATT_COLS), F32)]


def _flash_reset(state):
    m_ref, l_ref, acc_ref = state
    m_ref[...] = jnp.full(m_ref.shape, NEG_INF, F32)
    l_ref[...] = jnp.zeros_like(l_ref)
    acc_ref[...] = jnp.zeros_like(acc_ref)


def _flash_pipe_scratch():
    return [pltpu.VMEM((2, ATT_T, ATT_COLS), F32), pltpu.VMEM((2, 1, ATT_COLS), F32)]


def _flash_attend(n_steps, qm, key_block, value_block, bias_block, state, pipe, groups):
    m_ref, l_ref, acc_ref = state
    s_ref, mx_ref = pipe
    _flash_reset(state)

    def scores(j, slot):
        bias = bias_block(j)
        s_t = _dot_nt(key_block(j), qm) + jnp.concatenate([bias] * (ATT_COLS // bias.shape[1]), axis=1)
        s_ref[slot] = s_t
        mx_ref[slot] = jnp.max(s_t, axis=0, keepdims=True)

    scores(0, 0)

    def consume(j, with_next):
        cur = lax.rem(j, 2)
        m = m_ref[...]
        m_new = jnp.maximum(m, mx_ref[cur])
        a = jnp.exp2(m - m_new)
        pr = jnp.exp2(s_ref[cur] - m_new)
        m_ref[...] = m_new
        l_ref[...] = a * l_ref[...] + jnp.sum(pr, axis=0, keepdims=True)
        prb = pr.astype(BF16)
        if with_next:
            scores(j + 1, 1 - cur)
        vt_blk = value_block(j)
        for v_lo, c_lo, c_hi in groups:
            acc_ref[:, c_lo:c_hi] = (a[:, c_lo:c_hi] * acc_ref[:, c_lo:c_hi]
                                     + _dot(vt_blk[v_lo:v_lo + PAIR_W, :], prb[:, c_lo:c_hi]))

    def step(j, carry):
        consume(j, True)
        return carry

    lax.fori_loop(0, n_steps - 1, step, 0)
    consume(n_steps - 1, False)


def _flash_result(state):
    _, l_ref, acc_ref = state
    return jnp.transpose(acc_ref[...] * (1.0 / l_ref[...]))


def _merge_pair(o_a, o_b):
    lane = lax.broadcasted_iota(jnp.int32, (1, PAIR_W), 1)
    return jnp.where(lane < HEAD_DIM, o_a, o_b)


def _store_transposed_values(v, vt_ref, j):
    vt_ref[j] = jnp.transpose(v).astype(BF16)


def _mla_kernel(za_ref, qn_ref, kvn_ref, wuq_ref, wukv_ref, cos_ref, sin_ref, o_ref,
                qm_scr, kp_scr, vt_scr, o_scr, bias_scr, m_scr, l_scr, acc_scr, s_scr, mx_scr):
    seq = za_ref.shape[0]
    state = (m_scr, l_scr, acc_scr)
    pipe = (s_scr, mx_scr)
    c_q = (MLA_NOPE + MLA_ROPE) ** -0.5 * LOG2E

    def prep(j, carry):
        rows = pl.ds(pl.multiple_of(j * ATT_T, ATT_T), ATT_T)
        cos, sin = cos_ref[rows, :], sin_ref[rows, :]

        def rope(r):
            return r * cos + pltpu.roll(r, V7X_LANES // 2, axis=1) * sin

        cq = _rmsnorm(za_ref[rows, 0:Q_LORA], qn_ref[...])
        q = _dot(cq.astype(BF16), wuq_ref[...]) * c_q
        for h in range(HEADS):
            lo = h * GROUP_W
            qm_scr[h, rows, 0:V7X_LANES] = q[:, lo:lo + V7X_LANES].astype(BF16)
            qm_scr[h, rows, V7X_LANES:] = rope(q[:, lo + V7X_LANES:lo + GROUP_W]).astype(BF16)
        ckv = _rmsnorm(za_ref[rows, Q_LORA:Q_LORA + KV_LORA], kvn_ref[...])
        kv = _dot(ckv.astype(BF16), wukv_ref[...])
        k_rope = rope(za_ref[rows, Q_LORA + KV_LORA:]).astype(BF16)
        for p in range(2):
            kp_scr[p, rows, 0:V7X_LANES] = kv[:, p * PAIR_W:(p + 1) * PAIR_W].astype(BF16)
            kp_scr[p, rows, V7X_LANES:] = k_rope
        _store_transposed_values(kv[:, GROUP_W:], vt_scr, j)
        return carry

    lax.fori_loop(0, seq // ATT_T, prep, 0)

    key = lax.broadcasted_iota(jnp.int32, (ATT_T, MLA_TQ), 0)
    qry = lax.broadcasted_iota(jnp.int32, (ATT_T, MLA_TQ), 1)
    n_diag = MLA_TQ // ATT_T
    bias_scr[0] = jnp.zeros((ATT_T, MLA_TQ), F32)
    for d in range(n_diag):
        bias_scr[1 + d] = jnp.where(key + d * ATT_T <= qry, 0.0, NEG_INF)
    groups = [(0, 0, ATT_COLS)]

    def q_block(i, carry):
        q0 = pl.multiple_of(i * MLA_TQ, MLA_TQ)
        q_rows = pl.ds(q0, MLA_TQ)
        first_diag = i * n_diag

        def pair(p, c):
            qm = jnp.concatenate([qm_scr[2 * p, q_rows, :], qm_scr[2 * p + 1, q_rows, :]], axis=0)
            v_lo = pl.multiple_of(p * PAIR_W, PAIR_W)
            _flash_attend(
                first_diag + n_diag, qm,
                lambda j: kp_scr[p, pl.ds(pl.multiple_of(j * ATT_T, ATT_T), ATT_T), :],
                lambda j: vt_scr.at[j, pl.ds(v_lo, PAIR_W), :],
                lambda j: bias_scr[jnp.maximum(j - first_diag + 1, 0)],
                state, pipe, groups)
            o_t = _flash_result(state)
            o_scr[p] = _merge_pair(o_t[:MLA_TQ], o_t[MLA_TQ:])
            return c

        lax.fori_loop(0, 2, pair, 0)
        for p in range(2):
            o_ref[q_rows, p * PAIR_W:(p + 1) * PAIR_W] = o_scr[p]
        return carry

    lax.fori_loop(0, seq // MLA_TQ, q_block, 0)


def _mla(za, qn, kvn, wuq, wukv, cos, sin):
    b, s, _ = za.shape
    consts = (qn, kvn, wuq, wukv, cos, sin)
    return pl.pallas_call(
        _mla_kernel,
        grid=(b,),
        in_specs=[pl.BlockSpec((None, s, Z_A), lambda i: (i, 0, 0))] + [_const_spec(c.shape) for c in consts],
        out_specs=pl.BlockSpec((None, s, GROUP_W), lambda i: (i, 0, 0)),
        out_shape=jax.ShapeDtypeStruct((b, s, GROUP_W), F32),
        scratch_shapes=([pltpu.VMEM((HEADS, s, GROUP_W), BF16), pltpu.VMEM((2, s, GROUP_W), BF16),
                         pltpu.VMEM((s // ATT_T, GROUP_W, ATT_T), BF16), pltpu.VMEM((2, MLA_TQ, PAIR_W), F32),
                         pltpu.VMEM((1 + MLA_TQ // ATT_T, ATT_T, MLA_TQ), F32)]
                        + _flash_state_scratch() + _flash_pipe_scratch()),
        compiler_params=pltpu.CompilerParams(dimension_semantics=("arbitrary",),
                                             vmem_limit_bytes=_vmem_limit(48 * 1024 * 1024)),
        name="mla",
    )(za, *consts)


def _swa_kernel(q_ref, k_ref, v_ref, bias_ref, o_ref, qm_scr, k_scr, vt_scr, m_scr, l_scr, acc_scr,
                s_scr, mx_scr):
    seq = q_ref.shape[0]
    state = (m_scr, l_scr, acc_scr)
    pipe = (s_scr, mx_scr)
    head = _head_of_lane(GROUP_W)
    c_q = HEAD_DIM ** -0.5 * LOG2E
    half = ATT_COLS // 2
    groups = [(0, 0, half), (PAIR_W, half, ATT_COLS)]

    def prep(j, carry):
        rows = pl.ds(pl.multiple_of(j * ATT_T, ATT_T), ATT_T)
        q = (q_ref[rows, :] * c_q).astype(BF16)
        for h in range(HEADS):
            qm_scr[h, rows, :] = jnp.where(head == h, q, jnp.zeros_like(q))
        k_scr[rows, :] = k_ref[rows, :].astype(BF16)
        _store_transposed_values(v_ref[rows, :], vt_scr, j)
        return carry

    lax.fori_loop(0, seq // ATT_T, prep, 0)

    def q_block(i, carry):
        q_rows = pl.ds(pl.multiple_of(i * ATT_T, ATT_T), ATT_T)
        qm = jnp.concatenate([qm_scr[h, q_rows, :] for h in range(HEADS)], axis=0)
        _flash_attend(
            i + 1, qm,
            lambda j: k_scr[pl.ds(pl.multiple_of(j * ATT_T, ATT_T), ATT_T), :],
            lambda j: vt_scr.at[j],
            lambda j: bias_ref[jnp.minimum(i - j, N_BIAS_TILES - 1)],
            state, pipe, groups)
        o_t = _flash_result(state)
        for p in range(2):
            lo = 2 * p * ATT_T
            o_ref[q_rows, p * PAIR_W:(p + 1) * PAIR_W] = _merge_pair(o_t[lo:lo + ATT_T], o_t[lo + ATT_T:lo + 2 * ATT_T])
        return carry

    lax.fori_loop(0, seq // ATT_T, q_block, 0)


def _swa_bias_tiles(seq):
    assert DILATED_PATTERNS[-1][0] >= seq - 1, "far tiles assume the widest window covers the sequence"
    assert all(w <= (N_BIAS_TILES - 2) * ATT_T for w, _ in DILATED_PATTERNS[:-1]), "narrow windows must end before the far tile"
    key = np.arange(ATT_T)[:, None]
    qry = np.arange(ATT_T)[None, :]
    tiles = []
    for dist in range(N_BIAS_TILES):
        d = dist * ATT_T + qry - key
        mult = sum(((d >= 0) & (d % dil == 0) & (d <= window)).astype(np.float32) for window, dil in DILATED_PATTERNS)
        with np.errstate(divide="ignore"):
            tiles.append(np.log2(mult))
    return jnp.asarray(np.stack(tiles), F32)


def _swa(q, k, v, bias):
    b, s, w = q.shape
    spec = pl.BlockSpec((None, s, w), lambda i: (i, 0, 0))
    return pl.pallas_call(
        _swa_kernel,
        grid=(b,),
        in_specs=[spec, spec, spec, _const_spec(bias.shape)],
        out_specs=spec,
        out_shape=jax.ShapeDtypeStruct((b, s, w), F32),
        scratch_shapes=([pltpu.VMEM((HEADS, s, w), BF16), pltpu.VMEM((s, w), BF16),
                         pltpu.VMEM((s // ATT_T, w, ATT_T), BF16)] + _flash_state_scratch()
                        + _flash_pipe_scratch()),
        compiler_params=pltpu.CompilerParams(dimension_semantics=("arbitrary",),
                                             vmem_limit_bytes=_vmem_limit(48 * 1024 * 1024)),
        name="swa",
    )(q, k, v, bias)


def _shift_rows(x, d, fill):
    row = lax.broadcasted_iota(jnp.int32, x.shape, 0)
    return jnp.where(row >= d, pltpu.roll(x, d, axis=0), fill)


def _softplus(x):
    return jnp.maximum(x, 0.0) + jnp.log1p(jnp.exp(-jnp.abs(x)))


def _gelu_tanh(x):
    c = np.sqrt(2.0 / np.pi).astype(np.float32)
    return 0.5 * x * (1.0 + jnp.tanh(c * (x + 0.044715 * (x * x * x))))


def _rglru_kernel(z_ref, cw_ref, cb_ref, wa_ref, ba_ref, wx_ref, bx_ref, lam_ref, o_ref, a_scr, h_scr):
    seq = z_ref.shape[0]
    h_scr[0:V7X_SUBLANES, :] = jnp.zeros((V7X_SUBLANES, GROUP_W), F32)
    h_scr[V7X_SUBLANES:V7X_SUBLANES + seq, :] = z_ref[:, 0:GROUP_W]
    xc = cb_ref[...]
    for j in range(CONV_W):
        start = V7X_SUBLANES - (CONV_W - 1 - j)
        xc = xc + cw_ref[j:j + 1, :] * h_scr[start:start + seq, :]
    xcb = xc.astype(BF16)
    r = jax.nn.sigmoid(_dot(xcb, wa_ref[...]) + ba_ref[...])
    i = jax.nn.sigmoid(_dot(xcb, wx_ref[...]) + bx_ref[...])
    log_a = -LRU_C * r * _softplus(-lam_ref[...])
    a = jnp.exp(log_a)
    gain2 = -jnp.tanh(log_a) * (a * a + 1.0)
    gain = jnp.where(gain2 > 0.0, gain2 * lax.rsqrt(gain2), 0.0)
    u = gain * (i * xc)
    groups = (seq // V7X_SUBLANES, V7X_SUBLANES, GROUP_W)
    a, u = a.reshape(groups), u.reshape(groups)
    in_group = lax.broadcasted_iota(jnp.int32, groups, 1)
    d = 1
    while d < V7X_SUBLANES:
        keep = in_group >= d
        u = a * jnp.where(keep, pltpu.roll(u, d, axis=1), 0.0) + u
        a = a * jnp.where(keep, pltpu.roll(a, d, axis=1), 1.0)
        d *= 2
    a_scr[...] = a.reshape(seq, GROUP_W)
    h_scr[0:seq, :] = u.reshape(seq, GROUP_W)

    def group(g, h_prev):
        rows = pl.ds(pl.multiple_of(g * V7X_SUBLANES, V7X_SUBLANES), V7X_SUBLANES)
        h = h_scr[rows, :] + a_scr[rows, :] * h_prev
        h_scr[rows, :] = h
        return jnp.broadcast_to(h[V7X_SUBLANES - 1:, :], h.shape)

    lax.fori_loop(0, seq // V7X_SUBLANES, group, jnp.zeros((V7X_SUBLANES, GROUP_W), F32), unroll=8)
    o_ref[...] = h_scr[0:seq, :] * _gelu_tanh(z_ref[:, GROUP_W:])


def _rglru(z, cw, cb, wa, ba, wx, bx, lam):
    b, s, _ = z.shape
    consts = (cw, cb, wa, ba, wx, bx, lam)
    return pl.pallas_call(
        _rglru_kernel,
        grid=(b,),
        in_specs=[pl.BlockSpec((None, s, Z_LRU), lambda i: (i, 0, 0))] + [_const_spec(c.shape) for c in consts],
        out_specs=pl.BlockSpec((None, s, GROUP_W), lambda i: (i, 0, 0)),
        out_shape=jax.ShapeDtypeStruct((b, s, GROUP_W), F32),
        scratch_shapes=[pltpu.VMEM((s, GROUP_W), F32), pltpu.VMEM((s + V7X_SUBLANES, GROUP_W), F32)],
        compiler_params=pltpu.CompilerParams(dimension_semantics=("arbitrary",),
                                             vmem_limit_bytes=_vmem_limit(48 * 1024 * 1024)),
        name="rglru",
    )(z, *consts)


def _expand_heads(cols, head):
    out = jnp.zeros((cols.shape[0], GROUP_W), F32)
    for h in range(HEADS):
        out = out + jnp.where(head == h, cols[:, h:h + 1], 0.0)
    return out


def _mlstm_kernel(z_ref, g_ref, bi_ref, bf_ref, o_ref, c_scr, n_scr, m_scr):
    seq = z_ref.shape[0]
    lc = ML_CHUNK
    head = _head_of_lane(GROUP_W)
    row_head = lax.broadcasted_iota(jnp.int32, (GROUP_W, GROUP_W), 0) // HEAD_DIM
    same_head = row_head == head
    tri = lax.broadcasted_iota(jnp.int32, (lc, lc), 0) >= lax.broadcasted_iota(jnp.int32, (lc, lc), 1)
    c_scr[...] = jnp.zeros_like(c_scr)
    n_scr[...] = jnp.zeros_like(n_scr)
    m_scr[...] = jnp.zeros_like(m_scr)

    def chunk(ci, carry):
        t0 = pl.multiple_of(ci * lc, lc)
        rows = pl.ds(t0, lc)
        q = z_ref[rows, 0:GROUP_W]
        k = z_ref[rows, GROUP_W:2 * GROUP_W] * (HEAD_DIM ** -0.5)
        v = z_ref[rows, 2 * GROUP_W:3 * GROUP_W]
        og = z_ref[rows, 3 * GROUP_W:4 * GROUP_W]
        gates = g_ref[rows, :]
        log_i = _expand_heads(gates[:, 0:HEADS], head) + bi_ref[...]
        f_pre = _expand_heads(gates[:, HEADS:2 * HEADS], head) + bf_ref[...]
        log_f = jnp.minimum(f_pre, 0.0) - jnp.log1p(jnp.exp(-jnp.abs(f_pre)))
        bcum = log_f
        d = 1
        while d < lc:
            bcum = bcum + _shift_rows(bcum, d, 0.0)
            d *= 2
        g_tot = bcum[lc - 1:lc, :]
        m_prev = m_scr[...]
        c_prev = c_scr[...]
        n_prev = n_scr[...]
        qb, kb, vb = q.astype(BF16), k.astype(BF16), v.astype(BF16)
        qs = jnp.concatenate([jnp.where(head == h, qb, jnp.zeros_like(qb)) for h in range(HEADS)], axis=0)
        s_all = _dot_nt(qs, kb)
        q_c = _dot(qb, c_prev.astype(BF16))
        qn = q * n_prev
        key_row = jnp.transpose(log_i - bcum)
        inter = bcum + m_prev
        out = jnp.zeros((lc, GROUP_W), F32)
        for h in range(HEADS):
            lane0 = h * HEAD_DIM
            dmat = jnp.where(tri, bcum[:, lane0:lane0 + 1] + key_row[lane0:lane0 + 1, :], NEG_INF)
            inter_h = inter[:, lane0:lane0 + 1]
            mt = jnp.maximum(jnp.max(dmat, axis=-1, keepdims=True), inter_h)
            sw = s_all[h * lc:(h + 1) * lc] * jnp.exp(dmat - mt)
            w_inter = jnp.exp(inter_h - mt)
            num = _dot(sw.astype(BF16), vb) + w_inter * q_c
            qn_h = jnp.sum(jnp.where(head == h, qn, 0.0), axis=-1, keepdims=True)
            den = jnp.sum(sw, axis=-1, keepdims=True) + w_inter * qn_h
            hh = num / jnp.maximum(jnp.abs(den), jnp.exp(-mt))
            out = out + jnp.where(head == h, hh, 0.0)
        o_ref[rows, :] = jax.nn.sigmoid(og) * out
        wk = g_tot - bcum + log_i
        m_new = jnp.maximum(g_tot + m_prev, jnp.max(wk, axis=0, keepdims=True))
        decay = jnp.exp(g_tot + m_prev - m_new)
        kw = k * jnp.exp(wk - m_new)
        kv = _dot(jnp.transpose(kw).astype(BF16), vb)
        c_scr[...] = decay * c_prev + jnp.where(same_head, kv, 0.0)
        n_scr[...] = decay * n_prev + jnp.sum(kw, axis=0, keepdims=True)
        m_scr[...] = m_new
        return carry

    lax.fori_loop(0, seq // lc, chunk, 0)


def _mlstm(z, gates, bi, bf):
    b, s, _ = z.shape
    return pl.pallas_call(
        _mlstm_kernel,
        grid=(b,),
        in_specs=[pl.BlockSpec((None, s, Z_ML), lambda i: (i, 0, 0)),
                  pl.BlockSpec((None, s, Z_G), lambda i: (i, 0, 0)),
                  _const_spec(bi.shape), _const_spec(bf.shape)],
        out_specs=pl.BlockSpec((None, s, GROUP_W), lambda i: (i, 0, 0)),
        out_shape=jax.ShapeDtypeStruct((b, s, GROUP_W), F32),
        scratch_shapes=[pltpu.VMEM((GROUP_W, GROUP_W), F32), pltpu.VMEM((1, GROUP_W), F32),
                        pltpu.VMEM((1, GROUP_W), F32)],
        compiler_params=pltpu.CompilerParams(dimension_semantics=("arbitrary",),
                                             vmem_limit_bytes=_vmem_limit(48 * 1024 * 1024)),
        name="mlstm",
    )(z, gates, bi, bf)


def _dot_exact(a, b):
    return jnp.dot(a, b, precision=lax.Precision.HIGHEST, preferred_element_type=F32)


def _per_head_rows(rows, height):
    return jnp.concatenate([jnp.broadcast_to(rows[h:h + 1, :], (height, rows.shape[1])) for h in range(HEADS)], axis=0)


def _mlstm_kernel_t(z_ref, g_ref, gb_ref, o_ref, ct_scr, n_scr, m_scr):
    seq = z_ref.shape[0]
    lc = ML_CHUNK
    sub = lax.broadcasted_iota(jnp.int32, (lc, lc), 0)
    lan = lax.broadcasted_iota(jnp.int32, (lc, lc), 1)
    tri_incl = jnp.where(sub <= lan, 1.0, 0.0)
    visible = sub <= lan
    head = _head_of_lane(GROUP_W)
    same_head = lax.broadcasted_iota(jnp.int32, (GROUP_W, GROUP_W), 0) // HEAD_DIM == head
    n_rows = lax.broadcasted_iota(jnp.int32, (V7X_SUBLANES, GROUP_W), 0) == head
    pad_rows = jnp.zeros((lc - V7X_SUBLANES, lc), F32)
    ct_scr[...] = jnp.zeros_like(ct_scr)
    n_scr[...] = jnp.zeros_like(n_scr)
    m_scr[...] = jnp.zeros_like(m_scr)

    def chunk(ci, carry):
        rows = pl.ds(pl.multiple_of(ci * lc, lc), lc)
        q = z_ref[rows, 0:GROUP_W].astype(BF16)
        k = (z_ref[rows, GROUP_W:2 * GROUP_W] * (HEAD_DIM ** -0.5)).astype(BF16)
        v_t = jnp.transpose(z_ref[rows, 2 * GROUP_W:3 * GROUP_W])
        og = z_ref[rows, 3 * GROUP_W:4 * GROUP_W]
        pre = jnp.transpose(g_ref[rows, :])[0:V7X_SUBLANES, :] + gb_ref[...]
        log_f = jnp.minimum(pre, 0.0) - jnp.log1p(jnp.exp(-jnp.abs(pre)))
        b_row = pltpu.roll(_dot_exact(log_f, tri_incl), HEADS, axis=0)
        key_row = pre - b_row
        key_col = jnp.transpose(jnp.concatenate([key_row, pad_rows], axis=0))
        m_prev = m_scr[...]
        ct_prev = ct_scr[...]
        n_prev = n_scr[...]
        inter_all = b_row + m_prev

        qs = jnp.concatenate([jnp.where(head == h, q, jnp.zeros_like(q)) for h in range(HEADS)], axis=0)
        s_all = _dot_nt(k, qs)
        qc_t = _dot_nt(ct_prev.astype(BF16), q)
        qn = _dot_nt(n_prev.astype(BF16), q)
        outs = []
        for h in range(HEADS):
            b_t = b_row[h:h + 1, :]
            dmat = jnp.where(visible, b_t + key_col[:, h:h + 1], NEG_INF)
            inter = inter_all[h:h + 1, :]
            mt = jnp.maximum(jnp.max(dmat, axis=0, keepdims=True), inter)
            w_t = s_all[:, h * lc:(h + 1) * lc] * jnp.exp(dmat - mt)
            w_inter = jnp.exp(inter - mt)
            lanes = slice(h * HEAD_DIM, (h + 1) * HEAD_DIM)
            num = _dot(v_t[lanes, :].astype(BF16), w_t.astype(BF16)) + w_inter * qc_t[lanes, :]
            den = jnp.sum(w_t, axis=0, keepdims=True) + w_inter * qn[h:h + 1, :]
            outs.append(num * (1.0 / jnp.maximum(jnp.abs(den), jnp.exp(-mt))))
        o_ref[rows, :] = jax.nn.sigmoid(og) * jnp.transpose(jnp.concatenate(outs, axis=0))

        g_all = jnp.broadcast_to(b_row[:, lc - 1:lc], b_row.shape)
        wk = g_all + key_row
        m_new = jnp.maximum(g_all + m_prev, jnp.max(wk, axis=1, keepdims=True))
        decay = jnp.exp(g_all + m_prev - m_new)
        w_row = jnp.exp(wk - m_new)
        vw_t = (v_t * _per_head_rows(w_row, HEAD_DIM)).astype(BF16)
        decay_rows = _per_head_rows(decay, HEAD_DIM)
        ct_scr[...] = (jnp.concatenate([decay_rows, decay_rows], axis=1) * ct_prev
                       + jnp.where(same_head, _dot(vw_t, k), 0.0))
        n_scr[...] = (jnp.concatenate([decay, decay], axis=1) * n_prev
                      + jnp.where(n_rows, _dot(w_row.astype(BF16), k), 0.0))
        m_scr[...] = m_new
        return carry

    lax.fori_loop(0, seq // lc, chunk, 0, unroll=4)


def _mlstm_t(z, gates, gate_bias):
    b, s, _ = z.shape
    return pl.pallas_call(
        _mlstm_kernel_t,
        grid=(b,),
        in_specs=[pl.BlockSpec((None, s, Z_ML), lambda i: (i, 0, 0)),
                  pl.BlockSpec((None, s, Z_G), lambda i: (i, 0, 0)),
                  _const_spec(gate_bias.shape)],
        out_specs=pl.BlockSpec((None, s, GROUP_W), lambda i: (i, 0, 0)),
        out_shape=jax.ShapeDtypeStruct((b, s, GROUP_W), F32),
        scratch_shapes=[pltpu.VMEM((GROUP_W, GROUP_W), F32), pltpu.VMEM((V7X_SUBLANES, GROUP_W), F32),
                        pltpu.VMEM((V7X_SUBLANES, ML_CHUNK), F32)],
        compiler_params=pltpu.CompilerParams(dimension_semantics=("arbitrary",),
                                             vmem_limit_bytes=_vmem_limit(48 * 1024 * 1024)),
        name="mlstm",
    )(z, gates, gate_bias)


def _block_diag(w):
    out = jnp.zeros((GROUP_W, GROUP_W), w.dtype)
    for h in range(HEADS):
        out = out.at[h * HEAD_DIM:(h + 1) * HEAD_DIM, h * HEAD_DIM:(h + 1) * HEAD_DIM].set(w[h])
    return out


def _permute_w_in(w):
    kr = Q_LORA + KV_LORA
    zeros = lambda width: jnp.zeros(w.shape[:2] + (width,), w.dtype)
    gap = zeros(HEAD_DIM - ROPE_HALF)
    body_lo = kr + MLA_ROPE
    body_hi = body_lo + 9 * GROUP_W
    pad = zeros(Z_G - 2 * HEADS)
    return jnp.concatenate([w[..., :kr], w[..., kr:kr + ROPE_HALF], gap, w[..., kr + ROPE_HALF:kr + MLA_ROPE], gap,
                            w[..., body_lo:body_hi], w[..., body_hi:], pad], axis=-1).astype(BF16)


def _mla_up_layout():
    per_q = MLA_NOPE + MLA_ROPE
    q_src, q_dst = [], []
    for h in range(HEADS):
        base = h * GROUP_W
        nope_lo = base + (h % 2) * HEAD_DIM
        q_src += [h * per_q + i for i in range(per_q)]
        q_dst += [nope_lo + i for i in range(MLA_NOPE)]
        q_dst += [base + V7X_LANES + i for i in range(ROPE_HALF)]
        q_dst += [base + V7X_LANES + HEAD_DIM + i for i in range(ROPE_HALF)]
    per_kv = MLA_NOPE + HEAD_DIM
    kn = [h * per_kv + i for h in range(HEADS) for i in range(MLA_NOPE)]
    vv = [h * per_kv + MLA_NOPE + i for h in range(HEADS) for i in range(HEAD_DIM)]
    return np.array(q_src), np.array(q_dst), np.array(kn + vv)


def _rope_tables(seq):
    freqs = ROPE_THETA ** (-jnp.arange(ROPE_HALF, dtype=F32) / ROPE_HALF)
    ang = jnp.arange(seq, dtype=F32)[:, None] * freqs[None, :]
    cos = jnp.tile(jnp.cos(ang), (1, 2 * HEADS))
    sin = jnp.tile(jnp.sin(ang), (1, HEADS))
    return cos, jnp.concatenate([-sin, sin], axis=1)


def kernel(x, ln_g, ln_b, ffn_w1, ffn_w3, ffn_w2, w_in, mla_q_norm, mla_kv_norm, mla_w_uq, mla_w_ukv,
           lru_conv_w, lru_conv_b, lru_w_a, lru_b_a, lru_w_x, lru_b_x, lru_lambda, ml_b_i, ml_b_f,
           out_norm, w_out):
    bsz, seq, d = x.shape
    n = bsz * seq
    row = lambda t: t.reshape(1, -1)
    q_src, q_dst, kv_perm = _mla_up_layout()
    cos, sin = _rope_tables(seq)
    swa_bias = _swa_bias_tiles(seq)
    xt = x.reshape(n, d)
    ffn_w = (ffn_w1.astype(BF16), ffn_w3.astype(BF16), ffn_w2.astype(BF16))
    w_in_p = _permute_w_in(w_in)
    for l in range(DEPTH):
        x1, za, zlru, sq, sk, sv, zml, zg = _ffn_inproj(
            xt, *ffn_w, (l, 0), row(ln_g[l, 0]), row(ln_b[l, 0]), w_in_p)
        seqv = lambda t: t.reshape(bsz, seq, t.shape[-1])
        wuq = jnp.zeros((Q_LORA, HEADS * GROUP_W), F32).at[:, q_dst].set(mla_w_uq[l][:, q_src]).astype(BF16)
        y_a = _mla(seqv(za), row(mla_q_norm[l]), row(mla_kv_norm[l]), wuq,
                   mla_w_ukv[l][:, kv_perm].astype(BF16), cos, sin)
        y_b = _rglru(seqv(zlru), lru_conv_w[l], row(lru_conv_b[l]),
                     _block_diag(lru_w_a[l]).astype(BF16), row(lru_b_a[l]),
                     _block_diag(lru_w_x[l]).astype(BF16), row(lru_b_x[l]), row(lru_lambda[l]))
        y_c = _swa(seqv(sq), seqv(sk), seqv(sv), swa_bias)
        gate_bias = jnp.broadcast_to(jnp.concatenate([ml_b_i[l], ml_b_f[l]])[:, None], (2 * HEADS, ML_CHUNK))
        y_d = _mlstm_t(seqv(zml), seqv(zg), gate_bias)
        ys = [t.reshape(n, GROUP_W) for t in (y_a, y_b, y_c, y_d)]
        xt = _outproj_ffn(x1, ys, row(out_norm[l]), w_out[l].astype(BF16), row(ln_g[l, 1]), row(ln_b[l, 1]),
                          *ffn_w, (l, 1), row(ln_g[l, 2]), row(ln_b[l, 2]))
    return xt.reshape(bsz, seq, d)
```

```python
import numpy as np
import jax
import jax.numpy as jnp
from jax import lax
from jax.experimental import pallas as pl
from jax.experimental.pallas import tpu as pltpu

F32 = jnp.float32
BF16 = jnp.bfloat16

D_MODEL = 1024
DEPTH = 2
GROUP_W = 256
HEAD_DIM = 64
HEADS = 4
D_FF = 2816
ALPHA = (2.0 * DEPTH) ** 0.25
Q_LORA = 256
KV_LORA = 128
MLA_NOPE = 64
MLA_ROPE = 32
ROPE_HALF = MLA_ROPE // 2
ROPE_THETA = 10000.0
CONV_W = 4
LRU_C = 8.0
DILATED_PATTERNS = ((128, 1), (512, 4), (2048, 16))

V7X_LANES = 128
V7X_SUBLANES = 8
V7X_VMEM_BYTES = 64 * 1024 * 1024

FF_CHUNK = 256
N_FF_CHUNKS = D_FF // FF_CHUNK
TOKEN_TILE = 512
ML_CHUNK = 128
ATT_T = 256
ATT_COLS = 1024
MLA_TQ = 512
PAIR_W = 2 * HEAD_DIM
N_BIAS_TILES = 4

Z_A = 512
Z_LRU = 512
Z_ML = 1024
Z_G = 128

NEG_INF = float("-inf")
LOG2E = float(np.log2(np.e))


MIB = 1024 * 1024
FFN_VMEM_LIMIT = 56 * MIB
MIXER_VMEM_LIMIT = 48 * MIB
assert max(FFN_VMEM_LIMIT, MIXER_VMEM_LIMIT) < V7X_VMEM_BYTES


def _const_spec(shape):
    nd = len(shape)
    return pl.BlockSpec(shape, lambda *_: (0,) * nd, pipeline_mode=pl.Buffered(1))


def _stacked_spec(shape, lead):
    tail = shape[len(lead):]
    return pl.BlockSpec((None,) * len(lead) + tuple(tail), lambda *_: tuple(lead) + (0,) * len(tail),
                        pipeline_mode=pl.Buffered(1))


def _layernorm(v, g, b):
    mu = jnp.mean(v, axis=-1, keepdims=True)
    d = v - mu
    var = jnp.mean(d * d, axis=-1, keepdims=True)
    return d * lax.rsqrt(var + 1e-5) * g + b


def _rmsnorm(v, g, eps=1e-6):
    return v * lax.rsqrt(jnp.mean(v * v, axis=-1, keepdims=True) + eps) * g


def _dot(a, b):
    return jnp.dot(a, b, preferred_element_type=F32)


def _dot_nt(a, b):
    return lax.dot_general(a, b, (((1,), (1,)), ((), ())), preferred_element_type=F32)


def _head_of_lane(width, group=HEAD_DIM):
    return lax.broadcasted_iota(jnp.int32, (1, width), 1) // group


def _ffn_ln(x, w1_ref, w3_ref, w2_ref, acc_ref, g, b):
    xb = x.astype(BF16)
    for c in range(N_FF_CHUNKS):
        cols = slice(c * FF_CHUNK, (c + 1) * FF_CHUNK)
        h1 = _dot(xb, w1_ref[:, cols])
        h3 = _dot(xb, w3_ref[:, cols])
        h = (h1 * jax.nn.sigmoid(h1)) * h3
        part = _dot(h.astype(BF16), w2_ref[cols, :])
        if c == 0:
            acc_ref[...] = part
        else:
            acc_ref[...] += part
    return _layernorm(ALPHA * x + 0.5 * acc_ref[...], g, b)


def _ffn_inproj_kernel(x_ref, w1_ref, w3_ref, w2_ref, g_ref, b_ref, win_ref,
                       x1_ref, za_ref, zlru_ref, sq_ref, sk_ref, sv_ref, zml_ref, zg_ref, acc_ref):
    x1 = _ffn_ln(x_ref[...], w1_ref, w3_ref, w2_ref, acc_ref, g_ref[...], b_ref[...])
    x1_ref[...] = x1
    xb = x1.astype(BF16)
    off = 0
    for ref in (za_ref, zlru_ref, sq_ref, sk_ref, sv_ref, zml_ref, zg_ref):
        w = ref.shape[-1]
        ref[...] = _dot(xb, win_ref[:, off:off + w])
        off += w


def _ffn_inproj(x, w1, w3, w2, lj, g, b, win):
    n = x.shape[0]
    tm = TOKEN_TILE
    tile = lambda w: pl.BlockSpec((tm, w), lambda i: (i, 0))
    widths = (D_MODEL, Z_A, Z_LRU, GROUP_W, GROUP_W, GROUP_W, Z_ML, Z_G)
    return pl.pallas_call(
        _ffn_inproj_kernel,
        grid=(n // tm,),
        in_specs=[tile(D_MODEL), _stacked_spec(w1.shape, lj), _stacked_spec(w3.shape, lj), _stacked_spec(w2.shape, lj),
                  _const_spec(g.shape), _const_spec(b.shape), _stacked_spec(win.shape, lj[:1])],
        out_specs=[tile(w) for w in widths],
        out_shape=[jax.ShapeDtypeStruct((n, w), F32) for w in widths],
        scratch_shapes=[pltpu.VMEM((tm, D_MODEL), F32)],
        compiler_params=pltpu.CompilerParams(dimension_semantics=("arbitrary",),
                                             vmem_limit_bytes=FFN_VMEM_LIMIT),
        name="ffn_inproj",
    )(x, w1, w3, w2, g, b, win)


def _outproj_ffn_kernel(x_ref, ya_ref, yb_ref, yc_ref, yd_ref,
                        on_ref, wout_ref, g1_ref, b1_ref, w1_ref, w3_ref, w2_ref, g2_ref, b2_ref,
                        out_ref, acc_ref):
    y = jnp.zeros(x_ref.shape, F32)
    for gi, y_ref in enumerate((ya_ref, yb_ref, yc_ref, yd_ref)):
        lo = gi * GROUP_W
        yn = _rmsnorm(y_ref[...], on_ref[:, lo:lo + GROUP_W])
        y = y + _dot(yn.astype(BF16), wout_ref[lo:lo + GROUP_W, :])
    x2 = _layernorm(ALPHA * x_ref[...] + y, g1_ref[...], b1_ref[...])
    out_ref[...] = _ffn_ln(x2, w1_ref, w3_ref, w2_ref, acc_ref, g2_ref[...], b2_ref[...])


def _outproj_ffn(x, ys, on, wout, g1, b1, w1, w3, w2, lj, g2, b2):
    n = x.shape[0]
    tm = TOKEN_TILE
    tile = lambda w: pl.BlockSpec((tm, w), lambda i: (i, 0))
    consts = (on, wout, g1, b1, w1, w3, w2, g2, b2)
    const_specs = [_stacked_spec(c.shape, lj) if c.ndim == 4 else _const_spec(c.shape) for c in consts]
    return pl.pallas_call(
        _outproj_ffn_kernel,
        grid=(n // tm,),
        in_specs=[tile(D_MODEL)] + [tile(GROUP_W)] * len(ys) + const_specs,
        out_specs=tile(D_MODEL),
        out_shape=jax.ShapeDtypeStruct((n, D_MODEL), F32),
        scratch_shapes=[pltpu.VMEM((tm, D_MODEL), F32)],
        compiler_params=pltpu.CompilerParams(dimension_semantics=("arbitrary",),
                                             vmem_limit_bytes=FFN_VMEM_LIMIT),
        name="outproj_ffn",
    )(x, *ys, *consts)


def _flash_state_scratch():
    return [pltpu.VMEM((1, ATT_COLS), F32), pltpu.VMEM((1, ATT_COLS), F32), pltpu.VMEM((PAIR_W, ATT_COLS), F32)]


def _flash_reset(state):
    m_ref, l_ref, acc_ref = state
    m_ref[...] = jnp.full(m_ref.shape, NEG_INF, F32)
    l_ref[...] = jnp.zeros_like(l_ref)
    acc_ref[...] = jnp.zeros_like(acc_ref)


def _flash_pipe_scratch():
    return [pltpu.VMEM((2, ATT_T, ATT_COLS), F32), pltpu.VMEM((2, 1, ATT_COLS), F32)]


def _flash_attend(n_steps, qm, key_block, value_block, bias_block, state, pipe, groups):
    m_ref, l_ref, acc_ref = state
    s_ref, mx_ref = pipe
    _flash_reset(state)

    def scores(j, slot):
        bias = bias_block(j)
        s_t = _dot_nt(key_block(j), qm) + jnp.concatenate([bias] * (ATT_COLS // bias.shape[1]), axis=1)
        s_ref[slot] = s_t
        mx_ref[slot] = jnp.max(s_t, axis=0, keepdims=True)

    scores(0, 0)

    def consume(j, with_next):
        cur = lax.rem(j, 2)
        m = m_ref[...]
        m_new = jnp.maximum(m, mx_ref[cur])
        a = jnp.exp2(m - m_new)
        pr = jnp.exp2(s_ref[cur] - m_new)
        m_ref[...] = m_new
        l_ref[...] = a * l_ref[...] + jnp.sum(pr, axis=0, keepdims=True)
        prb = pr.astype(BF16)
        if with_next:
            scores(j + 1, 1 - cur)
        vt_blk = value_block(j)
        for v_lo, c_lo, c_hi in groups:
            acc_ref[:, c_lo:c_hi] = (a[:, c_lo:c_hi] * acc_ref[:, c_lo:c_hi]
                                     + _dot(vt_blk[v_lo:v_lo + PAIR_W, :], prb[:, c_lo:c_hi]))

    def step(j, carry):
        consume(j, True)
        return carry

    lax.fori_loop(0, n_steps - 1, step, 0)
    consume(n_steps - 1, False)


def _flash_result(state):
    _, l_ref, acc_ref = state
    return jnp.transpose(acc_ref[...] * (1.0 / l_ref[...]))


def _merge_pair(o_a, o_b):
    lane = lax.broadcasted_iota(jnp.int32, (1, PAIR_W), 1)
    return jnp.where(lane < HEAD_DIM, o_a, o_b)


def _store_transposed_values(v, vt_ref, j):
    vt_ref[j] = jnp.transpose(v).astype(BF16)


def _mla_kernel(za_ref, qn_ref, kvn_ref, wuq_ref, wukv_ref, cos_ref, sin_ref, o_ref,
                qm_scr, kp_scr, vt_scr, o_scr, bias_scr, m_scr, l_scr, acc_scr, s_scr, mx_scr):
    seq = za_ref.shape[0]
    state = (m_scr, l_scr, acc_scr)
    pipe = (s_scr, mx_scr)
    c_q = (MLA_NOPE + MLA_ROPE) ** -0.5 * LOG2E

    def prep(j, carry):
        rows = pl.ds(pl.multiple_of(j * ATT_T, ATT_T), ATT_T)
        cos, sin = cos_ref[rows, :], sin_ref[rows, :]

        def rope(r):
            return r * cos + pltpu.roll(r, V7X_LANES // 2, axis=1) * sin

        cq = _rmsnorm(za_ref[rows, 0:Q_LORA], qn_ref[...])
        q = _dot(cq.astype(BF16), wuq_ref[...]) * c_q
        for h in range(HEADS):
            lo = h * GROUP_W
            qm_scr[h, rows, 0:V7X_LANES] = q[:, lo:lo + V7X_LANES].astype(BF16)
            qm_scr[h, rows, V7X_LANES:] = rope(q[:, lo + V7X_LANES:lo + GROUP_W]).astype(BF16)
        ckv = _rmsnorm(za_ref[rows, Q_LORA:Q_LORA + KV_LORA], kvn_ref[...])
        kv = _dot(ckv.astype(BF16), wukv_ref[...])
        k_rope = rope(za_ref[rows, Q_LORA + KV_LORA:]).astype(BF16)
        for p in range(2):
            kp_scr[p, rows, 0:V7X_LANES] = kv[:, p * PAIR_W:(p + 1) * PAIR_W].astype(BF16)
            kp_scr[p, rows, V7X_LANES:] = k_rope
        _store_transposed_values(kv[:, GROUP_W:], vt_scr, j)
        return carry

    lax.fori_loop(0, seq // ATT_T, prep, 0)

    key = lax.broadcasted_iota(jnp.int32, (ATT_T, MLA_TQ), 0)
    qry = lax.broadcasted_iota(jnp.int32, (ATT_T, MLA_TQ), 1)
    n_diag = MLA_TQ // ATT_T
    bias_scr[0] = jnp.zeros((ATT_T, MLA_TQ), F32)
    for d in range(n_diag):
        bias_scr[1 + d] = jnp.where(key + d * ATT_T <= qry, 0.0, NEG_INF)
    groups = [(0, 0, ATT_COLS)]

    def q_block(i, carry):
        q0 = pl.multiple_of(i * MLA_TQ, MLA_TQ)
        q_rows = pl.ds(q0, MLA_TQ)
        first_diag = i * n_diag

        def pair(p, c):
            qm = jnp.concatenate([qm_scr[2 * p, q_rows, :], qm_scr[2 * p + 1, q_rows, :]], axis=0)
            v_lo = pl.multiple_of(p * PAIR_W, PAIR_W)
            _flash_attend(
                first_diag + n_diag, qm,
                lambda j: kp_scr[p, pl.ds(pl.multiple_of(j * ATT_T, ATT_T), ATT_T), :],
                lambda j: vt_scr.at[j, pl.ds(v_lo, PAIR_W), :],
                lambda j: bias_scr[jnp.maximum(j - first_diag + 1, 0)],
                state, pipe, groups)
            o_t = _flash_result(state)
            o_scr[p] = _merge_pair(o_t[:MLA_TQ], o_t[MLA_TQ:])
            return c

        lax.fori_loop(0, 2, pair, 0)
        for p in range(2):
            o_ref[q_rows, p * PAIR_W:(p + 1) * PAIR_W] = o_scr[p]
        return carry

    lax.fori_loop(0, seq // MLA_TQ, q_block, 0)


def _mla(za, qn, kvn, wuq, wukv, cos, sin):
    b, s, _ = za.shape
    consts = (qn, kvn, wuq, wukv, cos, sin)
    return pl.pallas_call(
        _mla_kernel,
        grid=(b,),
        in_specs=[pl.BlockSpec((None, s, Z_A), lambda i: (i, 0, 0))] + [_const_spec(c.shape) for c in consts],
        out_specs=pl.BlockSpec((None, s, GROUP_W), lambda i: (i, 0, 0)),
        out_shape=jax.ShapeDtypeStruct((b, s, GROUP_W), F32),
        scratch_shapes=([pltpu.VMEM((HEADS, s, GROUP_W), BF16), pltpu.VMEM((2, s, GROUP_W), BF16),
                         pltpu.VMEM((s // ATT_T, GROUP_W, ATT_T), BF16), pltpu.VMEM((2, MLA_TQ, PAIR_W), F32),
                         pltpu.VMEM((1 + MLA_TQ // ATT_T, ATT_T, MLA_TQ), F32)]
                        + _flash_state_scratch() + _flash_pipe_scratch()),
        compiler_params=pltpu.CompilerParams(dimension_semantics=("arbitrary",),
                                             vmem_limit_bytes=MIXER_VMEM_LIMIT),
        name="mla",
    )(za, *consts)


def _swa_kernel(q_ref, k_ref, v_ref, bias_ref, o_ref, qm_scr, k_scr, vt_scr, m_scr, l_scr, acc_scr,
                s_scr, mx_scr):
    seq = q_ref.shape[0]
    state = (m_scr, l_scr, acc_scr)
    pipe = (s_scr, mx_scr)
    head = _head_of_lane(GROUP_W)
    c_q = HEAD_DIM ** -0.5 * LOG2E
    half = ATT_COLS // 2
    groups = [(0, 0, half), (PAIR_W, half, ATT_COLS)]

    def prep(j, carry):
        rows = pl.ds(pl.multiple_of(j * ATT_T, ATT_T), ATT_T)
        q = (q_ref[rows, :] * c_q).astype(BF16)
        for h in range(HEADS):
            qm_scr[h, rows, :] = jnp.where(head == h, q, jnp.zeros_like(q))
        k_scr[rows, :] = k_ref[rows, :].astype(BF16)
        _store_transposed_values(v_ref[rows, :], vt_scr, j)
        return carry

    lax.fori_loop(0, seq // ATT_T, prep, 0)

    def q_block(i, carry):
        q_rows = pl.ds(pl.multiple_of(i * ATT_T, ATT_T), ATT_T)
        qm = jnp.concatenate([qm_scr[h, q_rows, :] for h in range(HEADS)], axis=0)
        _flash_attend(
            i + 1, qm,
            lambda j: k_scr[pl.ds(pl.multiple_of(j * ATT_T, ATT_T), ATT_T), :],
            lambda j: vt_scr.at[j],
            lambda j: bias_ref[jnp.minimum(i - j, N_BIAS_TILES - 1)],
            state, pipe, groups)
        o_t = _flash_result(state)
        for p in range(2):
            lo = 2 * p * ATT_T
            o_ref[q_rows, p * PAIR_W:(p + 1) * PAIR_W] = _merge_pair(o_t[lo:lo + ATT_T], o_t[lo + ATT_T:lo + 2 * ATT_T])
        return carry

    lax.fori_loop(0, seq // ATT_T, q_block, 0)


def _swa_bias_tiles(seq):
    assert DILATED_PATTERNS[-1][0] >= seq - 1, "far tiles assume the widest window covers the sequence"
    assert all(w <= (N_BIAS_TILES - 2) * ATT_T for w, _ in DILATED_PATTERNS[:-1]), "narrow windows must end before the far tile"
    key = np.arange(ATT_T)[:, None]
    qry = np.arange(ATT_T)[None, :]
    tiles = []
    for dist in range(N_BIAS_TILES):
        d = dist * ATT_T + qry - key
        mult = sum(((d >= 0) & (d % dil == 0) & (d <= window)).astype(np.float32) for window, dil in DILATED_PATTERNS)
        with np.errstate(divide="ignore"):
            tiles.append(np.log2(mult))
    return jnp.asarray(np.stack(tiles), F32)


def _swa(q, k, v, bias):
    b, s, w = q.shape
    spec = pl.BlockSpec((None, s, w), lambda i: (i, 0, 0))
    return pl.pallas_call(
        _swa_kernel,
        grid=(b,),
        in_specs=[spec, spec, spec, _const_spec(bias.shape)],
        out_specs=spec,
        out_shape=jax.ShapeDtypeStruct((b, s, w), F32),
        scratch_shapes=([pltpu.VMEM((HEADS, s, w), BF16), pltpu.VMEM((s, w), BF16),
                         pltpu.VMEM((s // ATT_T, w, ATT_T), BF16)] + _flash_state_scratch()
                        + _flash_pipe_scratch()),
        compiler_params=pltpu.CompilerParams(dimension_semantics=("arbitrary",),
                                             vmem_limit_bytes=MIXER_VMEM_LIMIT),
        name="swa",
    )(q, k, v, bias)


def _softplus(x):
    return jnp.maximum(x, 0.0) + jnp.log1p(jnp.exp(-jnp.abs(x)))


def _gelu_tanh(x):
    c = np.sqrt(2.0 / np.pi).astype(np.float32)
    return 0.5 * x * (1.0 + jnp.tanh(c * (x + 0.044715 * (x * x * x))))


def _rglru_kernel(z_ref, cw_ref, cb_ref, wa_ref, ba_ref, wx_ref, bx_ref, lam_ref, o_ref, a_scr, h_scr):
    seq = z_ref.shape[0]
    h_scr[0:V7X_SUBLANES, :] = jnp.zeros((V7X_SUBLANES, GROUP_W), F32)
    h_scr[V7X_SUBLANES:V7X_SUBLANES + seq, :] = z_ref[:, 0:GROUP_W]
    xc = cb_ref[...]
    for j in range(CONV_W):
        start = V7X_SUBLANES - (CONV_W - 1 - j)
        xc = xc + cw_ref[j:j + 1, :] * h_scr[start:start + seq, :]
    xcb = xc.astype(BF16)
    r = jax.nn.sigmoid(_dot(xcb, wa_ref[...]) + ba_ref[...])
    i = jax.nn.sigmoid(_dot(xcb, wx_ref[...]) + bx_ref[...])
    log_a = -LRU_C * r * _softplus(-lam_ref[...])
    a = jnp.exp(log_a)
    gain2 = -jnp.tanh(log_a) * (a * a + 1.0)
    gain = jnp.where(gain2 > 0.0, gain2 * lax.rsqrt(gain2), 0.0)
    u = gain * (i * xc)
    groups = (seq // V7X_SUBLANES, V7X_SUBLANES, GROUP_W)
    a, u = a.reshape(groups), u.reshape(groups)
    in_group = lax.broadcasted_iota(jnp.int32, groups, 1)
    d = 1
    while d < V7X_SUBLANES:
        keep = in_group >= d
        u = a * jnp.where(keep, pltpu.roll(u, d, axis=1), 0.0) + u
        a = a * jnp.where(keep, pltpu.roll(a, d, axis=1), 1.0)
        d *= 2
    a_scr[...] = a.reshape(seq, GROUP_W)
    h_scr[0:seq, :] = u.reshape(seq, GROUP_W)

    def group(g, h_prev):
        rows = pl.ds(pl.multiple_of(g * V7X_SUBLANES, V7X_SUBLANES), V7X_SUBLANES)
        h = h_scr[rows, :] + a_scr[rows, :] * h_prev
        h_scr[rows, :] = h
        return jnp.broadcast_to(h[V7X_SUBLANES - 1:, :], h.shape)

    lax.fori_loop(0, seq // V7X_SUBLANES, group, jnp.zeros((V7X_SUBLANES, GROUP_W), F32), unroll=8)
    o_ref[...] = h_scr[0:seq, :] * _gelu_tanh(z_ref[:, GROUP_W:])


def _rglru(z, cw, cb, wa, ba, wx, bx, lam):
    b, s, _ = z.shape
    consts = (cw, cb, wa, ba, wx, bx, lam)
    return pl.pallas_call(
        _rglru_kernel,
        grid=(b,),
        in_specs=[pl.BlockSpec((None, s, Z_LRU), lambda i: (i, 0, 0))] + [_const_spec(c.shape) for c in consts],
        out_specs=pl.BlockSpec((None, s, GROUP_W), lambda i: (i, 0, 0)),
        out_shape=jax.ShapeDtypeStruct((b, s, GROUP_W), F32),
        scratch_shapes=[pltpu.VMEM((s, GROUP_W), F32), pltpu.VMEM((s + V7X_SUBLANES, GROUP_W), F32)],
        compiler_params=pltpu.CompilerParams(dimension_semantics=("arbitrary",),
                                             vmem_limit_bytes=MIXER_VMEM_LIMIT),
        name="rglru",
    )(z, *consts)


def _dot_exact(a, b):
    return jnp.dot(a, b, precision=lax.Precision.HIGHEST, preferred_element_type=F32)


def _per_head_rows(rows, height):
    return jnp.concatenate([jnp.broadcast_to(rows[h:h + 1, :], (height, rows.shape[1])) for h in range(HEADS)], axis=0)


def _mlstm_kernel(z_ref, g_ref, gb_ref, o_ref, ct_scr, n_scr, m_scr):
    seq = z_ref.shape[0]
    lc = ML_CHUNK
    sub = lax.broadcasted_iota(jnp.int32, (lc, lc), 0)
    lan = lax.broadcasted_iota(jnp.int32, (lc, lc), 1)
    tri_incl = jnp.where(sub <= lan, 1.0, 0.0)
    visible = sub <= lan
    head = _head_of_lane(GROUP_W)
    same_head = lax.broadcasted_iota(jnp.int32, (GROUP_W, GROUP_W), 0) // HEAD_DIM == head
    n_rows = lax.broadcasted_iota(jnp.int32, (V7X_SUBLANES, GROUP_W), 0) == head
    pad_rows = jnp.zeros((lc - V7X_SUBLANES, lc), F32)
    ct_scr[...] = jnp.zeros_like(ct_scr)
    n_scr[...] = jnp.zeros_like(n_scr)
    m_scr[...] = jnp.zeros_like(m_scr)

    def chunk(ci, carry):
        rows = pl.ds(pl.multiple_of(ci * lc, lc), lc)
        q = z_ref[rows, 0:GROUP_W].astype(BF16)
        k = (z_ref[rows, GROUP_W:2 * GROUP_W] * (HEAD_DIM ** -0.5)).astype(BF16)
        v_t = jnp.transpose(z_ref[rows, 2 * GROUP_W:3 * GROUP_W])
        og = z_ref[rows, 3 * GROUP_W:4 * GROUP_W]
        pre = jnp.transpose(g_ref[rows, :])[0:V7X_SUBLANES, :] + gb_ref[...]
        log_f = jnp.minimum(pre, 0.0) - jnp.log1p(jnp.exp(-jnp.abs(pre)))
        b_row = pltpu.roll(_dot_exact(log_f, tri_incl), HEADS, axis=0)
        key_row = pre - b_row
        key_col = jnp.transpose(jnp.concatenate([key_row, pad_rows], axis=0))
        m_prev = m_scr[...]
        ct_prev = ct_scr[...]
        n_prev = n_scr[...]
        inter_all = b_row + m_prev

        qs = jnp.concatenate([jnp.where(head == h, q, jnp.zeros_like(q)) for h in range(HEADS)], axis=0)
        s_all = _dot_nt(k, qs)
        qc_t = _dot_nt(ct_prev.astype(BF16), q)
        qn = _dot_nt(n_prev.astype(BF16), q)
        outs = []
        for h in range(HEADS):
            b_t = b_row[h:h + 1, :]
            dmat = jnp.where(visible, b_t + key_col[:, h:h + 1], NEG_INF)
            inter = inter_all[h:h + 1, :]
            mt = jnp.maximum(jnp.max(dmat, axis=0, keepdims=True), inter)
            w_t = s_all[:, h * lc:(h + 1) * lc] * jnp.exp(dmat - mt)
            w_inter = jnp.exp(inter - mt)
            lanes = slice(h * HEAD_DIM, (h + 1) * HEAD_DIM)
            num = _dot(v_t[lanes, :].astype(BF16), w_t.astype(BF16)) + w_inter * qc_t[lanes, :]
            den = jnp.sum(w_t, axis=0, keepdims=True) + w_inter * qn[h:h + 1, :]
            outs.append(num * (1.0 / jnp.maximum(jnp.abs(den), jnp.exp(-mt))))
        o_ref[rows, :] = jax.nn.sigmoid(og) * jnp.transpose(jnp.concatenate(outs, axis=0))

        g_all = jnp.broadcast_to(b_row[:, lc - 1:lc], b_row.shape)
        wk = g_all + key_row
        m_new = jnp.maximum(g_all + m_prev, jnp.max(wk, axis=1, keepdims=True))
        decay = jnp.exp(g_all + m_prev - m_new)
        w_row = jnp.exp(wk - m_new)
        vw_t = (v_t * _per_head_rows(w_row, HEAD_DIM)).astype(BF16)
        decay_rows = _per_head_rows(decay, HEAD_DIM)
        ct_scr[...] = (jnp.concatenate([decay_rows, decay_rows], axis=1) * ct_prev
                       + jnp.where(same_head, _dot(vw_t, k), 0.0))
        n_scr[...] = (jnp.concatenate([decay, decay], axis=1) * n_prev
                      + jnp.where(n_rows, _dot(w_row.astype(BF16), k), 0.0))
        m_scr[...] = m_new
        return carry

    lax.fori_loop(0, seq // lc, chunk, 0, unroll=4)


def _mlstm(z, gates, gate_bias):
    b, s, _ = z.shape
    return pl.pallas_call(
        _mlstm_kernel,
        grid=(b,),
        in_specs=[pl.BlockSpec((None, s, Z_ML), lambda i: (i, 0, 0)),
                  pl.BlockSpec((None, s, Z_G), lambda i: (i, 0, 0)),
                  _const_spec(gate_bias.shape)],
        out_specs=pl.BlockSpec((None, s, GROUP_W), lambda i: (i, 0, 0)),
        out_shape=jax.ShapeDtypeStruct((b, s, GROUP_W), F32),
        scratch_shapes=[pltpu.VMEM((GROUP_W, GROUP_W), F32), pltpu.VMEM((V7X_SUBLANES, GROUP_W), F32),
                        pltpu.VMEM((V7X_SUBLANES, ML_CHUNK), F32)],
        compiler_params=pltpu.CompilerParams(dimension_semantics=("arbitrary",),
                                             vmem_limit_bytes=MIXER_VMEM_LIMIT),
        name="mlstm",
    )(z, gates, gate_bias)


def _block_diag(w):
    out = jnp.zeros((GROUP_W, GROUP_W), w.dtype)
    for h in range(HEADS):
        out = out.at[h * HEAD_DIM:(h + 1) * HEAD_DIM, h * HEAD_DIM:(h + 1) * HEAD_DIM].set(w[h])
    return out


def _permute_w_in(w):
    kr = Q_LORA + KV_LORA
    zeros = lambda width: jnp.zeros(w.shape[:2] + (width,), w.dtype)
    gap = zeros(HEAD_DIM - ROPE_HALF)
    body_lo = kr + MLA_ROPE
    body_hi = body_lo + 9 * GROUP_W
    pad = zeros(Z_G - 2 * HEADS)
    return jnp.concatenate([w[..., :kr], w[..., kr:kr + ROPE_HALF], gap, w[..., kr + ROPE_HALF:kr + MLA_ROPE], gap,
                            w[..., body_lo:body_hi], w[..., body_hi:], pad], axis=-1).astype(BF16)


def _mla_up_layout():
    per_q = MLA_NOPE + MLA_ROPE
    q_src, q_dst = [], []
    for h in range(HEADS):
        base = h * GROUP_W
        nope_lo = base + (h % 2) * HEAD_DIM
        q_src += [h * per_q + i for i in range(per_q)]
        q_dst += [nope_lo + i for i in range(MLA_NOPE)]
        q_dst += [base + V7X_LANES + i for i in range(ROPE_HALF)]
        q_dst += [base + V7X_LANES + HEAD_DIM + i for i in range(ROPE_HALF)]
    per_kv = MLA_NOPE + HEAD_DIM
    kn = [h * per_kv + i for h in range(HEADS) for i in range(MLA_NOPE)]
    vv = [h * per_kv + MLA_NOPE + i for h in range(HEADS) for i in range(HEAD_DIM)]
    return np.array(q_src), np.array(q_dst), np.array(kn + vv)


def _rope_tables(seq):
    freqs = ROPE_THETA ** (-jnp.arange(ROPE_HALF, dtype=F32) / ROPE_HALF)
    ang = jnp.arange(seq, dtype=F32)[:, None] * freqs[None, :]
    cos = jnp.tile(jnp.cos(ang), (1, 2 * HEADS))
    sin = jnp.tile(jnp.sin(ang), (1, HEADS))
    return cos, jnp.concatenate([-sin, sin], axis=1)


def kernel(x, ln_g, ln_b, ffn_w1, ffn_w3, ffn_w2, w_in, mla_q_norm, mla_kv_norm, mla_w_uq, mla_w_ukv,
           lru_conv_w, lru_conv_b, lru_w_a, lru_b_a, lru_w_x, lru_b_x, lru_lambda, ml_b_i, ml_b_f,
           out_norm, w_out):
    bsz, seq, d = x.shape
    n = bsz * seq
    row = lambda t: t.reshape(1, -1)
    q_src, q_dst, kv_perm = _mla_up_layout()
    cos, sin = _rope_tables(seq)
    swa_bias = _swa_bias_tiles(seq)
    xt = x.reshape(n, d)
    ffn_w = (ffn_w1.astype(BF16), ffn_w3.astype(BF16), ffn_w2.astype(BF16))
    w_in_p = _permute_w_in(w_in)
    for l in range(DEPTH):
        x1, za, zlru, sq, sk, sv, zml, zg = _ffn_inproj(
            xt, *ffn_w, (l, 0), row(ln_g[l, 0]), row(ln_b[l, 0]), w_in_p)
        seqv = lambda t: t.reshape(bsz, seq, t.shape[-1])
        wuq = jnp.zeros((Q_LORA, HEADS * GROUP_W), F32).at[:, q_dst].set(mla_w_uq[l][:, q_src]).astype(BF16)
        y_a = _mla(seqv(za), row(mla_q_norm[l]), row(mla_kv_norm[l]), wuq,
                   mla_w_ukv[l][:, kv_perm].astype(BF16), cos, sin)
        y_b = _rglru(seqv(zlru), lru_conv_w[l], row(lru_conv_b[l]),
                     _block_diag(lru_w_a[l]).astype(BF16), row(lru_b_a[l]),
                     _block_diag(lru_w_x[l]).astype(BF16), row(lru_b_x[l]), row(lru_lambda[l]))
        y_c = _swa(seqv(sq), seqv(sk), seqv(sv), swa_bias)
        gate_bias = jnp.broadcast_to(jnp.concatenate([ml_b_i[l], ml_b_f[l]])[:, None], (2 * HEADS, ML_CHUNK))
        y_d = _mlstm(seqv(zml), seqv(zg), gate_bias)
        ys = [t.reshape(n, GROUP_W) for t in (y_a, y_b, y_c, y_d)]
        xt = _outproj_ffn(x1, ys, row(out_norm[l]), w_out[l].astype(BF16), row(ln_g[l, 1]), row(ln_b[l, 1]),
                          *ffn_w, (l, 1), row(ln_g[l, 2]), row(ln_b[l, 2]))
    return xt.reshape(bsz, seq, d)
```

```python
import numpy as np
import jax
import jax.numpy as jnp
from jax import lax
from jax.experimental import pallas as pl
from jax.experimental.pallas import tpu as pltpu

F32 = jnp.float32
BF16 = jnp.bfloat16

D_MODEL = 1024
DEPTH = 2
GROUP_W = 256
HEAD_DIM = 64
HEADS = 4
D_FF = 2816
ALPHA = (2.0 * DEPTH) ** 0.25
Q_LORA = 256
KV_LORA = 128
MLA_NOPE = 64
MLA_ROPE = 32
ROPE_HALF = MLA_ROPE // 2
ROPE_THETA = 10000.0
CONV_W = 4
LRU_C = 8.0
DILATED_PATTERNS = ((128, 1), (512, 4), (2048, 16))

V7X_LANES = 128
V7X_SUBLANES = 8
V7X_VMEM_BYTES = 64 * 1024 * 1024

FF_CHUNK = 256
N_FF_CHUNKS = D_FF // FF_CHUNK
TOKEN_TILE = 512
ML_CHUNK = 256
ATT_T = 256
ATT_COLS = 1024
MLA_TQ = 512
PAIR_W = 2 * HEAD_DIM
N_BIAS_TILES = 4

Z_A = 512
Z_LRU = 512
Z_ML = 1024
Z_G = 128

NEG_INF = float("-inf")
LOG2E = float(np.log2(np.e))


MIB = 1024 * 1024
FFN_VMEM_LIMIT = 56 * MIB
MIXER_VMEM_LIMIT = 48 * MIB
assert max(FFN_VMEM_LIMIT, MIXER_VMEM_LIMIT) < V7X_VMEM_BYTES


def _const_spec(shape):
    nd = len(shape)
    return pl.BlockSpec(shape, lambda *_: (0,) * nd, pipeline_mode=pl.Buffered(1))


def _stacked_spec(shape, lead):
    tail = shape[len(lead):]
    return pl.BlockSpec((None,) * len(lead) + tuple(tail), lambda *_: tuple(lead) + (0,) * len(tail),
                        pipeline_mode=pl.Buffered(1))


def _layernorm(v, g, b):
    mu = jnp.mean(v, axis=-1, keepdims=True)
    d = v - mu
    var = jnp.mean(d * d, axis=-1, keepdims=True)
    return d * lax.rsqrt(var + 1e-5) * g + b


def _rmsnorm(v, g, eps=1e-6):
    return v * lax.rsqrt(jnp.mean(v * v, axis=-1, keepdims=True) + eps) * g


def _dot(a, b):
    return jnp.dot(a, b, preferred_element_type=F32)


def _dot_nt(a, b):
    return lax.dot_general(a, b, (((1,), (1,)), ((), ())), preferred_element_type=F32)


def _head_of_lane(width, group=HEAD_DIM):
    return lax.broadcasted_iota(jnp.int32, (1, width), 1) // group


def _ffn_ln(x, w1_ref, w3_ref, w2_ref, acc_ref, g, b):
    xb = x.astype(BF16)
    for c in range(N_FF_CHUNKS):
        cols = slice(c * FF_CHUNK, (c + 1) * FF_CHUNK)
        h1 = _dot(xb, w1_ref[:, cols])
        h3 = _dot(xb, w3_ref[:, cols])
        h = (h1 * jax.nn.sigmoid(h1)) * h3
        part = _dot(h.astype(BF16), w2_ref[cols, :])
        if c == 0:
            acc_ref[...] = part
        else:
            acc_ref[...] += part
    return _layernorm(ALPHA * x + 0.5 * acc_ref[...], g, b)


def _ffn_inproj_kernel(x_ref, w1_ref, w3_ref, w2_ref, g_ref, b_ref, win_ref,
                       x1_ref, za_ref, zlru_ref, sq_ref, sk_ref, sv_ref, zml_ref, zg_ref, acc_ref):
    x1 = _ffn_ln(x_ref[...], w1_ref, w3_ref, w2_ref, acc_ref, g_ref[...], b_ref[...])
    x1_ref[...] = x1
    xb = x1.astype(BF16)
    off = 0
    for ref in (za_ref, zlru_ref, sq_ref, sk_ref, sv_ref, zml_ref, zg_ref):
        w = ref.shape[-1]
        ref[...] = _dot(xb, win_ref[:, off:off + w])
        off += w


def _ffn_inproj(x, w1, w3, w2, lj, g, b, win):
    n = x.shape[0]
    tm = TOKEN_TILE
    tile = lambda w: pl.BlockSpec((tm, w), lambda i: (i, 0))
    widths = (D_MODEL, Z_A, Z_LRU, GROUP_W, GROUP_W, GROUP_W, Z_ML, Z_G)
    return pl.pallas_call(
        _ffn_inproj_kernel,
        grid=(n // tm,),
        in_specs=[tile(D_MODEL), _stacked_spec(w1.shape, lj), _stacked_spec(w3.shape, lj), _stacked_spec(w2.shape, lj),
                  _const_spec(g.shape), _const_spec(b.shape), _stacked_spec(win.shape, lj[:1])],
        out_specs=[tile(w) for w in widths],
        out_shape=[jax.ShapeDtypeStruct((n, w), F32) for w in widths],
        scratch_shapes=[pltpu.VMEM((tm, D_MODEL), F32)],
        compiler_params=pltpu.CompilerParams(dimension_semantics=("arbitrary",),
                                             vmem_limit_bytes=FFN_VMEM_LIMIT),
        name="ffn_inproj",
    )(x, w1, w3, w2, g, b, win)


def _outproj_ffn_kernel(x_ref, ya_ref, yb_ref, yc_ref, yd_ref,
                        on_ref, wout_ref, g1_ref, b1_ref, w1_ref, w3_ref, w2_ref, g2_ref, b2_ref,
                        out_ref, acc_ref):
    y = jnp.zeros(x_ref.shape, F32)
    for gi, y_ref in enumerate((ya_ref, yb_ref, yc_ref, yd_ref)):
        lo = gi * GROUP_W
        yn = _rmsnorm(y_ref[...], on_ref[:, lo:lo + GROUP_W])
        y = y + _dot(yn.astype(BF16), wout_ref[lo:lo + GROUP_W, :])
    x2 = _layernorm(ALPHA * x_ref[...] + y, g1_ref[...], b1_ref[...])
    out_ref[...] = _ffn_ln(x2, w1_ref, w3_ref, w2_ref, acc_ref, g2_ref[...], b2_ref[...])


def _outproj_ffn(x, ys, on, wout, g1, b1, w1, w3, w2, lj, g2, b2):
    n = x.shape[0]
    tm = TOKEN_TILE
    tile = lambda w: pl.BlockSpec((tm, w), lambda i: (i, 0))
    consts = (on, wout, g1, b1, w1, w3, w2, g2, b2)
    const_specs = [_stacked_spec(c.shape, lj) if c.ndim == 4 else _const_spec(c.shape) for c in consts]
    return pl.pallas_call(
        _outproj_ffn_kernel,
        grid=(n // tm,),
        in_specs=[tile(D_MODEL)] + [tile(GROUP_W)] * len(ys) + const_specs,
        out_specs=tile(D_MODEL),
        out_shape=jax.ShapeDtypeStruct((n, D_MODEL), F32),
        scratch_shapes=[pltpu.VMEM((tm, D_MODEL), F32)],
        compiler_params=pltpu.CompilerParams(dimension_semantics=("arbitrary",),
                                             vmem_limit_bytes=FFN_VMEM_LIMIT),
        name="outproj_ffn",
    )(x, *ys, *consts)


def _flash_state_scratch():
    return [pltpu.VMEM((1, ATT_COLS), F32), pltpu.VMEM((1, ATT_COLS), F32), pltpu.VMEM((PAIR_W, ATT_COLS), F32)]


def _flash_reset(state):
    m_ref, l_ref, acc_ref = state
    m_ref[...] = jnp.full(m_ref.shape, NEG_INF, F32)
    l_ref[...] = jnp.zeros_like(l_ref)
    acc_ref[...] = jnp.zeros_like(acc_ref)


def _flash_pipe_scratch():
    return [pltpu.VMEM((2, ATT_T, ATT_COLS), F32), pltpu.VMEM((2, 1, ATT_COLS), F32)]


def _flash_attend(n_steps, qm, key_block, value_block, bias_block, state, pipe, groups):
    m_ref, l_ref, acc_ref = state
    s_ref, mx_ref = pipe
    _flash_reset(state)

    def scores(j, slot):
        bias = bias_block(j)
        s_t = _dot_nt(key_block(j), qm) + jnp.concatenate([bias] * (ATT_COLS // bias.shape[1]), axis=1)
        s_ref[slot] = s_t
        mx_ref[slot] = jnp.max(s_t, axis=0, keepdims=True)

    scores(0, 0)

    def consume(j, with_next):
        cur = lax.rem(j, 2)
        m = m_ref[...]
        m_new = jnp.maximum(m, mx_ref[cur])
        a = jnp.exp2(m - m_new)
        pr = jnp.exp2(s_ref[cur] - m_new)
        m_ref[...] = m_new
        l_ref[...] = a * l_ref[...] + jnp.sum(pr, axis=0, keepdims=True)
        prb = pr.astype(BF16)
        if with_next:
            scores(j + 1, 1 - cur)
        vt_blk = value_block(j)
        for v_lo, c_lo, c_hi in groups:
            acc_ref[:, c_lo:c_hi] = (a[:, c_lo:c_hi] * acc_ref[:, c_lo:c_hi]
                                     + _dot(vt_blk[v_lo:v_lo + PAIR_W, :], prb[:, c_lo:c_hi]))

    def step(j, carry):
        consume(j, True)
        return carry

    lax.fori_loop(0, n_steps - 1, step, 0)
    consume(n_steps - 1, False)


def _flash_result(state):
    _, l_ref, acc_ref = state
    return jnp.transpose(acc_ref[...] * (1.0 / l_ref[...]))


def _merge_pair(o_a, o_b):
    lane = lax.broadcasted_iota(jnp.int32, (1, PAIR_W), 1)
    return jnp.where(lane < HEAD_DIM, o_a, o_b)


def _store_transposed_values(v, vt_ref, j):
    vt_ref[j] = jnp.transpose(v).astype(BF16)


def _mla_kernel(za_ref, qn_ref, kvn_ref, wuq_ref, wukv_ref, cos_ref, sin_ref, o_ref,
                qm_scr, kp_scr, vt_scr, o_scr, bias_scr, m_scr, l_scr, acc_scr, s_scr, mx_scr):
    seq = za_ref.shape[0]
    state = (m_scr, l_scr, acc_scr)
    pipe = (s_scr, mx_scr)
    c_q = (MLA_NOPE + MLA_ROPE) ** -0.5 * LOG2E

    def prep(j, carry):
        rows = pl.ds(pl.multiple_of(j * ATT_T, ATT_T), ATT_T)
        cos, sin = cos_ref[rows, :], sin_ref[rows, :]

        def rope(r):
            return r * cos + pltpu.roll(r, V7X_LANES // 2, axis=1) * sin

        cq = _rmsnorm(za_ref[rows, 0:Q_LORA], qn_ref[...])
        q = _dot(cq.astype(BF16), wuq_ref[...]) * c_q
        for h in range(HEADS):
            lo = h * GROUP_W
            qm_scr[h, rows, 0:V7X_LANES] = q[:, lo:lo + V7X_LANES].astype(BF16)
            qm_scr[h, rows, V7X_LANES:] = rope(q[:, lo + V7X_LANES:lo + GROUP_W]).astype(BF16)
        ckv = _rmsnorm(za_ref[rows, Q_LORA:Q_LORA + KV_LORA], kvn_ref[...])
        kv = _dot(ckv.astype(BF16), wukv_ref[...])
        k_rope = rope(za_ref[rows, Q_LORA + KV_LORA:]).astype(BF16)
        for p in range(2):
            kp_scr[p, rows, 0:V7X_LANES] = kv[:, p * PAIR_W:(p + 1) * PAIR_W].astype(BF16)
            kp_scr[p, rows, V7X_LANES:] = k_rope
        _store_transposed_values(kv[:, GROUP_W:], vt_scr, j)
        return carry

    lax.fori_loop(0, seq // ATT_T, prep, 0)

    key = lax.broadcasted_iota(jnp.int32, (ATT_T, MLA_TQ), 0)
    qry = lax.broadcasted_iota(jnp.int32, (ATT_T, MLA_TQ), 1)
    n_diag = MLA_TQ // ATT_T
    bias_scr[0] = jnp.zeros((ATT_T, MLA_TQ), F32)
    for d in range(n_diag):
        bias_scr[1 + d] = jnp.where(key + d * ATT_T <= qry, 0.0, NEG_INF)
    groups = [(0, 0, ATT_COLS)]

    def q_block(i, carry):
        q0 = pl.multiple_of(i * MLA_TQ, MLA_TQ)
        q_rows = pl.ds(q0, MLA_TQ)
        first_diag = i * n_diag

        def pair(p, c):
            qm = jnp.concatenate([qm_scr[2 * p, q_rows, :], qm_scr[2 * p + 1, q_rows, :]], axis=0)
            v_lo = pl.multiple_of(p * PAIR_W, PAIR_W)
            _flash_attend(
                first_diag + n_diag, qm,
                lambda j: kp_scr[p, pl.ds(pl.multiple_of(j * ATT_T, ATT_T), ATT_T), :],
                lambda j: vt_scr.at[j, pl.ds(v_lo, PAIR_W), :],
                lambda j: bias_scr[jnp.maximum(j - first_diag + 1, 0)],
                state, pipe, groups)
            o_t = _flash_result(state)
            o_scr[p] = _merge_pair(o_t[:MLA_TQ], o_t[MLA_TQ:])
            return c

        lax.fori_loop(0, 2, pair, 0)
        for p in range(2):
            o_ref[q_rows, p * PAIR_W:(p + 1) * PAIR_W] = o_scr[p]
        return carry

    lax.fori_loop(0, seq // MLA_TQ, q_block, 0)


def _mla(za, qn, kvn, wuq, wukv, cos, sin):
    b, s, _ = za.shape
    consts = (qn, kvn, wuq, wukv, cos, sin)
    return pl.pallas_call(
        _mla_kernel,
        grid=(b,),
        in_specs=[pl.BlockSpec((None, s, Z_A), lambda i: (i, 0, 0))] + [_const_spec(c.shape) for c in consts],
        out_specs=pl.BlockSpec((None, s, GROUP_W), lambda i: (i, 0, 0)),
        out_shape=jax.ShapeDtypeStruct((b, s, GROUP_W), F32),
        scratch_shapes=([pltpu.VMEM((HEADS, s, GROUP_W), BF16), pltpu.VMEM((2, s, GROUP_W), BF16),
                         pltpu.VMEM((s // ATT_T, GROUP_W, ATT_T), BF16), pltpu.VMEM((2, MLA_TQ, PAIR_W), F32),
                         pltpu.VMEM((1 + MLA_TQ // ATT_T, ATT_T, MLA_TQ), F32)]
                        + _flash_state_scratch() + _flash_pipe_scratch()),
        compiler_params=pltpu.CompilerParams(dimension_semantics=("arbitrary",),
                                             vmem_limit_bytes=MIXER_VMEM_LIMIT),
        name="mla",
    )(za, *consts)


def _swa_kernel(q_ref, k_ref, v_ref, bias_ref, o_ref, qm_scr, k_scr, vt_scr, m_scr, l_scr, acc_scr,
                s_scr, mx_scr):
    seq = q_ref.shape[0]
    state = (m_scr, l_scr, acc_scr)
    pipe = (s_scr, mx_scr)
    head = _head_of_lane(GROUP_W)
    c_q = HEAD_DIM ** -0.5 * LOG2E
    half = ATT_COLS // 2
    groups = [(0, 0, half), (PAIR_W, half, ATT_COLS)]

    def prep(j, carry):
        rows = pl.ds(pl.multiple_of(j * ATT_T, ATT_T), ATT_T)
        q = (q_ref[rows, :] * c_q).astype(BF16)
        for h in range(HEADS):
            qm_scr[h, rows, :] = jnp.where(head == h, q, jnp.zeros_like(q))
        k_scr[rows, :] = k_ref[rows, :].astype(BF16)
        _store_transposed_values(v_ref[rows, :], vt_scr, j)
        return carry

    lax.fori_loop(0, seq // ATT_T, prep, 0)

    def q_block(i, carry):
        q_rows = pl.ds(pl.multiple_of(i * ATT_T, ATT_T), ATT_T)
        qm = jnp.concatenate([qm_scr[h, q_rows, :] for h in range(HEADS)], axis=0)
        _flash_attend(
            i + 1, qm,
            lambda j: k_scr[pl.ds(pl.multiple_of(j * ATT_T, ATT_T), ATT_T), :],
            lambda j: vt_scr.at[j],
            lambda j: bias_ref[jnp.minimum(i - j, N_BIAS_TILES - 1)],
            state, pipe, groups)
        o_t = _flash_result(state)
        for p in range(2):
            lo = 2 * p * ATT_T
            o_ref[q_rows, p * PAIR_W:(p + 1) * PAIR_W] = _merge_pair(o_t[lo:lo + ATT_T], o_t[lo + ATT_T:lo + 2 * ATT_T])
        return carry

    lax.fori_loop(0, seq // ATT_T, q_block, 0)


def _swa_bias_tiles(seq):
    assert DILATED_PATTERNS[-1][0] >= seq - 1, "far tiles assume the widest window covers the sequence"
    assert all(w <= (N_BIAS_TILES - 2) * ATT_T for w, _ in DILATED_PATTERNS[:-1]), "narrow windows must end before the far tile"
    key = np.arange(ATT_T)[:, None]
    qry = np.arange(ATT_T)[None, :]
    tiles = []
    for dist in range(N_BIAS_TILES):
        d = dist * ATT_T + qry - key
        mult = sum(((d >= 0) & (d % dil == 0) & (d <= window)).astype(np.float32) for window, dil in DILATED_PATTERNS)
        with np.errstate(divide="ignore"):
            tiles.append(np.log2(mult))
    return jnp.asarray(np.stack(tiles), F32)


def _swa(q, k, v, bias):
    b, s, w = q.shape
    spec = pl.BlockSpec((None, s, w), lambda i: (i, 0, 0))
    return pl.pallas_call(
        _swa_kernel,
        grid=(b,),
        in_specs=[spec, spec, spec, _const_spec(bias.shape)],
        out_specs=spec,
        out_shape=jax.ShapeDtypeStruct((b, s, w), F32),
        scratch_shapes=([pltpu.VMEM((HEADS, s, w), BF16), pltpu.VMEM((s, w), BF16),
                         pltpu.VMEM((s // ATT_T, w, ATT_T), BF16)] + _flash_state_scratch()
                        + _flash_pipe_scratch()),
        compiler_params=pltpu.CompilerParams(dimension_semantics=("arbitrary",),
                                             vmem_limit_bytes=MIXER_VMEM_LIMIT),
        name="swa",
    )(q, k, v, bias)


def _softplus(x):
    return jnp.maximum(x, 0.0) + jnp.log1p(jnp.exp(-jnp.abs(x)))


def _gelu_tanh(x):
    c = np.sqrt(2.0 / np.pi).astype(np.float32)
    return 0.5 * x * (1.0 + jnp.tanh(c * (x + 0.044715 * (x * x * x))))


def _rglru_kernel(z_ref, cw_ref, cb_ref, wa_ref, ba_ref, wx_ref, bx_ref, lam_ref, o_ref, a_scr, h_scr):
    seq = z_ref.shape[0]
    h_scr[0:V7X_SUBLANES, :] = jnp.zeros((V7X_SUBLANES, GROUP_W), F32)
    h_scr[V7X_SUBLANES:V7X_SUBLANES + seq, :] = z_ref[:, 0:GROUP_W]
    xc = cb_ref[...]
    for j in range(CONV_W):
        start = V7X_SUBLANES - (CONV_W - 1 - j)
        xc = xc + cw_ref[j:j + 1, :] * h_scr[start:start + seq, :]
    xcb = xc.astype(BF16)
    r = jax.nn.sigmoid(_dot(xcb, wa_ref[...]) + ba_ref[...])
    i = jax.nn.sigmoid(_dot(xcb, wx_ref[...]) + bx_ref[...])
    log_a = -LRU_C * r * _softplus(-lam_ref[...])
    a = jnp.exp(log_a)
    gain2 = -jnp.tanh(log_a) * (a * a + 1.0)
    gain = jnp.where(gain2 > 0.0, gain2 * lax.rsqrt(gain2), 0.0)
    u = gain * (i * xc)
    groups = (seq // V7X_SUBLANES, V7X_SUBLANES, GROUP_W)
    a, u = a.reshape(groups), u.reshape(groups)
    in_group = lax.broadcasted_iota(jnp.int32, groups, 1)
    d = 1
    while d < V7X_SUBLANES:
        keep = in_group >= d
        u = a * jnp.where(keep, pltpu.roll(u, d, axis=1), 0.0) + u
        a = a * jnp.where(keep, pltpu.roll(a, d, axis=1), 1.0)
        d *= 2
    a_scr[...] = a.reshape(seq, GROUP_W)
    h_scr[0:seq, :] = u.reshape(seq, GROUP_W)

    def group(g, h_prev):
        rows = pl.ds(pl.multiple_of(g * V7X_SUBLANES, V7X_SUBLANES), V7X_SUBLANES)
        h = h_scr[rows, :] + a_scr[rows, :] * h_prev
        h_scr[rows, :] = h
        return jnp.broadcast_to(h[V7X_SUBLANES - 1:, :], h.shape)

    lax.fori_loop(0, seq // V7X_SUBLANES, group, jnp.zeros((V7X_SUBLANES, GROUP_W), F32), unroll=8)
    o_ref[...] = h_scr[0:seq, :] * _gelu_tanh(z_ref[:, GROUP_W:])


def _rglru(z, cw, cb, wa, ba, wx, bx, lam):
    b, s, _ = z.shape
    consts = (cw, cb, wa, ba, wx, bx, lam)
    return pl.pallas_call(
        _rglru_kernel,
        grid=(b,),
        in_specs=[pl.BlockSpec((None, s, Z_LRU), lambda i: (i, 0, 0))] + [_const_spec(c.shape) for c in consts],
        out_specs=pl.BlockSpec((None, s, GROUP_W), lambda i: (i, 0, 0)),
        out_shape=jax.ShapeDtypeStruct((b, s, GROUP_W), F32),
        scratch_shapes=[pltpu.VMEM((s, GROUP_W), F32), pltpu.VMEM((s + V7X_SUBLANES, GROUP_W), F32)],
        compiler_params=pltpu.CompilerParams(dimension_semantics=("arbitrary",),
                                             vmem_limit_bytes=MIXER_VMEM_LIMIT),
        name="rglru",
    )(z, *consts)


def _dot_exact(a, b):
    return jnp.dot(a, b, precision=lax.Precision.HIGHEST, preferred_element_type=F32)


def _per_head_rows(rows, height):
    return jnp.concatenate([jnp.broadcast_to(rows[h:h + 1, :], (height, rows.shape[1])) for h in range(HEADS)], axis=0)


def _lane_constant_to_width(x, width):
    w = x.shape[1]
    return x[:, :width] if w >= width else jnp.concatenate([x] * (width // w), axis=1)


def _mlstm_kernel(z_ref, g_ref, gb_ref, o_ref, ct_scr, n_scr, m_scr):
    seq = z_ref.shape[0]
    lc = ML_CHUNK
    sub = lax.broadcasted_iota(jnp.int32, (lc, lc), 0)
    lan = lax.broadcasted_iota(jnp.int32, (lc, lc), 1)
    tri_incl = jnp.where(sub <= lan, 1.0, 0.0)
    visible = sub <= lan
    head = _head_of_lane(GROUP_W)
    same_head = lax.broadcasted_iota(jnp.int32, (GROUP_W, GROUP_W), 0) // HEAD_DIM == head
    n_rows = lax.broadcasted_iota(jnp.int32, (V7X_SUBLANES, GROUP_W), 0) == head
    pad_rows = jnp.zeros((lc - V7X_SUBLANES, lc), F32)
    ct_scr[...] = jnp.zeros_like(ct_scr)
    n_scr[...] = jnp.zeros_like(n_scr)
    m_scr[...] = jnp.zeros_like(m_scr)

    def chunk(ci, carry):
        rows = pl.ds(pl.multiple_of(ci * lc, lc), lc)
        q = z_ref[rows, 0:GROUP_W].astype(BF16)
        k = (z_ref[rows, GROUP_W:2 * GROUP_W] * (HEAD_DIM ** -0.5)).astype(BF16)
        v_t = jnp.transpose(z_ref[rows, 2 * GROUP_W:3 * GROUP_W])
        og = z_ref[rows, 3 * GROUP_W:4 * GROUP_W]
        pre = jnp.transpose(g_ref[rows, :])[0:V7X_SUBLANES, :] + gb_ref[...]
        log_f = jnp.minimum(pre, 0.0) - jnp.log1p(jnp.exp(-jnp.abs(pre)))
        b_row = pltpu.roll(_dot_exact(log_f, tri_incl), HEADS, axis=0)
        key_row = pre - b_row
        key_col = jnp.transpose(jnp.concatenate([key_row, pad_rows], axis=0))
        m_prev = m_scr[...]
        ct_prev = ct_scr[...]
        n_prev = n_scr[...]
        inter_all = b_row + m_prev

        qs = jnp.concatenate([jnp.where(head == h, q, jnp.zeros_like(q)) for h in range(HEADS)], axis=0)
        s_all = _dot_nt(k, qs)
        qc_t = _dot_nt(ct_prev.astype(BF16), q)
        qn = _dot_nt(n_prev.astype(BF16), q)
        outs = []
        for h in range(HEADS):
            b_t = b_row[h:h + 1, :]
            dmat = jnp.where(visible, b_t + key_col[:, h:h + 1], NEG_INF)
            inter = inter_all[h:h + 1, :]
            mt = jnp.maximum(jnp.max(dmat, axis=0, keepdims=True), inter)
            w_t = s_all[:, h * lc:(h + 1) * lc] * jnp.exp(dmat - mt)
            w_inter = jnp.exp(inter - mt)
            lanes = slice(h * HEAD_DIM, (h + 1) * HEAD_DIM)
            num = _dot(v_t[lanes, :].astype(BF16), w_t.astype(BF16)) + w_inter * qc_t[lanes, :]
            den = jnp.sum(w_t, axis=0, keepdims=True) + w_inter * qn[h:h + 1, :]
            outs.append(num * (1.0 / jnp.maximum(jnp.abs(den), jnp.exp(-mt))))
        o_ref[rows, :] = jax.nn.sigmoid(og) * jnp.transpose(jnp.concatenate(outs, axis=0))

        g_all = jnp.broadcast_to(b_row[:, lc - 1:lc], b_row.shape)
        wk = g_all + key_row
        m_new = jnp.maximum(g_all + m_prev, jnp.max(wk, axis=1, keepdims=True))
        decay = jnp.exp(g_all + m_prev - m_new)
        w_row = jnp.exp(wk - m_new)
        vw_t = (v_t * _per_head_rows(w_row, HEAD_DIM)).astype(BF16)
        decay_w = _lane_constant_to_width(decay, GROUP_W)
        ct_scr[...] = _per_head_rows(decay_w, HEAD_DIM) * ct_prev + jnp.where(same_head, _dot(vw_t, k), 0.0)
        n_scr[...] = decay_w * n_prev + jnp.where(n_rows, _dot(w_row.astype(BF16), k), 0.0)
        m_scr[...] = m_new
        return carry

    lax.fori_loop(0, seq // lc, chunk, 0, unroll=4)


def _mlstm(z, gates, gate_bias):
    b, s, _ = z.shape
    return pl.pallas_call(
        _mlstm_kernel,
        grid=(b,),
        in_specs=[pl.BlockSpec((None, s, Z_ML), lambda i: (i, 0, 0)),
                  pl.BlockSpec((None, s, Z_G), lambda i: (i, 0, 0)),
                  _const_spec(gate_bias.shape)],
        out_specs=pl.BlockSpec((None, s, GROUP_W), lambda i: (i, 0, 0)),
        out_shape=jax.ShapeDtypeStruct((b, s, GROUP_W), F32),
        scratch_shapes=[pltpu.VMEM((GROUP_W, GROUP_W), F32), pltpu.VMEM((V7X_SUBLANES, GROUP_W), F32),
                        pltpu.VMEM((V7X_SUBLANES, ML_CHUNK), F32)],
        compiler_params=pltpu.CompilerParams(dimension_semantics=("arbitrary",),
                                             vmem_limit_bytes=MIXER_VMEM_LIMIT),
        name="mlstm",
    )(z, gates, gate_bias)


def _block_diag(w):
    out = jnp.zeros((GROUP_W, GROUP_W), w.dtype)
    for h in range(HEADS):
        out = out.at[h * HEAD_DIM:(h + 1) * HEAD_DIM, h * HEAD_DIM:(h + 1) * HEAD_DIM].set(w[h])
    return out


def _permute_w_in(w):
    kr = Q_LORA + KV_LORA
    zeros = lambda width: jnp.zeros(w.shape[:2] + (width,), w.dtype)
    gap = zeros(HEAD_DIM - ROPE_HALF)
    body_lo = kr + MLA_ROPE
    body_hi = body_lo + 9 * GROUP_W
    pad = zeros(Z_G - 2 * HEADS)
    return jnp.concatenate([w[..., :kr], w[..., kr:kr + ROPE_HALF], gap, w[..., kr + ROPE_HALF:kr + MLA_ROPE], gap,
                            w[..., body_lo:body_hi], w[..., body_hi:], pad], axis=-1).astype(BF16)


def _mla_up_layout():
    per_q = MLA_NOPE + MLA_ROPE
    q_src, q_dst = [], []
    for h in range(HEADS):
        base = h * GROUP_W
        nope_lo = base + (h % 2) * HEAD_DIM
        q_src += [h * per_q + i for i in range(per_q)]
        q_dst += [nope_lo + i for i in range(MLA_NOPE)]
        q_dst += [base + V7X_LANES + i for i in range(ROPE_HALF)]
        q_dst += [base + V7X_LANES + HEAD_DIM + i for i in range(ROPE_HALF)]
    per_kv = MLA_NOPE + HEAD_DIM
    kn = [h * per_kv + i for h in range(HEADS) for i in range(MLA_NOPE)]
    vv = [h * per_kv + MLA_NOPE + i for h in range(HEADS) for i in range(HEAD_DIM)]
    return np.array(q_src), np.array(q_dst), np.array(kn + vv)


def _rope_tables(seq):
    freqs = ROPE_THETA ** (-jnp.arange(ROPE_HALF, dtype=F32) / ROPE_HALF)
    ang = jnp.arange(seq, dtype=F32)[:, None] * freqs[None, :]
    cos = jnp.tile(jnp.cos(ang), (1, 2 * HEADS))
    sin = jnp.tile(jnp.sin(ang), (1, HEADS))
    return cos, jnp.concatenate([-sin, sin], axis=1)


def kernel(x, ln_g, ln_b, ffn_w1, ffn_w3, ffn_w2, w_in, mla_q_norm, mla_kv_norm, mla_w_uq, mla_w_ukv,
           lru_conv_w, lru_conv_b, lru_w_a, lru_b_a, lru_w_x, lru_b_x, lru_lambda, ml_b_i, ml_b_f,
           out_norm, w_out):
    bsz, seq, d = x.shape
    n = bsz * seq
    row = lambda t: t.reshape(1, -1)
    q_src, q_dst, kv_perm = _mla_up_layout()
    cos, sin = _rope_tables(seq)
    swa_bias = _swa_bias_tiles(seq)
    xt = x.reshape(n, d)
    ffn_w = (ffn_w1.astype(BF16), ffn_w3.astype(BF16), ffn_w2.astype(BF16))
    w_in_p = _permute_w_in(w_in)
    for l in range(DEPTH):
        x1, za, zlru, sq, sk, sv, zml, zg = _ffn_inproj(
            xt, *ffn_w, (l, 0), row(ln_g[l, 0]), row(ln_b[l, 0]), w_in_p)
        seqv = lambda t: t.reshape(bsz, seq, t.shape[-1])
        wuq = jnp.zeros((Q_LORA, HEADS * GROUP_W), F32).at[:, q_dst].set(mla_w_uq[l][:, q_src]).astype(BF16)
        y_a = _mla(seqv(za), row(mla_q_norm[l]), row(mla_kv_norm[l]), wuq,
                   mla_w_ukv[l][:, kv_perm].astype(BF16), cos, sin)
        y_b = _rglru(seqv(zlru), lru_conv_w[l], row(lru_conv_b[l]),
                     _block_diag(lru_w_a[l]).astype(BF16), row(lru_b_a[l]),
                     _block_diag(lru_w_x[l]).astype(BF16), row(lru_b_x[l]), row(lru_lambda[l]))
        y_c = _swa(seqv(sq), seqv(sk), seqv(sv), swa_bias)
        gate_bias = jnp.broadcast_to(jnp.concatenate([ml_b_i[l], ml_b_f[l]])[:, None], (2 * HEADS, ML_CHUNK))
        y_d = _mlstm(seqv(zml), seqv(zg), gate_bias)
        ys = [t.reshape(n, GROUP_W) for t in (y_a, y_b, y_c, y_d)]
        xt = _outproj_ffn(x1, ys, row(out_norm[l]), w_out[l].astype(BF16), row(ln_g[l, 1]), row(ln_b[l, 1]),
                          *ffn_w, (l, 1), row(ln_g[l, 2]), row(ln_b[l, 2]))
    return xt.reshape(bsz, seq, d)
```

```python
import functools

import numpy as np
import jax
import jax.numpy as jnp
from jax import lax
from jax.experimental import pallas as pl
from jax.experimental.pallas import tpu as pltpu

F32 = jnp.float32
BF16 = jnp.bfloat16

D_MODEL = 1024
DEPTH = 2
GROUP_W = 256
HEAD_DIM = 64
HEADS = 4
D_FF = 2816
ALPHA = (2.0 * DEPTH) ** 0.25
Q_LORA = 256
KV_LORA = 128
MLA_NOPE = 64
MLA_ROPE = 32
ROPE_HALF = MLA_ROPE // 2
ROPE_THETA = 10000.0
CONV_W = 4
LRU_C = 8.0
DILATED_PATTERNS = ((128, 1), (512, 4), (2048, 16))

V7X_LANES = 128
V7X_SUBLANES = 8
V7X_VMEM_BYTES = 64 * 1024 * 1024

FF_CHUNK = 256
N_FF_CHUNKS = D_FF // FF_CHUNK
TOKEN_TILE = 512
ML_CHUNK = 256
ATT_T = 256
ATT_COLS = 1024
MLA_TQ = 512
PAIR_W = 2 * HEAD_DIM
N_BIAS_TILES = 4

Z_A = 512
Z_LRU = 512
Z_ML = 1024
Z_G = 128

NEG_INF = float("-inf")
LOG2E = float(np.log2(np.e))


MIB = 1024 * 1024
FFN_VMEM_LIMIT = 56 * MIB
MIXER_VMEM_LIMIT = 48 * MIB
assert max(FFN_VMEM_LIMIT, MIXER_VMEM_LIMIT) < V7X_VMEM_BYTES


def _const_spec(shape):
    nd = len(shape)
    return pl.BlockSpec(shape, lambda *_: (0,) * nd, pipeline_mode=pl.Buffered(1))


def _stacked_spec(shape, lead):
    tail = shape[len(lead):]
    return pl.BlockSpec((None,) * len(lead) + tuple(tail), lambda *_: tuple(lead) + (0,) * len(tail),
                        pipeline_mode=pl.Buffered(1))


def _layernorm(v, g, b):
    mu = jnp.mean(v, axis=-1, keepdims=True)
    d = v - mu
    var = jnp.mean(d * d, axis=-1, keepdims=True)
    return d * lax.rsqrt(var + 1e-5) * g + b


def _rmsnorm(v, g, eps=1e-6):
    return v * lax.rsqrt(jnp.mean(v * v, axis=-1, keepdims=True) + eps) * g


def _dot(a, b):
    return jnp.dot(a, b, preferred_element_type=F32)


def _dot_nt(a, b):
    return lax.dot_general(a, b, (((1,), (1,)), ((), ())), preferred_element_type=F32)


def _head_of_lane(width, group=HEAD_DIM):
    return lax.broadcasted_iota(jnp.int32, (1, width), 1) // group


def _ffn_ln(x, w1_ref, w3_ref, w2_ref, acc_ref, g, b):
    xb = x.astype(BF16)
    for c in range(N_FF_CHUNKS):
        cols = slice(c * FF_CHUNK, (c + 1) * FF_CHUNK)
        h1 = _dot(xb, w1_ref[:, cols])
        h3 = _dot(xb, w3_ref[:, cols])
        h = (h1 * jax.nn.sigmoid(h1)) * h3
        part = _dot(h.astype(BF16), w2_ref[cols, :])
        if c == 0:
            acc_ref[...] = part
        else:
            acc_ref[...] += part
    return _layernorm(ALPHA * x + 0.5 * acc_ref[...], g, b)


W13_STAGE_ROWS = 128
W2_STAGE_ROWS = 256


def _ffn_weight_scratch():
    return [pltpu.VMEM((D_MODEL, D_FF), BF16), pltpu.VMEM((D_MODEL, D_FF), BF16), pltpu.VMEM((D_FF, D_MODEL), BF16),
            pltpu.VMEM((2, W13_STAGE_ROWS, D_FF), F32), pltpu.VMEM((2, W2_STAGE_ROWS, D_MODEL), F32),
            pltpu.SemaphoreType.DMA((2,))]


def _stream_to_bf16(src_hbm, dst_ref, stage_ref, sem):
    rows = stage_ref.shape[1]
    n_chunks = src_hbm.shape[0] // rows

    def copy(c):
        return pltpu.make_async_copy(src_hbm.at[pl.ds(c * rows, rows), :], stage_ref.at[c % 2], sem.at[c % 2])

    copy(0).start()
    for c in range(n_chunks):
        if c + 1 < n_chunks:
            copy(c + 1).start()
        copy(c).wait()
        dst_ref[c * rows:(c + 1) * rows, :] = stage_ref[c % 2].astype(BF16)


def _stage_ffn_weights(hbm_refs, lj, scratch):
    w1_scr, w3_scr, w2_scr, stage13, stage2, sem = scratch

    @pl.when(pl.program_id(0) == 0)
    def _():
        for src, dst, stage in zip(hbm_refs, (w1_scr, w3_scr, w2_scr), (stage13, stage13, stage2)):
            _stream_to_bf16(src.at[lj[0], lj[1]], dst, stage, sem)

    return w1_scr, w3_scr, w2_scr


def _ffn_inproj_kernel(lj, x_ref, w1_hbm, w3_hbm, w2_hbm, g_ref, b_ref, win_ref,
                       x1_ref, za_ref, zlru_ref, sq_ref, sk_ref, sv_ref, zml_ref, zg_ref, acc_ref, *w_scratch):
    w1_ref, w3_ref, w2_ref = _stage_ffn_weights((w1_hbm, w3_hbm, w2_hbm), lj, w_scratch)
    x1 = _ffn_ln(x_ref[...], w1_ref, w3_ref, w2_ref, acc_ref, g_ref[...], b_ref[...])
    x1_ref[...] = x1
    xb = x1.astype(BF16)
    off = 0
    for ref in (za_ref, zlru_ref, sq_ref, sk_ref, sv_ref, zml_ref, zg_ref):
        w = ref.shape[-1]
        ref[...] = _dot(xb, win_ref[:, off:off + w])
        off += w


def _ffn_inproj(x, w1, w3, w2, lj, g, b, win):
    n = x.shape[0]
    tm = TOKEN_TILE
    tile = lambda w: pl.BlockSpec((tm, w), lambda i: (i, 0))
    widths = (D_MODEL, Z_A, Z_LRU, GROUP_W, GROUP_W, GROUP_W, Z_ML, Z_G)
    hbm = pl.BlockSpec(memory_space=pl.ANY)
    return pl.pallas_call(
        functools.partial(_ffn_inproj_kernel, lj),
        grid=(n // tm,),
        in_specs=[tile(D_MODEL), hbm, hbm, hbm,
                  _const_spec(g.shape), _const_spec(b.shape), _stacked_spec(win.shape, lj[:1])],
        out_specs=[tile(w) for w in widths],
        out_shape=[jax.ShapeDtypeStruct((n, w), F32) for w in widths],
        scratch_shapes=[pltpu.VMEM((tm, D_MODEL), F32)] + _ffn_weight_scratch(),
        compiler_params=pltpu.CompilerParams(dimension_semantics=("arbitrary",),
                                             vmem_limit_bytes=FFN_VMEM_LIMIT),
        name="ffn_inproj",
    )(x, w1, w3, w2, g, b, win)


def _outproj_ffn_kernel(lj, x_ref, ya_ref, yb_ref, yc_ref, yd_ref,
                        on_ref, wout_ref, g1_ref, b1_ref, w1_hbm, w3_hbm, w2_hbm, g2_ref, b2_ref,
                        out_ref, acc_ref, *w_scratch):
    w1_ref, w3_ref, w2_ref = _stage_ffn_weights((w1_hbm, w3_hbm, w2_hbm), lj, w_scratch)
    y = jnp.zeros(x_ref.shape, F32)
    for gi, y_ref in enumerate((ya_ref, yb_ref, yc_ref, yd_ref)):
        lo = gi * GROUP_W
        yn = _rmsnorm(y_ref[...], on_ref[:, lo:lo + GROUP_W])
        y = y + _dot(yn.astype(BF16), wout_ref[lo:lo + GROUP_W, :])
    x2 = _layernorm(ALPHA * x_ref[...] + y, g1_ref[...], b1_ref[...])
    out_ref[...] = _ffn_ln(x2, w1_ref, w3_ref, w2_ref, acc_ref, g2_ref[...], b2_ref[...])


def _outproj_ffn(x, ys, on, wout, g1, b1, w1, w3, w2, lj, g2, b2):
    n = x.shape[0]
    tm = TOKEN_TILE
    tile = lambda w: pl.BlockSpec((tm, w), lambda i: (i, 0))
    consts = (on, wout, g1, b1, w1, w3, w2, g2, b2)
    const_specs = [pl.BlockSpec(memory_space=pl.ANY) if c.ndim == 4 else _const_spec(c.shape) for c in consts]
    return pl.pallas_call(
        functools.partial(_outproj_ffn_kernel, lj),
        grid=(n // tm,),
        in_specs=[tile(D_MODEL)] + [tile(GROUP_W)] * len(ys) + const_specs,
        out_specs=tile(D_MODEL),
        out_shape=jax.ShapeDtypeStruct((n, D_MODEL), F32),
        scratch_shapes=[pltpu.VMEM((tm, D_MODEL), F32)] + _ffn_weight_scratch(),
        compiler_params=pltpu.CompilerParams(dimension_semantics=("arbitrary",),
                                             vmem_limit_bytes=FFN_VMEM_LIMIT),
        name="outproj_ffn",
    )(x, *ys, *consts)


def _flash_state_scratch():
    return [pltpu.VMEM((1, ATT_COLS), F32), pltpu.VMEM((1, ATT_COLS), F32), pltpu.VMEM((PAIR_W, ATT_COLS), F32)]


def _flash_reset(state):
    m_ref, l_ref, acc_ref = state
    m_ref[...] = jnp.full(m_ref.shape, NEG_INF, F32)
    l_ref[...] = jnp.zeros_like(l_ref)
    acc_ref[...] = jnp.zeros_like(acc_ref)


def _flash_pipe_scratch():
    return [pltpu.VMEM((2, ATT_T, ATT_COLS), F32), pltpu.VMEM((2, 1, ATT_COLS), F32)]


def _flash_attend(n_steps, qm, key_block, value_block, bias_block, state, pipe, groups):
    m_ref, l_ref, acc_ref = state
    s_ref, mx_ref = pipe
    _flash_reset(state)

    def scores(j, slot):
        bias = bias_block(j)
        s_t = _dot_nt(key_block(j), qm) + jnp.concatenate([bias] * (ATT_COLS // bias.shape[1]), axis=1)
        s_ref[slot] = s_t
        mx_ref[slot] = jnp.max(s_t, axis=0, keepdims=True)

    scores(0, 0)

    def consume(j, with_next):
        cur = lax.rem(j, 2)
        m = m_ref[...]
        m_new = jnp.maximum(m, mx_ref[cur])
        a = jnp.exp2(m - m_new)
        pr = jnp.exp2(s_ref[cur] - m_new)
        m_ref[...] = m_new
        l_ref[...] = a * l_ref[...] + jnp.sum(pr, axis=0, keepdims=True)
        prb = pr.astype(BF16)
        if with_next:
            scores(j + 1, 1 - cur)
        vt_blk = value_block(j)
        for v_lo, c_lo, c_hi in groups:
            acc_ref[:, c_lo:c_hi] = (a[:, c_lo:c_hi] * acc_ref[:, c_lo:c_hi]
                                     + _dot(vt_blk[v_lo:v_lo + PAIR_W, :], prb[:, c_lo:c_hi]))

    def step(j, carry):
        consume(j, True)
        return carry

    lax.fori_loop(0, n_steps - 1, step, 0)
    consume(n_steps - 1, False)


def _flash_result(state):
    _, l_ref, acc_ref = state
    return jnp.transpose(acc_ref[...] * (1.0 / l_ref[...]))


def _merge_pair(o_a, o_b):
    lane = lax.broadcasted_iota(jnp.int32, (1, PAIR_W), 1)
    return jnp.where(lane < HEAD_DIM, o_a, o_b)


def _store_transposed_values(v, vt_ref, j):
    vt_ref[j] = jnp.transpose(v).astype(BF16)


def _mla_kernel(za_ref, qn_ref, kvn_ref, wuq_ref, wukv_ref, cos_ref, sin_ref, o_ref,
                qm_scr, kp_scr, vt_scr, o_scr, bias_scr, m_scr, l_scr, acc_scr, s_scr, mx_scr):
    seq = za_ref.shape[0]
    state = (m_scr, l_scr, acc_scr)
    pipe = (s_scr, mx_scr)
    c_q = (MLA_NOPE + MLA_ROPE) ** -0.5 * LOG2E

    def prep(j, carry):
        rows = pl.ds(pl.multiple_of(j * ATT_T, ATT_T), ATT_T)
        cos, sin = cos_ref[rows, :], sin_ref[rows, :]

        def rope(r):
            return r * cos + pltpu.roll(r, V7X_LANES // 2, axis=1) * sin

        cq = _rmsnorm(za_ref[rows, 0:Q_LORA], qn_ref[...])
        q = _dot(cq.astype(BF16), wuq_ref[...]) * c_q
        for h in range(HEADS):
            lo = h * GROUP_W
            qm_scr[h, rows, 0:V7X_LANES] = q[:, lo:lo + V7X_LANES].astype(BF16)
            qm_scr[h, rows, V7X_LANES:] = rope(q[:, lo + V7X_LANES:lo + GROUP_W]).astype(BF16)
        ckv = _rmsnorm(za_ref[rows, Q_LORA:Q_LORA + KV_LORA], kvn_ref[...])
        kv = _dot(ckv.astype(BF16), wukv_ref[...])
        k_rope = rope(za_ref[rows, Q_LORA + KV_LORA:]).astype(BF16)
        for p in range(2):
            kp_scr[p, rows, 0:V7X_LANES] = kv[:, p * PAIR_W:(p + 1) * PAIR_W].astype(BF16)
            kp_scr[p, rows, V7X_LANES:] = k_rope
        _store_transposed_values(kv[:, GROUP_W:], vt_scr, j)
        return carry

    lax.fori_loop(0, seq // ATT_T, prep, 0)

    key = lax.broadcasted_iota(jnp.int32, (ATT_T, MLA_TQ), 0)
    qry = lax.broadcasted_iota(jnp.int32, (ATT_T, MLA_TQ), 1)
    n_diag = MLA_TQ // ATT_T
    bias_scr[0] = jnp.zeros((ATT_T, MLA_TQ), F32)
    for d in range(n_diag):
        bias_scr[1 + d] = jnp.where(key + d * ATT_T <= qry, 0.0, NEG_INF)
    groups = [(0, 0, ATT_COLS)]

    def q_block(i, carry):
        q0 = pl.multiple_of(i * MLA_TQ, MLA_TQ)
        q_rows = pl.ds(q0, MLA_TQ)
        first_diag = i * n_diag

        def pair(p, c):
            qm = jnp.concatenate([qm_scr[2 * p, q_rows, :], qm_scr[2 * p + 1, q_rows, :]], axis=0)
            v_lo = pl.multiple_of(p * PAIR_W, PAIR_W)
            _flash_attend(
                first_diag + n_diag, qm,
                lambda j: kp_scr[p, pl.ds(pl.multiple_of(j * ATT_T, ATT_T), ATT_T), :],
                lambda j: vt_scr.at[j, pl.ds(v_lo, PAIR_W), :],
                lambda j: bias_scr[jnp.maximum(j - first_diag + 1, 0)],
                state, pipe, groups)
            o_t = _flash_result(state)
            o_scr[p] = _merge_pair(o_t[:MLA_TQ], o_t[MLA_TQ:])
            return c

        lax.fori_loop(0, 2, pair, 0)
        for p in range(2):
            o_ref[q_rows, p * PAIR_W:(p + 1) * PAIR_W] = o_scr[p]
        return carry

    lax.fori_loop(0, seq // MLA_TQ, q_block, 0)


def _mla(za, qn, kvn, wuq, wukv, cos, sin):
    b, s, _ = za.shape
    consts = (qn, kvn, wuq, wukv, cos, sin)
    return pl.pallas_call(
        _mla_kernel,
        grid=(b,),
        in_specs=[pl.BlockSpec((None, s, Z_A), lambda i: (i, 0, 0))] + [_const_spec(c.shape) for c in consts],
        out_specs=pl.BlockSpec((None, s, GROUP_W), lambda i: (i, 0, 0)),
        out_shape=jax.ShapeDtypeStruct((b, s, GROUP_W), F32),
        scratch_shapes=([pltpu.VMEM((HEADS, s, GROUP_W), BF16), pltpu.VMEM((2, s, GROUP_W), BF16),
                         pltpu.VMEM((s // ATT_T, GROUP_W, ATT_T), BF16), pltpu.VMEM((2, MLA_TQ, PAIR_W), F32),
                         pltpu.VMEM((1 + MLA_TQ // ATT_T, ATT_T, MLA_TQ), F32)]
                        + _flash_state_scratch() + _flash_pipe_scratch()),
        compiler_params=pltpu.CompilerParams(dimension_semantics=("arbitrary",),
                                             vmem_limit_bytes=MIXER_VMEM_LIMIT),
        name="mla",
    )(za, *consts)


def _swa_kernel(q_ref, k_ref, v_ref, bias_ref, o_ref, qm_scr, k_scr, vt_scr, m_scr, l_scr, acc_scr,
                s_scr, mx_scr):
    seq = q_ref.shape[0]
    state = (m_scr, l_scr, acc_scr)
    pipe = (s_scr, mx_scr)
    head = _head_of_lane(GROUP_W)
    c_q = HEAD_DIM ** -0.5 * LOG2E
    half = ATT_COLS // 2
    groups = [(0, 0, half), (PAIR_W, half, ATT_COLS)]

    def prep(j, carry):
        rows = pl.ds(pl.multiple_of(j * ATT_T, ATT_T), ATT_T)
        q = (q_ref[rows, :] * c_q).astype(BF16)
        for h in range(HEADS):
            qm_scr[h, rows, :] = jnp.where(head == h, q, jnp.zeros_like(q))
        k_scr[rows, :] = k_ref[rows, :].astype(BF16)
        _store_transposed_values(v_ref[rows, :], vt_scr, j)
        return carry

    lax.fori_loop(0, seq // ATT_T, prep, 0)

    def q_block(i, carry):
        q_rows = pl.ds(pl.multiple_of(i * ATT_T, ATT_T), ATT_T)
        qm = jnp.concatenate([qm_scr[h, q_rows, :] for h in range(HEADS)], axis=0)
        _flash_attend(
            i + 1, qm,
            lambda j: k_scr[pl.ds(pl.multiple_of(j * ATT_T, ATT_T), ATT_T), :],
            lambda j: vt_scr.at[j],
            lambda j: bias_ref[jnp.minimum(i - j, N_BIAS_TILES - 1)],
            state, pipe, groups)
        o_t = _flash_result(state)
        for p in range(2):
            lo = 2 * p * ATT_T
            o_ref[q_rows, p * PAIR_W:(p + 1) * PAIR_W] = _merge_pair(o_t[lo:lo + ATT_T], o_t[lo + ATT_T:lo + 2 * ATT_T])
        return carry

    lax.fori_loop(0, seq // ATT_T, q_block, 0)


def _swa_bias_tiles(seq):
    assert DILATED_PATTERNS[-1][0] >= seq - 1, "far tiles assume the widest window covers the sequence"
    assert all(w <= (N_BIAS_TILES - 2) * ATT_T for w, _ in DILATED_PATTERNS[:-1]), "narrow windows must end before the far tile"
    key = np.arange(ATT_T)[:, None]
    qry = np.arange(ATT_T)[None, :]
    tiles = []
    for dist in range(N_BIAS_TILES):
        d = dist * ATT_T + qry - key
        mult = sum(((d >= 0) & (d % dil == 0) & (d <= window)).astype(np.float32) for window, dil in DILATED_PATTERNS)
        with np.errstate(divide="ignore"):
            tiles.append(np.log2(mult))
    return jnp.asarray(np.stack(tiles), F32)


def _swa(q, k, v, bias):
    b, s, w = q.shape
    spec = pl.BlockSpec((None, s, w), lambda i: (i, 0, 0))
    return pl.pallas_call(
        _swa_kernel,
        grid=(b,),
        in_specs=[spec, spec, spec, _const_spec(bias.shape)],
        out_specs=spec,
        out_shape=jax.ShapeDtypeStruct((b, s, w), F32),
        scratch_shapes=([pltpu.VMEM((HEADS, s, w), BF16), pltpu.VMEM((s, w), BF16),
                         pltpu.VMEM((s // ATT_T, w, ATT_T), BF16)] + _flash_state_scratch()
                        + _flash_pipe_scratch()),
        compiler_params=pltpu.CompilerParams(dimension_semantics=("arbitrary",),
                                             vmem_limit_bytes=MIXER_VMEM_LIMIT),
        name="swa",
    )(q, k, v, bias)


def _softplus(x):
    return jnp.maximum(x, 0.0) + jnp.log1p(jnp.exp(-jnp.abs(x)))


def _gelu_tanh(x):
    c = np.sqrt(2.0 / np.pi).astype(np.float32)
    return 0.5 * x * (1.0 + jnp.tanh(c * (x + 0.044715 * (x * x * x))))


def _rglru_kernel(z_ref, cw_ref, cb_ref, wa_ref, ba_ref, wx_ref, bx_ref, lam_ref, o_ref, a_scr, h_scr):
    seq = z_ref.shape[0]
    h_scr[0:V7X_SUBLANES, :] = jnp.zeros((V7X_SUBLANES, GROUP_W), F32)
    h_scr[V7X_SUBLANES:V7X_SUBLANES + seq, :] = z_ref[:, 0:GROUP_W]
    xc = cb_ref[...]
    for j in range(CONV_W):
        start = V7X_SUBLANES - (CONV_W - 1 - j)
        xc = xc + cw_ref[j:j + 1, :] * h_scr[start:start + seq, :]
    xcb = xc.astype(BF16)
    r = jax.nn.sigmoid(_dot(xcb, wa_ref[...]) + ba_ref[...])
    i = jax.nn.sigmoid(_dot(xcb, wx_ref[...]) + bx_ref[...])
    log_a = -LRU_C * r * _softplus(-lam_ref[...])
    a = jnp.exp(log_a)
    gain2 = -jnp.tanh(log_a) * (a * a + 1.0)
    gain = jnp.where(gain2 > 0.0, gain2 * lax.rsqrt(gain2), 0.0)
    u = gain * (i * xc)
    groups = (seq // V7X_SUBLANES, V7X_SUBLANES, GROUP_W)
    a, u = a.reshape(groups), u.reshape(groups)
    in_group = lax.broadcasted_iota(jnp.int32, groups, 1)
    d = 1
    while d < V7X_SUBLANES:
        keep = in_group >= d
        u = a * jnp.where(keep, pltpu.roll(u, d, axis=1), 0.0) + u
        a = a * jnp.where(keep, pltpu.roll(a, d, axis=1), 1.0)
        d *= 2
    a_scr[...] = a.reshape(seq, GROUP_W)
    h_scr[0:seq, :] = u.reshape(seq, GROUP_W)

    def group(g, h_prev):
        rows = pl.ds(pl.multiple_of(g * V7X_SUBLANES, V7X_SUBLANES), V7X_SUBLANES)
        h = h_scr[rows, :] + a_scr[rows, :] * h_prev
        h_scr[rows, :] = h
        return jnp.broadcast_to(h[V7X_SUBLANES - 1:, :], h.shape)

    lax.fori_loop(0, seq // V7X_SUBLANES, group, jnp.zeros((V7X_SUBLANES, GROUP_W), F32), unroll=8)
    o_ref[...] = h_scr[0:seq, :] * _gelu_tanh(z_ref[:, GROUP_W:])


def _rglru(z, cw, cb, wa, ba, wx, bx, lam):
    b, s, _ = z.shape
    consts = (cw, cb, wa, ba, wx, bx, lam)
    return pl.pallas_call(
        _rglru_kernel,
        grid=(b,),
        in_specs=[pl.BlockSpec((None, s, Z_LRU), lambda i: (i, 0, 0))] + [_const_spec(c.shape) for c in consts],
        out_specs=pl.BlockSpec((None, s, GROUP_W), lambda i: (i, 0, 0)),
        out_shape=jax.ShapeDtypeStruct((b, s, GROUP_W), F32),
        scratch_shapes=[pltpu.VMEM((s, GROUP_W), F32), pltpu.VMEM((s + V7X_SUBLANES, GROUP_W), F32)],
        compiler_params=pltpu.CompilerParams(dimension_semantics=("arbitrary",),
                                             vmem_limit_bytes=MIXER_VMEM_LIMIT),
        name="rglru",
    )(z, *consts)


def _dot_exact(a, b):
    return jnp.dot(a, b, precision=lax.Precision.HIGHEST, preferred_element_type=F32)


def _per_head_rows(rows, height):
    return jnp.concatenate([jnp.broadcast_to(rows[h:h + 1, :], (height, rows.shape[1])) for h in range(HEADS)], axis=0)


def _lane_constant_to_width(x, width):
    w = x.shape[1]
    return x[:, :width] if w >= width else jnp.concatenate([x] * (width // w), axis=1)


def _mlstm_kernel(z_ref, g_ref, gb_ref, o_ref, ct_scr, n_scr, m_scr):
    seq = z_ref.shape[0]
    lc = ML_CHUNK
    sub = lax.broadcasted_iota(jnp.int32, (lc, lc), 0)
    lan = lax.broadcasted_iota(jnp.int32, (lc, lc), 1)
    tri_incl = jnp.where(sub <= lan, 1.0, 0.0)
    visible = sub <= lan
    head = _head_of_lane(GROUP_W)
    same_head = lax.broadcasted_iota(jnp.int32, (GROUP_W, GROUP_W), 0) // HEAD_DIM == head
    n_rows = lax.broadcasted_iota(jnp.int32, (V7X_SUBLANES, GROUP_W), 0) == head
    pad_rows = jnp.zeros((lc - V7X_SUBLANES, lc), F32)
    ct_scr[...] = jnp.zeros_like(ct_scr)
    n_scr[...] = jnp.zeros_like(n_scr)
    m_scr[...] = jnp.zeros_like(m_scr)

    def chunk(ci, carry):
        rows = pl.ds(pl.multiple_of(ci * lc, lc), lc)
        q = z_ref[rows, 0:GROUP_W].astype(BF16)
        k = (z_ref[rows, GROUP_W:2 * GROUP_W] * (HEAD_DIM ** -0.5)).astype(BF16)
        v_t = jnp.transpose(z_ref[rows, 2 * GROUP_W:3 * GROUP_W])
        og = z_ref[rows, 3 * GROUP_W:4 * GROUP_W]
        pre = jnp.transpose(g_ref[rows, :])[0:V7X_SUBLANES, :] + gb_ref[...]
        log_f = jnp.minimum(pre, 0.0) - jnp.log1p(jnp.exp(-jnp.abs(pre)))
        b_row = pltpu.roll(_dot_exact(log_f, tri_incl), HEADS, axis=0)
        key_row = pre - b_row
        key_col = jnp.transpose(jnp.concatenate([key_row, pad_rows], axis=0))
        m_prev = m_scr[...]
        ct_prev = ct_scr[...]
        n_prev = n_scr[...]
        inter_all = b_row + m_prev

        qs = jnp.concatenate([jnp.where(head == h, q, jnp.zeros_like(q)) for h in range(HEADS)], axis=0)
        s_all = _dot_nt(k, qs)
        qc_t = _dot_nt(ct_prev.astype(BF16), q)
        qn = _dot_nt(n_prev.astype(BF16), q)
        outs = []
        for h in range(HEADS):
            b_t = b_row[h:h + 1, :]
            dmat = jnp.where(visible, b_t + key_col[:, h:h + 1], NEG_INF)
            inter = inter_all[h:h + 1, :]
            mt = jnp.maximum(jnp.max(dmat, axis=0, keepdims=True), inter)
            w_t = s_all[:, h * lc:(h + 1) * lc] * jnp.exp(dmat - mt)
            w_inter = jnp.exp(inter - mt)
            lanes = slice(h * HEAD_DIM, (h + 1) * HEAD_DIM)
            num = _dot(v_t[lanes, :].astype(BF16), w_t.astype(BF16)) + w_inter * qc_t[lanes, :]
            den = jnp.sum(w_t, axis=0, keepdims=True) + w_inter * qn[h:h + 1, :]
            outs.append(num * (1.0 / jnp.maximum(jnp.abs(den), jnp.exp(-mt))))
        o_ref[rows, :] = jax.nn.sigmoid(og) * jnp.transpose(jnp.concatenate(outs, axis=0))

        g_all = jnp.broadcast_to(b_row[:, lc - 1:lc], b_row.shape)
        wk = g_all + key_row
        m_new = jnp.maximum(g_all + m_prev, jnp.max(wk, axis=1, keepdims=True))
        decay = jnp.exp(g_all + m_prev - m_new)
        w_row = jnp.exp(wk - m_new)
        vw_t = (v_t * _per_head_rows(w_row, HEAD_DIM)).astype(BF16)
        decay_w = _lane_constant_to_width(decay, GROUP_W)
        ct_scr[...] = _per_head_rows(decay_w, HEAD_DIM) * ct_prev + jnp.where(same_head, _dot(vw_t, k), 0.0)
        n_scr[...] = decay_w * n_prev + jnp.where(n_rows, _dot(w_row.astype(BF16), k), 0.0)
        m_scr[...] = m_new
        return carry

    lax.fori_loop(0, seq // lc, chunk, 0, unroll=4)


def _mlstm(z, gates, gate_bias):
    b, s, _ = z.shape
    return pl.pallas_call(
        _mlstm_kernel,
        grid=(b,),
        in_specs=[pl.BlockSpec((None, s, Z_ML), lambda i: (i, 0, 0)),
                  pl.BlockSpec((None, s, Z_G), lambda i: (i, 0, 0)),
                  _const_spec(gate_bias.shape)],
        out_specs=pl.BlockSpec((None, s, GROUP_W), lambda i: (i, 0, 0)),
        out_shape=jax.ShapeDtypeStruct((b, s, GROUP_W), F32),
        scratch_shapes=[pltpu.VMEM((GROUP_W, GROUP_W), F32), pltpu.VMEM((V7X_SUBLANES, GROUP_W), F32),
                        pltpu.VMEM((V7X_SUBLANES, ML_CHUNK), F32)],
        compiler_params=pltpu.CompilerParams(dimension_semantics=("arbitrary",),
                                             vmem_limit_bytes=MIXER_VMEM_LIMIT),
        name="mlstm",
    )(z, gates, gate_bias)


def _block_diag(w):
    out = jnp.zeros((GROUP_W, GROUP_W), w.dtype)
    for h in range(HEADS):
        out = out.at[h * HEAD_DIM:(h + 1) * HEAD_DIM, h * HEAD_DIM:(h + 1) * HEAD_DIM].set(w[h])
    return out


def _permute_w_in(w):
    kr = Q_LORA + KV_LORA
    zeros = lambda width: jnp.zeros(w.shape[:2] + (width,), w.dtype)
    gap = zeros(HEAD_DIM - ROPE_HALF)
    body_lo = kr + MLA_ROPE
    body_hi = body_lo + 9 * GROUP_W
    pad = zeros(Z_G - 2 * HEADS)
    return jnp.concatenate([w[..., :kr], w[..., kr:kr + ROPE_HALF], gap, w[..., kr + ROPE_HALF:kr + MLA_ROPE], gap,
                            w[..., body_lo:body_hi], w[..., body_hi:], pad], axis=-1).astype(BF16)


def _mla_up_layout():
    per_q = MLA_NOPE + MLA_ROPE
    q_src, q_dst = [], []
    for h in range(HEADS):
        base = h * GROUP_W
        nope_lo = base + (h % 2) * HEAD_DIM
        q_src += [h * per_q + i for i in range(per_q)]
        q_dst += [nope_lo + i for i in range(MLA_NOPE)]
        q_dst += [base + V7X_LANES + i for i in range(ROPE_HALF)]
        q_dst += [base + V7X_LANES + HEAD_DIM + i for i in range(ROPE_HALF)]
    per_kv = MLA_NOPE + HEAD_DIM
    kn = [h * per_kv + i for h in range(HEADS) for i in range(MLA_NOPE)]
    vv = [h * per_kv + MLA_NOPE + i for h in range(HEADS) for i in range(HEAD_DIM)]
    return np.array(q_src), np.array(q_dst), np.array(kn + vv)


def _rope_tables(seq):
    freqs = ROPE_THETA ** (-jnp.arange(ROPE_HALF, dtype=F32) / ROPE_HALF)
    ang = jnp.arange(seq, dtype=F32)[:, None] * freqs[None, :]
    cos = jnp.tile(jnp.cos(ang), (1, 2 * HEADS))
    sin = jnp.tile(jnp.sin(ang), (1, HEADS))
    return cos, jnp.concatenate([-sin, sin], axis=1)


def kernel(x, ln_g, ln_b, ffn_w1, ffn_w3, ffn_w2, w_in, mla_q_norm, mla_kv_norm, mla_w_uq, mla_w_ukv,
           lru_conv_w, lru_conv_b, lru_w_a, lru_b_a, lru_w_x, lru_b_x, lru_lambda, ml_b_i, ml_b_f,
           out_norm, w_out):
    bsz, seq, d = x.shape
    n = bsz * seq
    row = lambda t: t.reshape(1, -1)
    q_src, q_dst, kv_perm = _mla_up_layout()
    cos, sin = _rope_tables(seq)
    swa_bias = _swa_bias_tiles(seq)
    xt = x.reshape(n, d)
    ffn_w = (ffn_w1, ffn_w3, ffn_w2)
    w_in_p = _permute_w_in(w_in)
    for l in range(DEPTH):
        x1, za, zlru, sq, sk, sv, zml, zg = _ffn_inproj(
            xt, *ffn_w, (l, 0), row(ln_g[l, 0]), row(ln_b[l, 0]), w_in_p)
        seqv = lambda t: t.reshape(bsz, seq, t.shape[-1])
        wuq = jnp.zeros((Q_LORA, HEADS * GROUP_W), F32).at[:, q_dst].set(mla_w_uq[l][:, q_src]).astype(BF16)
        y_a = _mla(seqv(za), row(mla_q_norm[l]), row(mla_kv_norm[l]), wuq,
                   mla_w_ukv[l][:, kv_perm].astype(BF16), cos, sin)
        y_b = _rglru(seqv(zlru), lru_conv_w[l], row(lru_conv_b[l]),
                     _block_diag(lru_w_a[l]).astype(BF16), row(lru_b_a[l]),
                     _block_diag(lru_w_x[l]).astype(BF16), row(lru_b_x[l]), row(lru_lambda[l]))
        y_c = _swa(seqv(sq), seqv(sk), seqv(sv), swa_bias)
        gate_bias = jnp.broadcast_to(jnp.concatenate([ml_b_i[l], ml_b_f[l]])[:, None], (2 * HEADS, ML_CHUNK))
        y_d = _mlstm(seqv(zml), seqv(zg), gate_bias)
        ys = [t.reshape(n, GROUP_W) for t in (y_a, y_b, y_c, y_d)]
        xt = _outproj_ffn(x1, ys, row(out_norm[l]), w_out[l].astype(BF16), row(ln_g[l, 1]), row(ln_b[l, 1]),
                          *ffn_w, (l, 1), row(ln_g[l, 2]), row(ln_b[l, 2]))
    return xt.reshape(bsz, seq, d)
```

```python
import numpy as np
import jax
import jax.numpy as jnp
from jax import lax
from jax.experimental import pallas as pl
from jax.experimental.pallas import tpu as pltpu

F32 = jnp.float32
BF16 = jnp.bfloat16

D_MODEL = 1024
DEPTH = 2
GROUP_W = 256
HEAD_DIM = 64
HEADS = 4
D_FF = 2816
ALPHA = (2.0 * DEPTH) ** 0.25
Q_LORA = 256
KV_LORA = 128
MLA_NOPE = 64
MLA_ROPE = 32
ROPE_HALF = MLA_ROPE // 2
ROPE_THETA = 10000.0
CONV_W = 4
LRU_C = 8.0
DILATED_PATTERNS = ((128, 1), (512, 4), (2048, 16))

V7X_LANES = 128
V7X_SUBLANES = 8
V7X_VMEM_BYTES = 64 * 1024 * 1024

FF_CHUNK = 256
N_FF_CHUNKS = D_FF // FF_CHUNK
TOKEN_TILE = 512
ML_CHUNK = 256
ATT_T = 256
ATT_COLS = 1024
MLA_TQ = 512
PAIR_W = 2 * HEAD_DIM
N_BIAS_TILES = 4

Z_A = 512
Z_LRU = 512
Z_ML = 1024
Z_G = 128

NEG_INF = float("-inf")
LOG2E = float(np.log2(np.e))


MIB = 1024 * 1024
FFN_VMEM_LIMIT = 56 * MIB
MIXER_VMEM_LIMIT = 48 * MIB
assert max(FFN_VMEM_LIMIT, MIXER_VMEM_LIMIT) < V7X_VMEM_BYTES


def _const_spec(shape):
    nd = len(shape)
    return pl.BlockSpec(shape, lambda *_: (0,) * nd, pipeline_mode=pl.Buffered(1))


def _stacked_spec(shape, lead):
    tail = shape[len(lead):]
    return pl.BlockSpec((None,) * len(lead) + tuple(tail), lambda *_: tuple(lead) + (0,) * len(tail),
                        pipeline_mode=pl.Buffered(1))


def _layernorm(v, g, b):
    mu = jnp.mean(v, axis=-1, keepdims=True)
    d = v - mu
    var = jnp.mean(d * d, axis=-1, keepdims=True)
    return d * lax.rsqrt(var + 1e-5) * g + b


def _rmsnorm(v, g, eps=1e-6):
    return v * lax.rsqrt(jnp.mean(v * v, axis=-1, keepdims=True) + eps) * g


def _dot(a, b):
    return jnp.dot(a, b, preferred_element_type=F32)


def _dot_nt(a, b):
    return lax.dot_general(a, b, (((1,), (1,)), ((), ())), preferred_element_type=F32)


def _head_of_lane(width, group=HEAD_DIM):
    return lax.broadcasted_iota(jnp.int32, (1, width), 1) // group


def _ffn_ln(x, w1_ref, w3_ref, w2_ref, acc_ref, g, b):
    xb = x.astype(BF16)
    for c in range(N_FF_CHUNKS):
        cols = slice(c * FF_CHUNK, (c + 1) * FF_CHUNK)
        h1 = _dot(xb, w1_ref[:, cols])
        h3 = _dot(xb, w3_ref[:, cols])
        h = (h1 * jax.nn.sigmoid(h1)) * h3
        part = _dot(h.astype(BF16), w2_ref[cols, :])
        if c == 0:
            acc_ref[...] = part
        else:
            acc_ref[...] += part
    return _layernorm(ALPHA * x + 0.5 * acc_ref[...], g, b)


def _ffn_inproj_kernel(x_ref, w1_ref, w3_ref, w2_ref, g_ref, b_ref, win_ref,
                       x1_ref, za_ref, zlru_ref, sq_ref, sk_ref, sv_ref, zml_ref, zg_ref, acc_ref):
    x1 = _ffn_ln(x_ref[...], w1_ref, w3_ref, w2_ref, acc_ref, g_ref[...], b_ref[...])
    x1_ref[...] = x1
    xb = x1.astype(BF16)
    off = 0
    for ref in (za_ref, zlru_ref, sq_ref, sk_ref, sv_ref, zml_ref, zg_ref):
        w = ref.shape[-1]
        ref[...] = _dot(xb, win_ref[:, off:off + w])
        off += w


def _ffn_inproj(x, w1, w3, w2, lj, g, b, win):
    n = x.shape[0]
    tm = TOKEN_TILE
    tile = lambda w: pl.BlockSpec((tm, w), lambda i: (i, 0))
    widths = (D_MODEL, Z_A, Z_LRU, GROUP_W, GROUP_W, GROUP_W, Z_ML, Z_G)
    return pl.pallas_call(
        _ffn_inproj_kernel,
        grid=(n // tm,),
        in_specs=[tile(D_MODEL), _stacked_spec(w1.shape, lj), _stacked_spec(w3.shape, lj), _stacked_spec(w2.shape, lj),
                  _const_spec(g.shape), _const_spec(b.shape), _stacked_spec(win.shape, lj[:1])],
        out_specs=[tile(w) for w in widths],
        out_shape=[jax.ShapeDtypeStruct((n, w), F32) for w in widths],
        scratch_shapes=[pltpu.VMEM((tm, D_MODEL), F32)],
        compiler_params=pltpu.CompilerParams(dimension_semantics=("arbitrary",),
                                             vmem_limit_bytes=FFN_VMEM_LIMIT),
        name="ffn_inproj",
    )(x, w1, w3, w2, g, b, win)


def _outproj_ffn_kernel(x_ref, ya_ref, yb_ref, yc_ref, yd_ref,
                        on_ref, wout_ref, g1_ref, b1_ref, w1_ref, w3_ref, w2_ref, g2_ref, b2_ref,
                        out_ref, acc_ref):
    y = jnp.zeros(x_ref.shape, F32)
    for gi, y_ref in enumerate((ya_ref, yb_ref, yc_ref, yd_ref)):
        lo = gi * GROUP_W
        yn = _rmsnorm(y_ref[...], on_ref[:, lo:lo + GROUP_W])
        y = y + _dot(yn.astype(BF16), wout_ref[lo:lo + GROUP_W, :])
    x2 = _layernorm(ALPHA * x_ref[...] + y, g1_ref[...], b1_ref[...])
    out_ref[...] = _ffn_ln(x2, w1_ref, w3_ref, w2_ref, acc_ref, g2_ref[...], b2_ref[...])


def _outproj_ffn(x, ys, on, wout, g1, b1, w1, w3, w2, lj, g2, b2):
    n = x.shape[0]
    tm = TOKEN_TILE
    tile = lambda w: pl.BlockSpec((tm, w), lambda i: (i, 0))
    consts = (on, wout, g1, b1, w1, w3, w2, g2, b2)
    const_specs = [_stacked_spec(c.shape, lj) if c.ndim == 4 else _const_spec(c.shape) for c in consts]
    return pl.pallas_call(
        _outproj_ffn_kernel,
        grid=(n // tm,),
        in_specs=[tile(D_MODEL)] + [tile(GROUP_W)] * len(ys) + const_specs,
        out_specs=tile(D_MODEL),
        out_shape=jax.ShapeDtypeStruct((n, D_MODEL), F32),
        scratch_shapes=[pltpu.VMEM((tm, D_MODEL), F32)],
        compiler_params=pltpu.CompilerParams(dimension_semantics=("arbitrary",),
                                             vmem_limit_bytes=FFN_VMEM_LIMIT),
        name="outproj_ffn",
    )(x, *ys, *consts)


def _flash_state_scratch():
    return [pltpu.VMEM((1, ATT_COLS), F32), pltpu.VMEM((1, ATT_COLS), F32), pltpu.VMEM((PAIR_W, ATT_COLS), F32)]


def _flash_reset(state):
    m_ref, l_ref, acc_ref = state
    m_ref[...] = jnp.full(m_ref.shape, NEG_INF, F32)
    l_ref[...] = jnp.zeros_like(l_ref)
    acc_ref[...] = jnp.zeros_like(acc_ref)


def _flash_pipe_scratch():
    return [pltpu.VMEM((2, ATT_T, ATT_COLS), F32), pltpu.VMEM((2, 1, ATT_COLS), F32)]


def _flash_attend(n_steps, qm, key_block, value_block, bias_block, state, pipe, groups):
    m_ref, l_ref, acc_ref = state
    s_ref, mx_ref = pipe
    _flash_reset(state)

    def scores(j, slot):
        bias = bias_block(j)
        s_t = _dot_nt(key_block(j), qm) + jnp.concatenate([bias] * (ATT_COLS // bias.shape[1]), axis=1)
        s_ref[slot] = s_t
        mx_ref[slot] = jnp.max(s_t, axis=0, keepdims=True)

    scores(0, 0)

    def consume(j, with_next):
        cur = lax.rem(j, 2)
        m = m_ref[...]
        m_new = jnp.maximum(m, mx_ref[cur])
        a = jnp.exp2(m - m_new)
        pr = jnp.exp2(s_ref[cur] - m_new)
        m_ref[...] = m_new
        l_ref[...] = a * l_ref[...] + jnp.sum(pr, axis=0, keepdims=True)
        prb = pr.astype(BF16)
        if with_next:
            scores(j + 1, 1 - cur)
        vt_blk = value_block(j)
        for v_lo, c_lo, c_hi in groups:
            acc_ref[:, c_lo:c_hi] = (a[:, c_lo:c_hi] * acc_ref[:, c_lo:c_hi]
                                     + _dot(vt_blk[v_lo:v_lo + PAIR_W, :], prb[:, c_lo:c_hi]))

    def step(j, carry):
        consume(j, True)
        return carry

    lax.fori_loop(0, n_steps - 1, step, 0)
    consume(n_steps - 1, False)


def _flash_result(state):
    _, l_ref, acc_ref = state
    return jnp.transpose(acc_ref[...] * (1.0 / l_ref[...]))


def _merge_pair(o_a, o_b):
    lane = lax.broadcasted_iota(jnp.int32, (1, PAIR_W), 1)
    return jnp.where(lane < HEAD_DIM, o_a, o_b)


def _store_transposed_values(v, vt_ref, j):
    vt_ref[j] = jnp.transpose(v).astype(BF16)


def _mla_kernel(za_ref, qn_ref, kvn_ref, wuq_ref, wukv_ref, cos_ref, sin_ref, o_ref,
                qm_scr, kp_scr, vt_scr, o_scr, bias_scr, m_scr, l_scr, acc_scr, s_scr, mx_scr):
    seq = za_ref.shape[0]
    state = (m_scr, l_scr, acc_scr)
    pipe = (s_scr, mx_scr)
    c_q = (MLA_NOPE + MLA_ROPE) ** -0.5 * LOG2E

    def prep(j, carry):
        rows = pl.ds(pl.multiple_of(j * ATT_T, ATT_T), ATT_T)
        cos, sin = cos_ref[rows, :], sin_ref[rows, :]

        def rope(r):
            return r * cos + pltpu.roll(r, V7X_LANES // 2, axis=1) * sin

        cq = _rmsnorm(za_ref[rows, 0:Q_LORA], qn_ref[...])
        q = _dot(cq.astype(BF16), wuq_ref[...]) * c_q
        for h in range(HEADS):
            lo = h * GROUP_W
            qm_scr[h, rows, 0:V7X_LANES] = q[:, lo:lo + V7X_LANES].astype(BF16)
            qm_scr[h, rows, V7X_LANES:] = rope(q[:, lo + V7X_LANES:lo + GROUP_W]).astype(BF16)
        ckv = _rmsnorm(za_ref[rows, Q_LORA:Q_LORA + KV_LORA], kvn_ref[...])
        kv = _dot(ckv.astype(BF16), wukv_ref[...])
        k_rope = rope(za_ref[rows, Q_LORA + KV_LORA:]).astype(BF16)
        for p in range(2):
            kp_scr[p, rows, 0:V7X_LANES] = kv[:, p * PAIR_W:(p + 1) * PAIR_W].astype(BF16)
            kp_scr[p, rows, V7X_LANES:] = k_rope
        _store_transposed_values(kv[:, GROUP_W:], vt_scr, j)
        return carry

    lax.fori_loop(0, seq // ATT_T, prep, 0)

    key = lax.broadcasted_iota(jnp.int32, (ATT_T, MLA_TQ), 0)
    qry = lax.broadcasted_iota(jnp.int32, (ATT_T, MLA_TQ), 1)
    n_diag = MLA_TQ // ATT_T
    bias_scr[0] = jnp.zeros((ATT_T, MLA_TQ), F32)
    for d in range(n_diag):
        bias_scr[1 + d] = jnp.where(key + d * ATT_T <= qry, 0.0, NEG_INF)
    groups = [(0, 0, ATT_COLS)]

    def q_block(i, carry):
        q0 = pl.multiple_of(i * MLA_TQ, MLA_TQ)
        q_rows = pl.ds(q0, MLA_TQ)
        first_diag = i * n_diag

        def pair(p, c):
            qm = jnp.concatenate([qm_scr[2 * p, q_rows, :], qm_scr[2 * p + 1, q_rows, :]], axis=0)
            v_lo = pl.multiple_of(p * PAIR_W, PAIR_W)
            _flash_attend(
                first_diag + n_diag, qm,
                lambda j: kp_scr[p, pl.ds(pl.multiple_of(j * ATT_T, ATT_T), ATT_T), :],
                lambda j: vt_scr.at[j, pl.ds(v_lo, PAIR_W), :],
                lambda j: bias_scr[jnp.maximum(j - first_diag + 1, 0)],
                state, pipe, groups)
            o_t = _flash_result(state)
            o_scr[p] = _merge_pair(o_t[:MLA_TQ], o_t[MLA_TQ:])
            return c

        lax.fori_loop(0, 2, pair, 0)
        for p in range(2):
            o_ref[q_rows, p * PAIR_W:(p + 1) * PAIR_W] = o_scr[p]
        return carry

    lax.fori_loop(0, seq // MLA_TQ, q_block, 0)


def _mla(za, qn, kvn, wuq, wukv, cos, sin):
    b, s, _ = za.shape
    consts = (qn, kvn, wuq, wukv, cos, sin)
    return pl.pallas_call(
        _mla_kernel,
        grid=(b,),
        in_specs=[pl.BlockSpec((None, s, Z_A), lambda i: (i, 0, 0))] + [_const_spec(c.shape) for c in consts],
        out_specs=pl.BlockSpec((None, s, GROUP_W), lambda i: (i, 0, 0)),
        out_shape=jax.ShapeDtypeStruct((b, s, GROUP_W), F32),
        scratch_shapes=([pltpu.VMEM((HEADS, s, GROUP_W), BF16), pltpu.VMEM((2, s, GROUP_W), BF16),
                         pltpu.VMEM((s // ATT_T, GROUP_W, ATT_T), BF16), pltpu.VMEM((2, MLA_TQ, PAIR_W), F32),
                         pltpu.VMEM((1 + MLA_TQ // ATT_T, ATT_T, MLA_TQ), F32)]
                        + _flash_state_scratch() + _flash_pipe_scratch()),
        compiler_params=pltpu.CompilerParams(dimension_semantics=("arbitrary",),
                                             vmem_limit_bytes=MIXER_VMEM_LIMIT),
        name="mla",
    )(za, *consts)


def _swa_kernel(q_ref, k_ref, v_ref, bias_ref, o_ref, qm_scr, k_scr, vt_scr, m_scr, l_scr, acc_scr,
                s_scr, mx_scr):
    seq = q_ref.shape[0]
    state = (m_scr, l_scr, acc_scr)
    pipe = (s_scr, mx_scr)
    head = _head_of_lane(GROUP_W)
    c_q = HEAD_DIM ** -0.5 * LOG2E
    half = ATT_COLS // 2
    groups = [(0, 0, half), (PAIR_W, half, ATT_COLS)]

    def prep(j, carry):
        rows = pl.ds(pl.multiple_of(j * ATT_T, ATT_T), ATT_T)
        q = (q_ref[rows, :] * c_q).astype(BF16)
        for h in range(HEADS):
            qm_scr[h, rows, :] = jnp.where(head == h, q, jnp.zeros_like(q))
        k_scr[rows, :] = k_ref[rows, :].astype(BF16)
        _store_transposed_values(v_ref[rows, :], vt_scr, j)
        return carry

    lax.fori_loop(0, seq // ATT_T, prep, 0)

    def q_block(i, carry):
        q_rows = pl.ds(pl.multiple_of(i * ATT_T, ATT_T), ATT_T)
        qm = jnp.concatenate([qm_scr[h, q_rows, :] for h in range(HEADS)], axis=0)
        _flash_attend(
            i + 1, qm,
            lambda j: k_scr[pl.ds(pl.multiple_of(j * ATT_T, ATT_T), ATT_T), :],
            lambda j: vt_scr.at[j],
            lambda j: bias_ref[jnp.minimum(i - j, N_BIAS_TILES - 1)],
            state, pipe, groups)
        o_t = _flash_result(state)
        for p in range(2):
            lo = 2 * p * ATT_T
            o_ref[q_rows, p * PAIR_W:(p + 1) * PAIR_W] = _merge_pair(o_t[lo:lo + ATT_T], o_t[lo + ATT_T:lo + 2 * ATT_T])
        return carry

    lax.fori_loop(0, seq // ATT_T, q_block, 0)


def _swa_bias_tiles(seq):
    assert DILATED_PATTERNS[-1][0] >= seq - 1, "far tiles assume the widest window covers the sequence"
    assert all(w <= (N_BIAS_TILES - 2) * ATT_T for w, _ in DILATED_PATTERNS[:-1]), "narrow windows must end before the far tile"
    key = np.arange(ATT_T)[:, None]
    qry = np.arange(ATT_T)[None, :]
    tiles = []
    for dist in range(N_BIAS_TILES):
        d = dist * ATT_T + qry - key
        mult = sum(((d >= 0) & (d % dil == 0) & (d <= window)).astype(np.float32) for window, dil in DILATED_PATTERNS)
        with np.errstate(divide="ignore"):
            tiles.append(np.log2(mult))
    return jnp.asarray(np.stack(tiles), F32)


def _swa(q, k, v, bias):
    b, s, w = q.shape
    spec = pl.BlockSpec((None, s, w), lambda i: (i, 0, 0))
    return pl.pallas_call(
        _swa_kernel,
        grid=(b,),
        in_specs=[spec, spec, spec, _const_spec(bias.shape)],
        out_specs=spec,
        out_shape=jax.ShapeDtypeStruct((b, s, w), F32),
        scratch_shapes=([pltpu.VMEM((HEADS, s, w), BF16), pltpu.VMEM((s, w), BF16),
                         pltpu.VMEM((s // ATT_T, w, ATT_T), BF16)] + _flash_state_scratch()
                        + _flash_pipe_scratch()),
        compiler_params=pltpu.CompilerParams(dimension_semantics=("arbitrary",),
                                             vmem_limit_bytes=MIXER_VMEM_LIMIT),
        name="swa",
    )(q, k, v, bias)


def _softplus(x):
    return jnp.maximum(x, 0.0) + jnp.log1p(jnp.exp(-jnp.abs(x)))


def _gelu_tanh(x):
    c = np.sqrt(2.0 / np.pi).astype(np.float32)
    return 0.5 * x * (1.0 + jnp.tanh(c * (x + 0.044715 * (x * x * x))))


def _rglru_kernel(z_ref, cw_ref, cb_ref, wa_ref, ba_ref, wx_ref, bx_ref, lam_ref, o_ref, a_scr, h_scr):
    seq = z_ref.shape[0]
    h_scr[0:V7X_SUBLANES, :] = jnp.zeros((V7X_SUBLANES, GROUP_W), F32)
    h_scr[V7X_SUBLANES:V7X_SUBLANES + seq, :] = z_ref[:, 0:GROUP_W]
    xc = cb_ref[...]
    for j in range(CONV_W):
        start = V7X_SUBLANES - (CONV_W - 1 - j)
        xc = xc + cw_ref[j:j + 1, :] * h_scr[start:start + seq, :]
    xcb = xc.astype(BF16)
    r = jax.nn.sigmoid(_dot(xcb, wa_ref[...]) + ba_ref[...])
    i = jax.nn.sigmoid(_dot(xcb, wx_ref[...]) + bx_ref[...])
    log_a = -LRU_C * r * _softplus(-lam_ref[...])
    a = jnp.exp(log_a)
    gain2 = -jnp.tanh(log_a) * (a * a + 1.0)
    gain = jnp.where(gain2 > 0.0, gain2 * lax.rsqrt(gain2), 0.0)
    u = gain * (i * xc)
    groups = (seq // V7X_SUBLANES, V7X_SUBLANES, GROUP_W)
    a, u = a.reshape(groups), u.reshape(groups)
    in_group = lax.broadcasted_iota(jnp.int32, groups, 1)
    d = 1
    while d < V7X_SUBLANES:
        keep = in_group >= d
        u = a * jnp.where(keep, pltpu.roll(u, d, axis=1), 0.0) + u
        a = a * jnp.where(keep, pltpu.roll(a, d, axis=1), 1.0)
        d *= 2
    a_scr[...] = a.reshape(seq, GROUP_W)
    h_scr[0:seq, :] = u.reshape(seq, GROUP_W)

    def group(g, h_prev):
        rows = pl.ds(pl.multiple_of(g * V7X_SUBLANES, V7X_SUBLANES), V7X_SUBLANES)
        h = h_scr[rows, :] + a_scr[rows, :] * h_prev
        h_scr[rows, :] = h
        return jnp.broadcast_to(h[V7X_SUBLANES - 1:, :], h.shape)

    lax.fori_loop(0, seq // V7X_SUBLANES, group, jnp.zeros((V7X_SUBLANES, GROUP_W), F32), unroll=8)
    o_ref[...] = h_scr[0:seq, :] * _gelu_tanh(z_ref[:, GROUP_W:])


def _rglru(z, cw, cb, wa, ba, wx, bx, lam):
    b, s, _ = z.shape
    consts = (cw, cb, wa, ba, wx, bx, lam)
    return pl.pallas_call(
        _rglru_kernel,
        grid=(b,),
        in_specs=[pl.BlockSpec((None, s, Z_LRU), lambda i: (i, 0, 0))] + [_const_spec(c.shape) for c in consts],
        out_specs=pl.BlockSpec((None, s, GROUP_W), lambda i: (i, 0, 0)),
        out_shape=jax.ShapeDtypeStruct((b, s, GROUP_W), F32),
        scratch_shapes=[pltpu.VMEM((s, GROUP_W), F32), pltpu.VMEM((s + V7X_SUBLANES, GROUP_W), F32)],
        compiler_params=pltpu.CompilerParams(dimension_semantics=("arbitrary",),
                                             vmem_limit_bytes=MIXER_VMEM_LIMIT),
        name="rglru",
    )(z, *consts)


def _dot_exact(a, b):
    return jnp.dot(a, b, precision=lax.Precision.HIGHEST, preferred_element_type=F32)


def _per_head_rows(rows, height):
    return jnp.concatenate([jnp.broadcast_to(rows[h:h + 1, :], (height, rows.shape[1])) for h in range(HEADS)], axis=0)


def _lane_constant_to_width(x, width):
    w = x.shape[1]
    return x[:, :width] if w >= width else jnp.concatenate([x] * (width // w), axis=1)


def _mlstm_kernel(z_ref, g_ref, gb_ref, o_ref, ct_scr, n_scr, m_scr):
    seq = z_ref.shape[0]
    lc = ML_CHUNK
    sub = lax.broadcasted_iota(jnp.int32, (lc, lc), 0)
    lan = lax.broadcasted_iota(jnp.int32, (lc, lc), 1)
    tri_incl = jnp.where(sub <= lan, 1.0, 0.0)
    visible = sub <= lan
    head = _head_of_lane(GROUP_W)
    same_head = lax.broadcasted_iota(jnp.int32, (GROUP_W, GROUP_W), 0) // HEAD_DIM == head
    n_rows = lax.broadcasted_iota(jnp.int32, (V7X_SUBLANES, GROUP_W), 0) == head
    pad_rows = jnp.zeros((lc - V7X_SUBLANES, lc), F32)
    ct_scr[...] = jnp.zeros_like(ct_scr)
    n_scr[...] = jnp.zeros_like(n_scr)
    m_scr[...] = jnp.zeros_like(m_scr)

    def chunk(ci, carry):
        rows = pl.ds(pl.multiple_of(ci * lc, lc), lc)
        q = z_ref[rows, 0:GROUP_W].astype(BF16)
        k = (z_ref[rows, GROUP_W:2 * GROUP_W] * (HEAD_DIM ** -0.5)).astype(BF16)
        v_t = jnp.transpose(z_ref[rows, 2 * GROUP_W:3 * GROUP_W])
        og = z_ref[rows, 3 * GROUP_W:4 * GROUP_W]
        pre = jnp.transpose(g_ref[rows, :])[0:V7X_SUBLANES, :] + gb_ref[...]
        log_f = jnp.minimum(pre, 0.0) - jnp.log1p(jnp.exp(-jnp.abs(pre)))
        b_row = pltpu.roll(_dot_exact(log_f, tri_incl), HEADS, axis=0)
        key_row = pre - b_row
        key_col = jnp.transpose(jnp.concatenate([key_row, pad_rows], axis=0))
        m_prev = m_scr[...]
        ct_prev = ct_scr[...]
        n_prev = n_scr[...]
        inter_all = b_row + m_prev

        qs = jnp.concatenate([jnp.where(head == h, q, jnp.zeros_like(q)) for h in range(HEADS)], axis=0)
        s_all = _dot_nt(k, qs)
        qc_t = _dot_nt(ct_prev.astype(BF16), q)
        qn = _dot_nt(n_prev.astype(BF16), q)
        outs = []
        for h in range(HEADS):
            b_t = b_row[h:h + 1, :]
            dmat = jnp.where(visible, b_t + key_col[:, h:h + 1], NEG_INF)
            inter = inter_all[h:h + 1, :]
            mt = jnp.maximum(jnp.max(dmat, axis=0, keepdims=True), inter)
            w_t = s_all[:, h * lc:(h + 1) * lc] * jnp.exp(dmat - mt)
            w_inter = jnp.exp(inter - mt)
            lanes = slice(h * HEAD_DIM, (h + 1) * HEAD_DIM)
            num = _dot(v_t[lanes, :].astype(BF16), w_t.astype(BF16)) + w_inter * qc_t[lanes, :]
            den = jnp.sum(w_t, axis=0, keepdims=True) + w_inter * qn[h:h + 1, :]
            outs.append(num * (1.0 / jnp.maximum(jnp.abs(den), jnp.exp(-mt))))
        o_ref[rows, :] = jax.nn.sigmoid(og) * jnp.transpose(jnp.concatenate(outs, axis=0))

        g_all = jnp.broadcast_to(b_row[:, lc - 1:lc], b_row.shape)
        wk = g_all + key_row
        m_new = jnp.maximum(g_all + m_prev, jnp.max(wk, axis=1, keepdims=True))
        decay = jnp.exp(g_all + m_prev - m_new)
        w_row = jnp.exp(wk - m_new)
        vw_t = (v_t * _per_head_rows(w_row, HEAD_DIM)).astype(BF16)
        decay_w = _lane_constant_to_width(decay, GROUP_W)
        ct_scr[...] = _per_head_rows(decay_w, HEAD_DIM) * ct_prev + jnp.where(same_head, _dot(vw_t, k), 0.0)
        n_scr[...] = decay_w * n_prev + jnp.where(n_rows, _dot(w_row.astype(BF16), k), 0.0)
        m_scr[...] = m_new
        return carry

    lax.fori_loop(0, seq // lc, chunk, 0, unroll=4)


def _mlstm(z, gates, gate_bias):
    b, s, _ = z.shape
    return pl.pallas_call(
        _mlstm_kernel,
        grid=(b,),
        in_specs=[pl.BlockSpec((None, s, Z_ML), lambda i: (i, 0, 0)),
                  pl.BlockSpec((None, s, Z_G), lambda i: (i, 0, 0)),
                  _const_spec(gate_bias.shape)],
        out_specs=pl.BlockSpec((None, s, GROUP_W), lambda i: (i, 0, 0)),
        out_shape=jax.ShapeDtypeStruct((b, s, GROUP_W), F32),
        scratch_shapes=[pltpu.VMEM((GROUP_W, GROUP_W), F32), pltpu.VMEM((V7X_SUBLANES, GROUP_W), F32),
                        pltpu.VMEM((V7X_SUBLANES, ML_CHUNK), F32)],
        compiler_params=pltpu.CompilerParams(dimension_semantics=("arbitrary",),
                                             vmem_limit_bytes=MIXER_VMEM_LIMIT),
        name="mlstm",
    )(z, gates, gate_bias)


def _block_diag(w):
    out = jnp.zeros((GROUP_W, GROUP_W), w.dtype)
    for h in range(HEADS):
        out = out.at[h * HEAD_DIM:(h + 1) * HEAD_DIM, h * HEAD_DIM:(h + 1) * HEAD_DIM].set(w[h])
    return out


def _permute_w_in(w):
    w = w.astype(BF16)
    kr = Q_LORA + KV_LORA
    zeros = lambda width: jnp.zeros(w.shape[:2] + (width,), w.dtype)
    gap = zeros(HEAD_DIM - ROPE_HALF)
    body_lo = kr + MLA_ROPE
    body_hi = body_lo + 9 * GROUP_W
    pad = zeros(Z_G - 2 * HEADS)
    return jnp.concatenate([w[..., :kr], w[..., kr:kr + ROPE_HALF], gap, w[..., kr + ROPE_HALF:kr + MLA_ROPE], gap,
                            w[..., body_lo:body_hi], w[..., body_hi:], pad], axis=-1)


def _mla_up_layout():
    per_q = MLA_NOPE + MLA_ROPE
    q_src, q_dst = [], []
    for h in range(HEADS):
        base = h * GROUP_W
        nope_lo = base + (h % 2) * HEAD_DIM
        q_src += [h * per_q + i for i in range(per_q)]
        q_dst += [nope_lo + i for i in range(MLA_NOPE)]
        q_dst += [base + V7X_LANES + i for i in range(ROPE_HALF)]
        q_dst += [base + V7X_LANES + HEAD_DIM + i for i in range(ROPE_HALF)]
    per_kv = MLA_NOPE + HEAD_DIM
    kn = [h * per_kv + i for h in range(HEADS) for i in range(MLA_NOPE)]
    vv = [h * per_kv + MLA_NOPE + i for h in range(HEADS) for i in range(HEAD_DIM)]
    return np.array(q_src), np.array(q_dst), np.array(kn + vv)


def _rope_tables(seq):
    freqs = ROPE_THETA ** (-jnp.arange(ROPE_HALF, dtype=F32) / ROPE_HALF)
    ang = jnp.arange(seq, dtype=F32)[:, None] * freqs[None, :]
    cos = jnp.tile(jnp.cos(ang), (1, 2 * HEADS))
    sin = jnp.tile(jnp.sin(ang), (1, HEADS))
    return cos, jnp.concatenate([-sin, sin], axis=1)


def kernel(x, ln_g, ln_b, ffn_w1, ffn_w3, ffn_w2, w_in, mla_q_norm, mla_kv_norm, mla_w_uq, mla_w_ukv,
           lru_conv_w, lru_conv_b, lru_w_a, lru_b_a, lru_w_x, lru_b_x, lru_lambda, ml_b_i, ml_b_f,
           out_norm, w_out):
    bsz, seq, d = x.shape
    n = bsz * seq
    row = lambda t: t.reshape(1, -1)
    q_src, q_dst, kv_perm = _mla_up_layout()
    cos, sin = _rope_tables(seq)
    swa_bias = _swa_bias_tiles(seq)
    xt = x.reshape(n, d)
    ffn_w = (ffn_w1.astype(BF16), ffn_w3.astype(BF16), ffn_w2.astype(BF16))
    w_in_p = _permute_w_in(w_in)
    for l in range(DEPTH):
        x1, za, zlru, sq, sk, sv, zml, zg = _ffn_inproj(
            xt, *ffn_w, (l, 0), row(ln_g[l, 0]), row(ln_b[l, 0]), w_in_p)
        seqv = lambda t: t.reshape(bsz, seq, t.shape[-1])
        wuq = jnp.zeros((Q_LORA, HEADS * GROUP_W), F32).at[:, q_dst].set(mla_w_uq[l][:, q_src]).astype(BF16)
        y_a = _mla(seqv(za), row(mla_q_norm[l]), row(mla_kv_norm[l]), wuq,
                   mla_w_ukv[l][:, kv_perm].astype(BF16), cos, sin)
        y_b = _rglru(seqv(zlru), lru_conv_w[l], row(lru_conv_b[l]),
                     _block_diag(lru_w_a[l]).astype(BF16), row(lru_b_a[l]),
                     _block_diag(lru_w_x[l]).astype(BF16), row(lru_b_x[l]), row(lru_lambda[l]))
        y_c = _swa(seqv(sq), seqv(sk), seqv(sv), swa_bias)
        gate_bias = jnp.broadcast_to(jnp.concatenate([ml_b_i[l], ml_b_f[l]])[:, None], (2 * HEADS, ML_CHUNK))
        y_d = _mlstm(seqv(zml), seqv(zg), gate_bias)
        ys = [t.reshape(n, GROUP_W) for t in (y_a, y_b, y_c, y_d)]
        xt = _outproj_ffn(x1, ys, row(out_norm[l]), w_out[l].astype(BF16), row(ln_g[l, 1]), row(ln_b[l, 1]),
                          *ffn_w, (l, 1), row(ln_g[l, 2]), row(ln_b[l, 2]))
    return xt.reshape(bsz, seq, d)
```

```python
import numpy as np
import jax
import jax.numpy as jnp
from jax import lax
from jax.experimental import pallas as pl
from jax.experimental.pallas import tpu as pltpu

F32 = jnp.float32
BF16 = jnp.bfloat16

D_MODEL = 1024
DEPTH = 2
GROUP_W = 256
HEAD_DIM = 64
HEADS = 4
D_FF = 2816
ALPHA = (2.0 * DEPTH) ** 0.25
Q_LORA = 256
KV_LORA = 128
MLA_NOPE = 64
MLA_ROPE = 32
ROPE_HALF = MLA_ROPE // 2
ROPE_THETA = 10000.0
CONV_W = 4
LRU_C = 8.0
DILATED_PATTERNS = ((128, 1), (512, 4), (2048, 16))

V7X_LANES = 128
V7X_SUBLANES = 8
V7X_VMEM_BYTES = 64 * 1024 * 1024

FF_CHUNK = 256
N_FF_CHUNKS = D_FF // FF_CHUNK
TOKEN_TILE = 512
ML_CHUNK = 256
ATT_T = 256
ATT_COLS = 1024
MLA_TQ = 512
PAIR_W = 2 * HEAD_DIM
N_BIAS_TILES = 4

Z_A = 512
Z_LRU = 512
Z_ML = 1024
Z_G = 128

NEG_INF = float("-inf")
LOG2E = float(np.log2(np.e))


MIB = 1024 * 1024
FFN_VMEM_LIMIT = 56 * MIB
MIXER_VMEM_LIMIT = 48 * MIB
assert max(FFN_VMEM_LIMIT, MIXER_VMEM_LIMIT) < V7X_VMEM_BYTES


def _const_spec(shape):
    nd = len(shape)
    return pl.BlockSpec(shape, lambda *_: (0,) * nd, pipeline_mode=pl.Buffered(1))


def _stacked_spec(shape, lead):
    tail = shape[len(lead):]
    return pl.BlockSpec((None,) * len(lead) + tuple(tail), lambda *_: tuple(lead) + (0,) * len(tail),
                        pipeline_mode=pl.Buffered(1))


def _layernorm(v, g, b):
    mu = jnp.mean(v, axis=-1, keepdims=True)
    d = v - mu
    var = jnp.mean(d * d, axis=-1, keepdims=True)
    return d * lax.rsqrt(var + 1e-5) * g + b


def _rmsnorm(v, g, eps=1e-6):
    return v * lax.rsqrt(jnp.mean(v * v, axis=-1, keepdims=True) + eps) * g


def _dot(a, b):
    return jnp.dot(a, b, preferred_element_type=F32)


def _dot_nt(a, b):
    return lax.dot_general(a, b, (((1,), (1,)), ((), ())), preferred_element_type=F32)


def _head_of_lane(width, group=HEAD_DIM):
    return lax.broadcasted_iota(jnp.int32, (1, width), 1) // group


def _ffn_ln(x, w1_ref, w3_ref, w2_ref, acc_ref, g, b):
    xb = x.astype(BF16)
    for c in range(N_FF_CHUNKS):
        cols = slice(c * FF_CHUNK, (c + 1) * FF_CHUNK)
        h1 = _dot(xb, w1_ref[:, cols])
        h3 = _dot(xb, w3_ref[:, cols])
        h = (h1 * jax.nn.sigmoid(h1)) * h3
        part = _dot(h.astype(BF16), w2_ref[cols, :])
        if c == 0:
            acc_ref[...] = part
        else:
            acc_ref[...] += part
    return _layernorm(ALPHA * x + 0.5 * acc_ref[...], g, b)


def _ffn_inproj_kernel(x_ref, w1_ref, w3_ref, w2_ref, g_ref, b_ref, win_ref,
                       x1_ref, za_ref, zlru_ref, sq_ref, sk_ref, sv_ref, zml_ref, zg_ref, acc_ref):
    x1 = _ffn_ln(x_ref[...], w1_ref, w3_ref, w2_ref, acc_ref, g_ref[...], b_ref[...])
    x1_ref[...] = x1
    xb = x1.astype(BF16)
    off = 0
    for ref in (za_ref, zlru_ref, sq_ref, sk_ref, sv_ref, zml_ref, zg_ref):
        w = ref.shape[-1]
        ref[...] = _dot(xb, win_ref[:, off:off + w])
        off += w


def _ffn_inproj(x, w1, w3, w2, lj, g, b, win):
    n = x.shape[0]
    tm = TOKEN_TILE
    tile = lambda w: pl.BlockSpec((tm, w), lambda i: (i, 0))
    widths = (D_MODEL, Z_A, Z_LRU, GROUP_W, GROUP_W, GROUP_W, Z_ML, Z_G)
    return pl.pallas_call(
        _ffn_inproj_kernel,
        grid=(n // tm,),
        in_specs=[tile(D_MODEL), _stacked_spec(w1.shape, lj), _stacked_spec(w3.shape, lj), _stacked_spec(w2.shape, lj),
                  _const_spec(g.shape), _const_spec(b.shape), _stacked_spec(win.shape, lj[:1])],
        out_specs=[tile(w) for w in widths],
        out_shape=[jax.ShapeDtypeStruct((n, w), F32) for w in widths],
        scratch_shapes=[pltpu.VMEM((tm, D_MODEL), F32)],
        compiler_params=pltpu.CompilerParams(dimension_semantics=("arbitrary",),
                                             vmem_limit_bytes=FFN_VMEM_LIMIT),
        name="ffn_inproj",
    )(x, w1, w3, w2, g, b, win)


def _outproj_ffn_kernel(x_ref, ya_ref, yb_ref, yc_ref, yd_ref,
                        on_ref, wout_ref, g1_ref, b1_ref, w1_ref, w3_ref, w2_ref, g2_ref, b2_ref,
                        out_ref, acc_ref):
    y = jnp.zeros(x_ref.shape, F32)
    for gi, y_ref in enumerate((ya_ref, yb_ref, yc_ref, yd_ref)):
        lo = gi * GROUP_W
        yn = _rmsnorm(y_ref[...], on_ref[:, lo:lo + GROUP_W])
        y = y + _dot(yn.astype(BF16), wout_ref[lo:lo + GROUP_W, :])
    x2 = _layernorm(ALPHA * x_ref[...] + y, g1_ref[...], b1_ref[...])
    out_ref[...] = _ffn_ln(x2, w1_ref, w3_ref, w2_ref, acc_ref, g2_ref[...], b2_ref[...])


def _outproj_ffn(x, ys, on, wout, g1, b1, w1, w3, w2, lj, g2, b2):
    n = x.shape[0]
    tm = TOKEN_TILE
    tile = lambda w: pl.BlockSpec((tm, w), lambda i: (i, 0))
    consts = (on, wout, g1, b1, w1, w3, w2, g2, b2)
    const_specs = [_stacked_spec(c.shape, lj) if c.ndim == 4 else _const_spec(c.shape) for c in consts]
    return pl.pallas_call(
        _outproj_ffn_kernel,
        grid=(n // tm,),
        in_specs=[tile(D_MODEL)] + [tile(GROUP_W)] * len(ys) + const_specs,
        out_specs=tile(D_MODEL),
        out_shape=jax.ShapeDtypeStruct((n, D_MODEL), F32),
        scratch_shapes=[pltpu.VMEM((tm, D_MODEL), F32)],
        compiler_params=pltpu.CompilerParams(dimension_semantics=("arbitrary",),
                                             vmem_limit_bytes=FFN_VMEM_LIMIT),
        name="outproj_ffn",
    )(x, *ys, *consts)


def _flash_state_scratch():
    return [pltpu.VMEM((1, ATT_COLS), F32), pltpu.VMEM((1, ATT_COLS), F32), pltpu.VMEM((PAIR_W, ATT_COLS), F32)]


def _flash_reset(state):
    m_ref, l_ref, acc_ref = state
    m_ref[...] = jnp.full(m_ref.shape, NEG_INF, F32)
    l_ref[...] = jnp.zeros_like(l_ref)
    acc_ref[...] = jnp.zeros_like(acc_ref)


def _flash_pipe_scratch():
    return [pltpu.VMEM((2, ATT_T, ATT_COLS), F32), pltpu.VMEM((2, 1, ATT_COLS), F32)]


def _flash_attend(n_steps, qm, key_block, value_block, bias_block, state, pipe, groups):
    m_ref, l_ref, acc_ref = state
    s_ref, mx_ref = pipe
    _flash_reset(state)

    def scores(j, slot):
        bias = bias_block(j)
        s_t = _dot_nt(key_block(j), qm) + jnp.concatenate([bias] * (ATT_COLS // bias.shape[1]), axis=1)
        s_ref[slot] = s_t
        mx_ref[slot] = jnp.max(s_t, axis=0, keepdims=True)

    scores(0, 0)

    def consume(j, with_next):
        cur = lax.rem(j, 2)
        m = m_ref[...]
        m_new = jnp.maximum(m, mx_ref[cur])
        a = jnp.exp2(m - m_new)
        pr = jnp.exp2(s_ref[cur] - m_new)
        m_ref[...] = m_new
        l_ref[...] = a * l_ref[...] + jnp.sum(pr, axis=0, keepdims=True)
        prb = pr.astype(BF16)
        if with_next:
            scores(j + 1, 1 - cur)
        vt_blk = value_block(j)
        for v_lo, c_lo, c_hi in groups:
            acc_ref[:, c_lo:c_hi] = (a[:, c_lo:c_hi] * acc_ref[:, c_lo:c_hi]
                                     + _dot(vt_blk[v_lo:v_lo + PAIR_W, :], prb[:, c_lo:c_hi]))

    def step(j, carry):
        consume(j, True)
        return carry

    lax.fori_loop(0, n_steps - 1, step, 0)
    consume(n_steps - 1, False)


def _flash_result(state):
    _, l_ref, acc_ref = state
    return jnp.transpose(acc_ref[...] * (1.0 / l_ref[...]))


def _merge_pair(o_a, o_b):
    lane = lax.broadcasted_iota(jnp.int32, (1, PAIR_W), 1)
    return jnp.where(lane < HEAD_DIM, o_a, o_b)


def _store_transposed_values(v, vt_ref, j):
    vt_ref[j] = jnp.transpose(v).astype(BF16)


def _mla_kernel(za_ref, qn_ref, kvn_ref, wuq_ref, wukv_ref, cos_ref, sin_ref, o_ref,
                qm_scr, kp_scr, vt_scr, o_scr, bias_scr, m_scr, l_scr, acc_scr, s_scr, mx_scr):
    seq = za_ref.shape[0]
    state = (m_scr, l_scr, acc_scr)
    pipe = (s_scr, mx_scr)
    c_q = (MLA_NOPE + MLA_ROPE) ** -0.5 * LOG2E

    def prep(j, carry):
        rows = pl.ds(pl.multiple_of(j * ATT_T, ATT_T), ATT_T)
        cos, sin = cos_ref[rows, :], sin_ref[rows, :]

        def rope(r):
            return r * cos + pltpu.roll(r, V7X_LANES // 2, axis=1) * sin

        cq = _rmsnorm(za_ref[rows, 0:Q_LORA], qn_ref[...])
        q = _dot(cq.astype(BF16), wuq_ref[...]) * c_q
        for h in range(HEADS):
            lo = h * GROUP_W
            qm_scr[h, rows, 0:V7X_LANES] = q[:, lo:lo + V7X_LANES].astype(BF16)
            qm_scr[h, rows, V7X_LANES:] = rope(q[:, lo + V7X_LANES:lo + GROUP_W]).astype(BF16)
        ckv = _rmsnorm(za_ref[rows, Q_LORA:Q_LORA + KV_LORA], kvn_ref[...])
        kv = _dot(ckv.astype(BF16), wukv_ref[...])
        k_rope = rope(za_ref[rows, Q_LORA + KV_LORA:]).astype(BF16)
        for p in range(2):
            kp_scr[p, rows, 0:V7X_LANES] = kv[:, p * PAIR_W:(p + 1) * PAIR_W].astype(BF16)
            kp_scr[p, rows, V7X_LANES:] = k_rope
        _store_transposed_values(kv[:, GROUP_W:], vt_scr, j)
        return carry

    lax.fori_loop(0, seq // ATT_T, prep, 0)

    key = lax.broadcasted_iota(jnp.int32, (ATT_T, MLA_TQ), 0)
    qry = lax.broadcasted_iota(jnp.int32, (ATT_T, MLA_TQ), 1)
    n_diag = MLA_TQ // ATT_T
    bias_scr[0] = jnp.zeros((ATT_T, MLA_TQ), F32)
    for d in range(n_diag):
        bias_scr[1 + d] = jnp.where(key + d * ATT_T <= qry, 0.0, NEG_INF)
    groups = [(0, 0, ATT_COLS)]

    def q_block(i, carry):
        q0 = pl.multiple_of(i * MLA_TQ, MLA_TQ)
        q_rows = pl.ds(q0, MLA_TQ)
        first_diag = i * n_diag

        def pair(p, c):
            qm = jnp.concatenate([qm_scr[2 * p, q_rows, :], qm_scr[2 * p + 1, q_rows, :]], axis=0)
            v_lo = pl.multiple_of(p * PAIR_W, PAIR_W)
            _flash_attend(
                first_diag + n_diag, qm,
                lambda j: kp_scr[p, pl.ds(pl.multiple_of(j * ATT_T, ATT_T), ATT_T), :],
                lambda j: vt_scr.at[j, pl.ds(v_lo, PAIR_W), :],
                lambda j: bias_scr[jnp.maximum(j - first_diag + 1, 0)],
                state, pipe, groups)
            o_t = _flash_result(state)
            o_scr[p] = _merge_pair(o_t[:MLA_TQ], o_t[MLA_TQ:])
            return c

        lax.fori_loop(0, 2, pair, 0)
        for p in range(2):
            o_ref[q_rows, p * PAIR_W:(p + 1) * PAIR_W] = o_scr[p]
        return carry

    lax.fori_loop(0, seq // MLA_TQ, q_block, 0)


def _mla(za, qn, kvn, wuq, wukv, cos, sin):
    b, s, _ = za.shape
    consts = (qn, kvn, wuq, wukv, cos, sin)
    return pl.pallas_call(
        _mla_kernel,
        grid=(b,),
        in_specs=[pl.BlockSpec((None, s, Z_A), lambda i: (i, 0, 0))] + [_const_spec(c.shape) for c in consts],
        out_specs=pl.BlockSpec((None, s, GROUP_W), lambda i: (i, 0, 0)),
        out_shape=jax.ShapeDtypeStruct((b, s, GROUP_W), F32),
        scratch_shapes=([pltpu.VMEM((HEADS, s, GROUP_W), BF16), pltpu.VMEM((2, s, GROUP_W), BF16),
                         pltpu.VMEM((s // ATT_T, GROUP_W, ATT_T), BF16), pltpu.VMEM((2, MLA_TQ, PAIR_W), F32),
                         pltpu.VMEM((1 + MLA_TQ // ATT_T, ATT_T, MLA_TQ), F32)]
                        + _flash_state_scratch() + _flash_pipe_scratch()),
        compiler_params=pltpu.CompilerParams(dimension_semantics=("arbitrary",),
                                             vmem_limit_bytes=MIXER_VMEM_LIMIT),
        name="mla",
    )(za, *consts)


def _swa_kernel(q_ref, k_ref, v_ref, bias_ref, o_ref, qm_scr, k_scr, vt_scr, m_scr, l_scr, acc_scr,
                s_scr, mx_scr):
    seq = q_ref.shape[0]
    state = (m_scr, l_scr, acc_scr)
    pipe = (s_scr, mx_scr)
    head = _head_of_lane(GROUP_W)
    c_q = HEAD_DIM ** -0.5 * LOG2E
    half = ATT_COLS // 2
    groups = [(0, 0, half), (PAIR_W, half, ATT_COLS)]

    def prep(j, carry):
        rows = pl.ds(pl.multiple_of(j * ATT_T, ATT_T), ATT_T)
        q = (q_ref[rows, :] * c_q).astype(BF16)
        for h in range(HEADS):
            qm_scr[h, rows, :] = jnp.where(head == h, q, jnp.zeros_like(q))
        k_scr[rows, :] = k_ref[rows, :].astype(BF16)
        _store_transposed_values(v_ref[rows, :], vt_scr, j)
        return carry

    lax.fori_loop(0, seq // ATT_T, prep, 0)

    def q_block(i, carry):
        q_rows = pl.ds(pl.multiple_of(i * ATT_T, ATT_T), ATT_T)
        qm = jnp.concatenate([qm_scr[h, q_rows, :] for h in range(HEADS)], axis=0)
        _flash_attend(
            i + 1, qm,
            lambda j: k_scr[pl.ds(pl.multiple_of(j * ATT_T, ATT_T), ATT_T), :],
            lambda j: vt_scr.at[j],
            lambda j: bias_ref[jnp.minimum(i - j, N_BIAS_TILES - 1)],
            state, pipe, groups)
        o_t = _flash_result(state)
        for p in range(2):
            lo = 2 * p * ATT_T
            o_ref[q_rows, p * PAIR_W:(p + 1) * PAIR_W] = _merge_pair(o_t[lo:lo + ATT_T], o_t[lo + ATT_T:lo + 2 * ATT_T])
        return carry

    lax.fori_loop(0, seq // ATT_T, q_block, 0)


def _swa_bias_tiles(seq):
    assert DILATED_PATTERNS[-1][0] >= seq - 1, "far tiles assume the widest window covers the sequence"
    assert all(w <= (N_BIAS_TILES - 2) * ATT_T for w, _ in DILATED_PATTERNS[:-1]), "narrow windows must end before the far tile"
    key = np.arange(ATT_T)[:, None]
    qry = np.arange(ATT_T)[None, :]
    tiles = []
    for dist in range(N_BIAS_TILES):
        d = dist * ATT_T + qry - key
        mult = sum(((d >= 0) & (d % dil == 0) & (d <= window)).astype(np.float32) for window, dil in DILATED_PATTERNS)
        with np.errstate(divide="ignore"):
            tiles.append(np.log2(mult))
    return jnp.asarray(np.stack(tiles), F32)


def _swa(q, k, v, bias):
    b, s, w = q.shape
    spec = pl.BlockSpec((None, s, w), lambda i: (i, 0, 0))
    return pl.pallas_call(
        _swa_kernel,
        grid=(b,),
        in_specs=[spec, spec, spec, _const_spec(bias.shape)],
        out_specs=spec,
        out_shape=jax.ShapeDtypeStruct((b, s, w), F32),
        scratch_shapes=([pltpu.VMEM((HEADS, s, w), BF16), pltpu.VMEM((s, w), BF16),
                         pltpu.VMEM((s // ATT_T, w, ATT_T), BF16)] + _flash_state_scratch()
                        + _flash_pipe_scratch()),
        compiler_params=pltpu.CompilerParams(dimension_semantics=("arbitrary",),
                                             vmem_limit_bytes=MIXER_VMEM_LIMIT),
        name="swa",
    )(q, k, v, bias)


def _softplus(x):
    return jnp.maximum(x, 0.0) + jnp.log1p(jnp.exp(-jnp.abs(x)))


def _gelu_tanh(x):
    c = np.sqrt(2.0 / np.pi).astype(np.float32)
    return 0.5 * x * (1.0 + jnp.tanh(c * (x + 0.044715 * (x * x * x))))


def _rglru_kernel(z_ref, cw_ref, cb_ref, wa_ref, ba_ref, wx_ref, bx_ref, lam_ref, o_ref, a_scr, h_scr):
    seq = z_ref.shape[0]
    h_scr[0:V7X_SUBLANES, :] = jnp.zeros((V7X_SUBLANES, GROUP_W), F32)
    h_scr[V7X_SUBLANES:V7X_SUBLANES + seq, :] = z_ref[:, 0:GROUP_W]
    xc = cb_ref[...]
    for j in range(CONV_W):
        start = V7X_SUBLANES - (CONV_W - 1 - j)
        xc = xc + cw_ref[j:j + 1, :] * h_scr[start:start + seq, :]
    xcb = xc.astype(BF16)
    r = jax.nn.sigmoid(_dot(xcb, wa_ref[...]) + ba_ref[...])
    i = jax.nn.sigmoid(_dot(xcb, wx_ref[...]) + bx_ref[...])
    log_a = -LRU_C * r * _softplus(-lam_ref[...])
    a = jnp.exp(log_a)
    gain2 = -jnp.tanh(log_a) * (a * a + 1.0)
    gain = jnp.where(gain2 > 0.0, gain2 * lax.rsqrt(gain2), 0.0)
    u = gain * (i * xc)
    groups = (seq // V7X_SUBLANES, V7X_SUBLANES, GROUP_W)
    a, u = a.reshape(groups), u.reshape(groups)
    in_group = lax.broadcasted_iota(jnp.int32, groups, 1)
    d = 1
    while d < V7X_SUBLANES:
        keep = in_group >= d
        u = a * jnp.where(keep, pltpu.roll(u, d, axis=1), 0.0) + u
        a = a * jnp.where(keep, pltpu.roll(a, d, axis=1), 1.0)
        d *= 2
    a_scr[...] = a.reshape(seq, GROUP_W)
    h_scr[0:seq, :] = u.reshape(seq, GROUP_W)

    def group(g, h_prev):
        rows = pl.ds(pl.multiple_of(g * V7X_SUBLANES, V7X_SUBLANES), V7X_SUBLANES)
        h = h_scr[rows, :] + a_scr[rows, :] * h_prev
        h_scr[rows, :] = h
        return jnp.broadcast_to(h[V7X_SUBLANES - 1:, :], h.shape)

    lax.fori_loop(0, seq // V7X_SUBLANES, group, jnp.zeros((V7X_SUBLANES, GROUP_W), F32), unroll=8)
    o_ref[...] = h_scr[0:seq, :] * _gelu_tanh(z_ref[:, GROUP_W:])


def _rglru(z, cw, cb, wa, ba, wx, bx, lam):
    b, s, _ = z.shape
    consts = (cw, cb, wa, ba, wx, bx, lam)
    return pl.pallas_call(
        _rglru_kernel,
        grid=(b,),
        in_specs=[pl.BlockSpec((None, s, Z_LRU), lambda i: (i, 0, 0))] + [_const_spec(c.shape) for c in consts],
        out_specs=pl.BlockSpec((None, s, GROUP_W), lambda i: (i, 0, 0)),
        out_shape=jax.ShapeDtypeStruct((b, s, GROUP_W), F32),
        scratch_shapes=[pltpu.VMEM((s, GROUP_W), F32), pltpu.VMEM((s + V7X_SUBLANES, GROUP_W), F32)],
        compiler_params=pltpu.CompilerParams(dimension_semantics=("arbitrary",),
                                             vmem_limit_bytes=MIXER_VMEM_LIMIT),
        name="rglru",
    )(z, *consts)


def _dot_exact(a, b):
    return jnp.dot(a, b, precision=lax.Precision.HIGHEST, preferred_element_type=F32)


def _per_head_rows(rows, height):
    return jnp.concatenate([jnp.broadcast_to(rows[h:h + 1, :], (height, rows.shape[1])) for h in range(HEADS)], axis=0)


def _lane_constant_to_width(x, width):
    w = x.shape[1]
    return x[:, :width] if w >= width else jnp.concatenate([x] * (width // w), axis=1)


def _mlstm_kernel(z_ref, g_ref, gb_ref, o_ref, ct_scr, n_scr, m_scr):
    seq = z_ref.shape[0]
    lc = ML_CHUNK
    sub = lax.broadcasted_iota(jnp.int32, (lc, lc), 0)
    lan = lax.broadcasted_iota(jnp.int32, (lc, lc), 1)
    tri_incl = jnp.where(sub <= lan, 1.0, 0.0)
    visible = sub <= lan
    head = _head_of_lane(GROUP_W)
    same_head = lax.broadcasted_iota(jnp.int32, (GROUP_W, GROUP_W), 0) // HEAD_DIM == head
    n_rows = lax.broadcasted_iota(jnp.int32, (V7X_SUBLANES, GROUP_W), 0) == head
    pad_rows = jnp.zeros((lc - V7X_SUBLANES, lc), F32)
    ct_scr[...] = jnp.zeros_like(ct_scr)
    n_scr[...] = jnp.zeros_like(n_scr)
    m_scr[...] = jnp.zeros_like(m_scr)

    def chunk(ci, carry):
        rows = pl.ds(pl.multiple_of(ci * lc, lc), lc)
        q = z_ref[rows, 0:GROUP_W].astype(BF16)
        k = (z_ref[rows, GROUP_W:2 * GROUP_W] * (HEAD_DIM ** -0.5)).astype(BF16)
        v_t = jnp.transpose(z_ref[rows, 2 * GROUP_W:3 * GROUP_W])
        og = z_ref[rows, 3 * GROUP_W:4 * GROUP_W]
        pre = jnp.transpose(g_ref[rows, :])[0:V7X_SUBLANES, :] + gb_ref[...]
        log_f = jnp.minimum(pre, 0.0) - jnp.log1p(jnp.exp(-jnp.abs(pre)))
        b_row = pltpu.roll(_dot_exact(log_f, tri_incl), HEADS, axis=0)
        key_row = pre - b_row
        key_col = jnp.transpose(jnp.concatenate([key_row, pad_rows], axis=0))
        m_prev = m_scr[...]
        ct_prev = ct_scr[...]
        n_prev = n_scr[...]
        inter_all = b_row + m_prev

        qs = jnp.concatenate([jnp.where(head == h, q, jnp.zeros_like(q)) for h in range(HEADS)], axis=0)
        s_all = _dot_nt(k, qs)
        qc_t = _dot_nt(ct_prev.astype(BF16), q)
        qn = _dot_nt(n_prev.astype(BF16), q)
        outs = []
        for h in range(HEADS):
            b_t = b_row[h:h + 1, :]
            dmat = jnp.where(visible, b_t + key_col[:, h:h + 1], NEG_INF)
            inter = inter_all[h:h + 1, :]
            mt = jnp.maximum(jnp.max(dmat, axis=0, keepdims=True), inter)
            w_t = s_all[:, h * lc:(h + 1) * lc] * jnp.exp(dmat - mt)
            w_inter = jnp.exp(inter - mt)
            lanes = slice(h * HEAD_DIM, (h + 1) * HEAD_DIM)
            num = _dot(v_t[lanes, :].astype(BF16), w_t.astype(BF16)) + w_inter * qc_t[lanes, :]
            den = jnp.sum(w_t, axis=0, keepdims=True) + w_inter * qn[h:h + 1, :]
            outs.append(num * (1.0 / jnp.maximum(jnp.abs(den), jnp.exp(-mt))))
        o_ref[rows, :] = jax.nn.sigmoid(og) * jnp.transpose(jnp.concatenate(outs, axis=0))

        g_all = jnp.broadcast_to(b_row[:, lc - 1:lc], b_row.shape)
        wk = g_all + key_row
        m_new = jnp.maximum(g_all + m_prev, jnp.max(wk, axis=1, keepdims=True))
        decay = jnp.exp(g_all + m_prev - m_new)
        w_row = jnp.exp(wk - m_new)
        vw_t = (v_t * _per_head_rows(w_row, HEAD_DIM)).astype(BF16)
        decay_w = _lane_constant_to_width(decay, GROUP_W)
        ct_scr[...] = _per_head_rows(decay_w, HEAD_DIM) * ct_prev + jnp.where(same_head, _dot(vw_t, k), 0.0)
        n_scr[...] = decay_w * n_prev + jnp.where(n_rows, _dot(w_row.astype(BF16), k), 0.0)
        m_scr[...] = m_new
        return carry

    lax.fori_loop(0, seq // lc, chunk, 0, unroll=4)


def _mlstm(z, gates, gate_bias):
    b, s, _ = z.shape
    return pl.pallas_call(
        _mlstm_kernel,
        grid=(b,),
        in_specs=[pl.BlockSpec((None, s, Z_ML), lambda i: (i, 0, 0)),
                  pl.BlockSpec((None, s, Z_G), lambda i: (i, 0, 0)),
                  _const_spec(gate_bias.shape)],
        out_specs=pl.BlockSpec((None, s, GROUP_W), lambda i: (i, 0, 0)),
        out_shape=jax.ShapeDtypeStruct((b, s, GROUP_W), F32),
        scratch_shapes=[pltpu.VMEM((GROUP_W, GROUP_W), F32), pltpu.VMEM((V7X_SUBLANES, GROUP_W), F32),
                        pltpu.VMEM((V7X_SUBLANES, ML_CHUNK), F32)],
        compiler_params=pltpu.CompilerParams(dimension_semantics=("arbitrary",),
                                             vmem_limit_bytes=MIXER_VMEM_LIMIT),
        name="mlstm",
    )(z, gates, gate_bias)


def _block_diag(w):
    out = jnp.zeros((GROUP_W, GROUP_W), w.dtype)
    for h in range(HEADS):
        out = out.at[h * HEAD_DIM:(h + 1) * HEAD_DIM, h * HEAD_DIM:(h + 1) * HEAD_DIM].set(w[h])
    return out


def _permute_w_in(w):
    kr = Q_LORA + KV_LORA
    zeros = lambda width: jnp.zeros(w.shape[:2] + (width,), w.dtype)
    gap = zeros(HEAD_DIM - ROPE_HALF)
    body_lo = kr + MLA_ROPE
    body_hi = body_lo + 9 * GROUP_W
    pad = zeros(Z_G - 2 * HEADS)
    return jnp.concatenate([w[..., :kr], w[..., kr:kr + ROPE_HALF], gap, w[..., kr + ROPE_HALF:kr + MLA_ROPE], gap,
                            w[..., body_lo:body_hi], w[..., body_hi:], pad], axis=-1).astype(BF16)


def _mla_up_layout():
    per_q = MLA_NOPE + MLA_ROPE
    q_src, q_dst = [], []
    for h in range(HEADS):
        base = h * GROUP_W
        nope_lo = base + (h % 2) * HEAD_DIM
        q_src += [h * per_q + i for i in range(per_q)]
        q_dst += [nope_lo + i for i in range(MLA_NOPE)]
        q_dst += [base + V7X_LANES + i for i in range(ROPE_HALF)]
        q_dst += [base + V7X_LANES + HEAD_DIM + i for i in range(ROPE_HALF)]
    per_kv = MLA_NOPE + HEAD_DIM
    kn = [h * per_kv + i for h in range(HEADS) for i in range(MLA_NOPE)]
    vv = [h * per_kv + MLA_NOPE + i for h in range(HEADS) for i in range(HEAD_DIM)]
    return np.array(q_src), np.array(q_dst), np.array(kn + vv)


def _rope_tables(seq):
    freqs = ROPE_THETA ** (-jnp.arange(ROPE_HALF, dtype=F32) / ROPE_HALF)
    ang = jnp.arange(seq, dtype=F32)[:, None] * freqs[None, :]
    cos = jnp.tile(jnp.cos(ang), (1, 2 * HEADS))
    sin = jnp.tile(jnp.sin(ang), (1, HEADS))
    return cos, jnp.concatenate([-sin, sin], axis=1)


def kernel(x, ln_g, ln_b, ffn_w1, ffn_w3, ffn_w2, w_in, mla_q_norm, mla_kv_norm, mla_w_uq, mla_w_ukv,
           lru_conv_w, lru_conv_b, lru_w_a, lru_b_a, lru_w_x, lru_b_x, lru_lambda, ml_b_i, ml_b_f,
           out_norm, w_out):
    bsz, seq, d = x.shape
    n = bsz * seq
    assert d == D_MODEL and x.dtype == F32, (x.shape, x.dtype)
    assert seq % MLA_TQ == 0 and seq % ML_CHUNK == 0 and n % TOKEN_TILE == 0, (bsz, seq)
    assert ffn_w1.shape == (DEPTH, 2, D_MODEL, D_FF) and ffn_w2.shape == (DEPTH, 2, D_FF, D_MODEL), ffn_w1.shape
    row = lambda t: t.reshape(1, -1)
    q_src, q_dst, kv_perm = _mla_up_layout()
    cos, sin = _rope_tables(seq)
    swa_bias = _swa_bias_tiles(seq)
    xt = x.reshape(n, d)
    ffn_w = (ffn_w1.astype(BF16), ffn_w3.astype(BF16), ffn_w2.astype(BF16))
    w_in_p = _permute_w_in(w_in)
    for l in range(DEPTH):
        x1, za, zlru, sq, sk, sv, zml, zg = _ffn_inproj(
            xt, *ffn_w, (l, 0), row(ln_g[l, 0]), row(ln_b[l, 0]), w_in_p)
        seqv = lambda t: t.reshape(bsz, seq, t.shape[-1])
        wuq = jnp.zeros((Q_LORA, HEADS * GROUP_W), F32).at[:, q_dst].set(mla_w_uq[l][:, q_src]).astype(BF16)
        y_a = _mla(seqv(za), row(mla_q_norm[l]), row(mla_kv_norm[l]), wuq,
                   mla_w_ukv[l][:, kv_perm].astype(BF16), cos, sin)
        y_b = _rglru(seqv(zlru), lru_conv_w[l], row(lru_conv_b[l]),
                     _block_diag(lru_w_a[l]).astype(BF16), row(lru_b_a[l]),
                     _block_diag(lru_w_x[l]).astype(BF16), row(lru_b_x[l]), row(lru_lambda[l]))
        y_c = _swa(seqv(sq), seqv(sk), seqv(sv), swa_bias)
        gate_bias = jnp.broadcast_to(jnp.concatenate([ml_b_i[l], ml_b_f[l]])[:, None], (2 * HEADS, ML_CHUNK))
        y_d = _mlstm(seqv(zml), seqv(zg), gate_bias)
        ys = [t.reshape(n, GROUP_W) for t in (y_a, y_b, y_c, y_d)]
        xt = _outproj_ffn(x1, ys, row(out_norm[l]), w_out[l].astype(BF16), row(ln_g[l, 1]), row(ln_b[l, 1]),
                          *ffn_w, (l, 1), row(ln_g[l, 2]), row(ln_b[l, 2]))
    return xt.reshape(bsz, seq, d)
```

```python
import numpy as np
import jax
import jax.numpy as jnp
from jax import lax
from jax.experimental import pallas as pl
from jax.experimental.pallas import tpu as pltpu

F32 = jnp.float32
BF16 = jnp.bfloat16

D_MODEL = 1024
DEPTH = 2
GROUP_W = 256
HEAD_DIM = 64
HEADS = 4
D_FF = 2816
ALPHA = (2.0 * DEPTH) ** 0.25
Q_LORA = 256
KV_LORA = 128
MLA_NOPE = 64
MLA_ROPE = 32
ROPE_HALF = MLA_ROPE // 2
ROPE_THETA = 10000.0
CONV_W = 4
LRU_C = 8.0
DILATED_PATTERNS = ((128, 1), (512, 4), (2048, 16))

V7X_LANES = 128
V7X_SUBLANES = 8
V7X_VMEM_BYTES = 64 * 1024 * 1024

FF_CHUNK = 256
N_FF_CHUNKS = D_FF // FF_CHUNK
TOKEN_TILE = 512
ML_CHUNK = 256
ATT_T = 256
ATT_COLS = 1024
MLA_TQ = 512
PAIR_W = 2 * HEAD_DIM
N_BIAS_TILES = 4

Z_A = 512
Z_LRU = 512
Z_ML = 1024
Z_G = 128
N_Z = Z_A + Z_LRU + 3 * GROUP_W + Z_ML + Z_G

NEG_INF = float("-inf")
LOG2E = float(np.log2(np.e))


MIB = 1024 * 1024
FFN_VMEM_LIMIT = 56 * MIB
MIXER_VMEM_LIMIT = 48 * MIB
assert max(FFN_VMEM_LIMIT, MIXER_VMEM_LIMIT) < V7X_VMEM_BYTES


def _const_spec(shape):
    nd = len(shape)
    return pl.BlockSpec(shape, lambda *_: (0,) * nd, pipeline_mode=pl.Buffered(1))


def _stacked_spec(shape, lead):
    tail = shape[len(lead):]
    return pl.BlockSpec((None,) * len(lead) + tuple(tail), lambda *_: tuple(lead) + (0,) * len(tail),
                        pipeline_mode=pl.Buffered(1))


def _layernorm(v, g, b):
    mu = jnp.mean(v, axis=-1, keepdims=True)
    d = v - mu
    var = jnp.mean(d * d, axis=-1, keepdims=True)
    return d * lax.rsqrt(var + 1e-5) * g + b


def _rmsnorm(v, g, eps=1e-6):
    return v * lax.rsqrt(jnp.mean(v * v, axis=-1, keepdims=True) + eps) * g


def _dot(a, b):
    return jnp.dot(a, b, preferred_element_type=F32)


def _dot_nt(a, b):
    return lax.dot_general(a, b, (((1,), (1,)), ((), ())), preferred_element_type=F32)


def _head_of_lane(width, group=HEAD_DIM):
    return lax.broadcasted_iota(jnp.int32, (1, width), 1) // group


def _ffn_ln(x, w1_ref, w3_ref, w2_ref, acc_ref, g, b):
    xb = x.astype(BF16)
    for c in range(N_FF_CHUNKS):
        cols = slice(c * FF_CHUNK, (c + 1) * FF_CHUNK)
        h1 = _dot(xb, w1_ref[:, cols])
        h3 = _dot(xb, w3_ref[:, cols])
        h = (h1 * jax.nn.sigmoid(h1)) * h3
        part = _dot(h.astype(BF16), w2_ref[cols, :])
        if c == 0:
            acc_ref[...] = part
        else:
            acc_ref[...] += part
    return _layernorm(ALPHA * x + 0.5 * acc_ref[...], g, b)


def _relayout_w_in(src_ref, dst_ref):
    kr = Q_LORA + KV_LORA
    body_lo = kr + MLA_ROPE
    body_w = 9 * GROUP_W
    dst_ref[:, kr:Z_A] = jnp.zeros((D_MODEL, Z_A - kr), BF16)
    dst_ref[:, N_Z - Z_G:] = jnp.zeros((D_MODEL, Z_G), BF16)
    dst_ref[:, 0:kr] = src_ref[:, 0:kr]
    dst_ref[:, kr:kr + ROPE_HALF] = src_ref[:, kr:kr + ROPE_HALF]
    dst_ref[:, kr + HEAD_DIM:kr + HEAD_DIM + ROPE_HALF] = src_ref[:, kr + ROPE_HALF:body_lo]
    dst_ref[:, Z_A:Z_A + body_w] = src_ref[:, body_lo:body_lo + body_w]
    dst_ref[:, N_Z - Z_G:N_Z - Z_G + 2 * HEADS] = src_ref[:, body_lo + body_w:]


def _ffn_inproj_kernel(x_ref, w1_ref, w3_ref, w2_ref, g_ref, b_ref, win_nat_ref,
                       x1_ref, za_ref, zlru_ref, sq_ref, sk_ref, sv_ref, zml_ref, zg_ref, acc_ref, win_ref):
    @pl.when(pl.program_id(0) == 0)
    def _():
        _relayout_w_in(win_nat_ref, win_ref)

    x1 = _ffn_ln(x_ref[...], w1_ref, w3_ref, w2_ref, acc_ref, g_ref[...], b_ref[...])
    x1_ref[...] = x1
    xb = x1.astype(BF16)
    off = 0
    for ref in (za_ref, zlru_ref, sq_ref, sk_ref, sv_ref, zml_ref, zg_ref):
        w = ref.shape[-1]
        ref[...] = _dot(xb, win_ref[:, off:off + w])
        off += w


def _ffn_inproj(x, w1, w3, w2, lj, g, b, win):
    n = x.shape[0]
    tm = TOKEN_TILE
    tile = lambda w: pl.BlockSpec((tm, w), lambda i: (i, 0))
    widths = (D_MODEL, Z_A, Z_LRU, GROUP_W, GROUP_W, GROUP_W, Z_ML, Z_G)
    return pl.pallas_call(
        _ffn_inproj_kernel,
        grid=(n // tm,),
        in_specs=[tile(D_MODEL), _stacked_spec(w1.shape, lj), _stacked_spec(w3.shape, lj), _stacked_spec(w2.shape, lj),
                  _const_spec(g.shape), _const_spec(b.shape), _stacked_spec(win.shape, lj[:1])],
        out_specs=[tile(w) for w in widths],
        out_shape=[jax.ShapeDtypeStruct((n, w), F32) for w in widths],
        scratch_shapes=[pltpu.VMEM((tm, D_MODEL), F32), pltpu.VMEM((D_MODEL, N_Z), BF16)],
        compiler_params=pltpu.CompilerParams(dimension_semantics=("arbitrary",),
                                             vmem_limit_bytes=FFN_VMEM_LIMIT),
        name="ffn_inproj",
    )(x, w1, w3, w2, g, b, win)


def _outproj_ffn_kernel(x_ref, ya_ref, yb_ref, yc_ref, yd_ref,
                        on_ref, wout_ref, g1_ref, b1_ref, w1_ref, w3_ref, w2_ref, g2_ref, b2_ref,
                        out_ref, acc_ref):
    y = jnp.zeros(x_ref.shape, F32)
    for gi, y_ref in enumerate((ya_ref, yb_ref, yc_ref, yd_ref)):
        lo = gi * GROUP_W
        yn = _rmsnorm(y_ref[...], on_ref[:, lo:lo + GROUP_W])
        y = y + _dot(yn.astype(BF16), wout_ref[lo:lo + GROUP_W, :])
    x2 = _layernorm(ALPHA * x_ref[...] + y, g1_ref[...], b1_ref[...])
    out_ref[...] = _ffn_ln(x2, w1_ref, w3_ref, w2_ref, acc_ref, g2_ref[...], b2_ref[...])


def _outproj_ffn(x, ys, on, wout, g1, b1, w1, w3, w2, lj, g2, b2):
    n = x.shape[0]
    tm = TOKEN_TILE
    tile = lambda w: pl.BlockSpec((tm, w), lambda i: (i, 0))
    consts = (on, wout, g1, b1, w1, w3, w2, g2, b2)
    const_specs = [_stacked_spec(c.shape, lj) if c.ndim == 4 else _const_spec(c.shape) for c in consts]
    return pl.pallas_call(
        _outproj_ffn_kernel,
        grid=(n // tm,),
        in_specs=[tile(D_MODEL)] + [tile(GROUP_W)] * len(ys) + const_specs,
        out_specs=tile(D_MODEL),
        out_shape=jax.ShapeDtypeStruct((n, D_MODEL), F32),
        scratch_shapes=[pltpu.VMEM((tm, D_MODEL), F32)],
        compiler_params=pltpu.CompilerParams(dimension_semantics=("arbitrary",),
                                             vmem_limit_bytes=FFN_VMEM_LIMIT),
        name="outproj_ffn",
    )(x, *ys, *consts)


def _flash_state_scratch():
    return [pltpu.VMEM((1, ATT_COLS), F32), pltpu.VMEM((1, ATT_COLS), F32), pltpu.VMEM((PAIR_W, ATT_COLS), F32)]


def _flash_reset(state):
    m_ref, l_ref, acc_ref = state
    m_ref[...] = jnp.full(m_ref.shape, NEG_INF, F32)
    l_ref[...] = jnp.zeros_like(l_ref)
    acc_ref[...] = jnp.zeros_like(acc_ref)


def _flash_pipe_scratch():
    return [pltpu.VMEM((2, ATT_T, ATT_COLS), F32), pltpu.VMEM((2, 1, ATT_COLS), F32)]


def _flash_attend(n_steps, qm, key_block, value_block, bias_block, state, pipe, groups):
    m_ref, l_ref, acc_ref = state
    s_ref, mx_ref = pipe
    _flash_reset(state)

    def scores(j, slot):
        bias = bias_block(j)
        s_t = _dot_nt(key_block(j), qm) + jnp.concatenate([bias] * (ATT_COLS // bias.shape[1]), axis=1)
        s_ref[slot] = s_t
        mx_ref[slot] = jnp.max(s_t, axis=0, keepdims=True)

    scores(0, 0)

    def consume(j, with_next):
        cur = lax.rem(j, 2)
        m = m_ref[...]
        m_new = jnp.maximum(m, mx_ref[cur])
        a = jnp.exp2(m - m_new)
        pr = jnp.exp2(s_ref[cur] - m_new)
        m_ref[...] = m_new
        l_ref[...] = a * l_ref[...] + jnp.sum(pr, axis=0, keepdims=True)
        prb = pr.astype(BF16)
        if with_next:
            scores(j + 1, 1 - cur)
        vt_blk = value_block(j)
        for v_lo, c_lo, c_hi in groups:
            acc_ref[:, c_lo:c_hi] = (a[:, c_lo:c_hi] * acc_ref[:, c_lo:c_hi]
                                     + _dot(vt_blk[v_lo:v_lo + PAIR_W, :], prb[:, c_lo:c_hi]))

    def step(j, carry):
        consume(j, True)
        return carry

    lax.fori_loop(0, n_steps - 1, step, 0)
    consume(n_steps - 1, False)


def _flash_result(state):
    _, l_ref, acc_ref = state
    return jnp.transpose(acc_ref[...] * (1.0 / l_ref[...]))


def _merge_pair(o_a, o_b):
    lane = lax.broadcasted_iota(jnp.int32, (1, PAIR_W), 1)
    return jnp.where(lane < HEAD_DIM, o_a, o_b)


def _store_transposed_values(v, vt_ref, j):
    vt_ref[j] = jnp.transpose(v).astype(BF16)


def _mla_kernel(za_ref, qn_ref, kvn_ref, wuq_ref, wukv_ref, cos_ref, sin_ref, o_ref,
                qm_scr, kp_scr, vt_scr, o_scr, bias_scr, m_scr, l_scr, acc_scr, s_scr, mx_scr):
    seq = za_ref.shape[0]
    state = (m_scr, l_scr, acc_scr)
    pipe = (s_scr, mx_scr)
    c_q = (MLA_NOPE + MLA_ROPE) ** -0.5 * LOG2E

    def prep(j, carry):
        rows = pl.ds(pl.multiple_of(j * ATT_T, ATT_T), ATT_T)
        cos, sin = cos_ref[rows, :], sin_ref[rows, :]

        def rope(r):
            return r * cos + pltpu.roll(r, V7X_LANES // 2, axis=1) * sin

        cq = _rmsnorm(za_ref[rows, 0:Q_LORA], qn_ref[...])
        q = _dot(cq.astype(BF16), wuq_ref[...]) * c_q
        for h in range(HEADS):
            lo = h * GROUP_W
            qm_scr[h, rows, 0:V7X_LANES] = q[:, lo:lo + V7X_LANES].astype(BF16)
            qm_scr[h, rows, V7X_LANES:] = rope(q[:, lo + V7X_LANES:lo + GROUP_W]).astype(BF16)
        ckv = _rmsnorm(za_ref[rows, Q_LORA:Q_LORA + KV_LORA], kvn_ref[...])
        kv = _dot(ckv.astype(BF16), wukv_ref[...])
        k_rope = rope(za_ref[rows, Q_LORA + KV_LORA:]).astype(BF16)
        for p in range(2):
            kp_scr[p, rows, 0:V7X_LANES] = kv[:, p * PAIR_W:(p + 1) * PAIR_W].astype(BF16)
            kp_scr[p, rows, V7X_LANES:] = k_rope
        _store_transposed_values(kv[:, GROUP_W:], vt_scr, j)
        return carry

    lax.fori_loop(0, seq // ATT_T, prep, 0)

    key = lax.broadcasted_iota(jnp.int32, (ATT_T, MLA_TQ), 0)
    qry = lax.broadcasted_iota(jnp.int32, (ATT_T, MLA_TQ), 1)
    n_diag = MLA_TQ // ATT_T
    bias_scr[0] = jnp.zeros((ATT_T, MLA_TQ), F32)
    for d in range(n_diag):
        bias_scr[1 + d] = jnp.where(key + d * ATT_T <= qry, 0.0, NEG_INF)
    groups = [(0, 0, ATT_COLS)]

    def q_block(i, carry):
        q0 = pl.multiple_of(i * MLA_TQ, MLA_TQ)
        q_rows = pl.ds(q0, MLA_TQ)
        first_diag = i * n_diag

        def pair(p, c):
            qm = jnp.concatenate([qm_scr[2 * p, q_rows, :], qm_scr[2 * p + 1, q_rows, :]], axis=0)
            v_lo = pl.multiple_of(p * PAIR_W, PAIR_W)
            _flash_attend(
                first_diag + n_diag, qm,
                lambda j: kp_scr[p, pl.ds(pl.multiple_of(j * ATT_T, ATT_T), ATT_T), :],
                lambda j: vt_scr.at[j, pl.ds(v_lo, PAIR_W), :],
                lambda j: bias_scr[jnp.maximum(j - first_diag + 1, 0)],
                state, pipe, groups)
            o_t = _flash_result(state)
            o_scr[p] = _merge_pair(o_t[:MLA_TQ], o_t[MLA_TQ:])
            return c

        lax.fori_loop(0, 2, pair, 0)
        for p in range(2):
            o_ref[q_rows, p * PAIR_W:(p + 1) * PAIR_W] = o_scr[p]
        return carry

    lax.fori_loop(0, seq // MLA_TQ, q_block, 0)


def _mla(za, qn, kvn, wuq, wukv, cos, sin):
    b, s, _ = za.shape
    consts = (qn, kvn, wuq, wukv, cos, sin)
    return pl.pallas_call(
        _mla_kernel,
        grid=(b,),
        in_specs=[pl.BlockSpec((None, s, Z_A), lambda i: (i, 0, 0))] + [_const_spec(c.shape) for c in consts],
        out_specs=pl.BlockSpec((None, s, GROUP_W), lambda i: (i, 0, 0)),
        out_shape=jax.ShapeDtypeStruct((b, s, GROUP_W), F32),
        scratch_shapes=([pltpu.VMEM((HEADS, s, GROUP_W), BF16), pltpu.VMEM((2, s, GROUP_W), BF16),
                         pltpu.VMEM((s // ATT_T, GROUP_W, ATT_T), BF16), pltpu.VMEM((2, MLA_TQ, PAIR_W), F32),
                         pltpu.VMEM((1 + MLA_TQ // ATT_T, ATT_T, MLA_TQ), F32)]
                        + _flash_state_scratch() + _flash_pipe_scratch()),
        compiler_params=pltpu.CompilerParams(dimension_semantics=("arbitrary",),
                                             vmem_limit_bytes=MIXER_VMEM_LIMIT),
        name="mla",
    )(za, *consts)


def _swa_kernel(q_ref, k_ref, v_ref, bias_ref, o_ref, qm_scr, k_scr, vt_scr, m_scr, l_scr, acc_scr,
                s_scr, mx_scr):
    seq = q_ref.shape[0]
    state = (m_scr, l_scr, acc_scr)
    pipe = (s_scr, mx_scr)
    head = _head_of_lane(GROUP_W)
    c_q = HEAD_DIM ** -0.5 * LOG2E
    half = ATT_COLS // 2
    groups = [(0, 0, half), (PAIR_W, half, ATT_COLS)]

    def prep(j, carry):
        rows = pl.ds(pl.multiple_of(j * ATT_T, ATT_T), ATT_T)
        q = (q_ref[rows, :] * c_q).astype(BF16)
        for h in range(HEADS):
            qm_scr[h, rows, :] = jnp.where(head == h, q, jnp.zeros_like(q))
        k_scr[rows, :] = k_ref[rows, :].astype(BF16)
        _store_transposed_values(v_ref[rows, :], vt_scr, j)
        return carry

    lax.fori_loop(0, seq // ATT_T, prep, 0)

    def q_block(i, carry):
        q_rows = pl.ds(pl.multiple_of(i * ATT_T, ATT_T), ATT_T)
        qm = jnp.concatenate([qm_scr[h, q_rows, :] for h in range(HEADS)], axis=0)
        _flash_attend(
            i + 1, qm,
            lambda j: k_scr[pl.ds(pl.multiple_of(j * ATT_T, ATT_T), ATT_T), :],
            lambda j: vt_scr.at[j],
            lambda j: bias_ref[jnp.minimum(i - j, N_BIAS_TILES - 1)],
            state, pipe, groups)
        o_t = _flash_result(state)
        for p in range(2):
            lo = 2 * p * ATT_T
            o_ref[q_rows, p * PAIR_W:(p + 1) * PAIR_W] = _merge_pair(o_t[lo:lo + ATT_T], o_t[lo + ATT_T:lo + 2 * ATT_T])
        return carry

    lax.fori_loop(0, seq // ATT_T, q_block, 0)


def _swa_bias_tiles(seq):
    assert DILATED_PATTERNS[-1][0] >= seq - 1, "far tiles assume the widest window covers the sequence"
    assert all(w <= (N_BIAS_TILES - 2) * ATT_T for w, _ in DILATED_PATTERNS[:-1]), "narrow windows must end before the far tile"
    key = np.arange(ATT_T)[:, None]
    qry = np.arange(ATT_T)[None, :]
    tiles = []
    for dist in range(N_BIAS_TILES):
        d = dist * ATT_T + qry - key
        mult = sum(((d >= 0) & (d % dil == 0) & (d <= window)).astype(np.float32) for window, dil in DILATED_PATTERNS)
        with np.errstate(divide="ignore"):
            tiles.append(np.log2(mult))
    return jnp.asarray(np.stack(tiles), F32)


def _swa(q, k, v, bias):
    b, s, w = q.shape
    spec = pl.BlockSpec((None, s, w), lambda i: (i, 0, 0))
    return pl.pallas_call(
        _swa_kernel,
        grid=(b,),
        in_specs=[spec, spec, spec, _const_spec(bias.shape)],
        out_specs=spec,
        out_shape=jax.ShapeDtypeStruct((b, s, w), F32),
        scratch_shapes=([pltpu.VMEM((HEADS, s, w), BF16), pltpu.VMEM((s, w), BF16),
                         pltpu.VMEM((s // ATT_T, w, ATT_T), BF16)] + _flash_state_scratch()
                        + _flash_pipe_scratch()),
        compiler_params=pltpu.CompilerParams(dimension_semantics=("arbitrary",),
                                             vmem_limit_bytes=MIXER_VMEM_LIMIT),
        name="swa",
    )(q, k, v, bias)


def _softplus(x):
    return jnp.maximum(x, 0.0) + jnp.log1p(jnp.exp(-jnp.abs(x)))


def _gelu_tanh(x):
    c = np.sqrt(2.0 / np.pi).astype(np.float32)
    return 0.5 * x * (1.0 + jnp.tanh(c * (x + 0.044715 * (x * x * x))))


def _rglru_kernel(z_ref, cw_ref, cb_ref, wa_ref, ba_ref, wx_ref, bx_ref, lam_ref, o_ref, a_scr, h_scr):
    seq = z_ref.shape[0]
    h_scr[0:V7X_SUBLANES, :] = jnp.zeros((V7X_SUBLANES, GROUP_W), F32)
    h_scr[V7X_SUBLANES:V7X_SUBLANES + seq, :] = z_ref[:, 0:GROUP_W]
    xc = cb_ref[...]
    for j in range(CONV_W):
        start = V7X_SUBLANES - (CONV_W - 1 - j)
        xc = xc + cw_ref[j:j + 1, :] * h_scr[start:start + seq, :]
    xcb = xc.astype(BF16)
    r = jax.nn.sigmoid(_dot(xcb, wa_ref[...]) + ba_ref[...])
    i = jax.nn.sigmoid(_dot(xcb, wx_ref[...]) + bx_ref[...])
    log_a = -LRU_C * r * _softplus(-lam_ref[...])
    a = jnp.exp(log_a)
    gain2 = -jnp.tanh(log_a) * (a * a + 1.0)
    gain = jnp.where(gain2 > 0.0, gain2 * lax.rsqrt(gain2), 0.0)
    u = gain * (i * xc)
    groups = (seq // V7X_SUBLANES, V7X_SUBLANES, GROUP_W)
    a, u = a.reshape(groups), u.reshape(groups)
    in_group = lax.broadcasted_iota(jnp.int32, groups, 1)
    d = 1
    while d < V7X_SUBLANES:
        keep = in_group >= d
        u = a * jnp.where(keep, pltpu.roll(u, d, axis=1), 0.0) + u
        a = a * jnp.where(keep, pltpu.roll(a, d, axis=1), 1.0)
        d *= 2
    a_scr[...] = a.reshape(seq, GROUP_W)
    h_scr[0:seq, :] = u.reshape(seq, GROUP_W)

    def group(g, h_prev):
        rows = pl.ds(pl.multiple_of(g * V7X_SUBLANES, V7X_SUBLANES), V7X_SUBLANES)
        h = h_scr[rows, :] + a_scr[rows, :] * h_prev
        h_scr[rows, :] = h
        return jnp.broadcast_to(h[V7X_SUBLANES - 1:, :], h.shape)

    lax.fori_loop(0, seq // V7X_SUBLANES, group, jnp.zeros((V7X_SUBLANES, GROUP_W), F32), unroll=8)
    o_ref[...] = h_scr[0:seq, :] * _gelu_tanh(z_ref[:, GROUP_W:])


def _rglru(z, cw, cb, wa, ba, wx, bx, lam):
    b, s, _ = z.shape
    consts = (cw, cb, wa, ba, wx, bx, lam)
    return pl.pallas_call(
        _rglru_kernel,
        grid=(b,),
        in_specs=[pl.BlockSpec((None, s, Z_LRU), lambda i: (i, 0, 0))] + [_const_spec(c.shape) for c in consts],
        out_specs=pl.BlockSpec((None, s, GROUP_W), lambda i: (i, 0, 0)),
        out_shape=jax.ShapeDtypeStruct((b, s, GROUP_W), F32),
        scratch_shapes=[pltpu.VMEM((s, GROUP_W), F32), pltpu.VMEM((s + V7X_SUBLANES, GROUP_W), F32)],
        compiler_params=pltpu.CompilerParams(dimension_semantics=("arbitrary",),
                                             vmem_limit_bytes=MIXER_VMEM_LIMIT),
        name="rglru",
    )(z, *consts)


def _dot_exact(a, b):
    return jnp.dot(a, b, precision=lax.Precision.HIGHEST, preferred_element_type=F32)


def _per_head_rows(rows, height):
    return jnp.concatenate([jnp.broadcast_to(rows[h:h + 1, :], (height, rows.shape[1])) for h in range(HEADS)], axis=0)


def _lane_constant_to_width(x, width):
    w = x.shape[1]
    return x[:, :width] if w >= width else jnp.concatenate([x] * (width // w), axis=1)


def _mlstm_kernel(z_ref, g_ref, gb_ref, o_ref, ct_scr, n_scr, m_scr):
    seq = z_ref.shape[0]
    lc = ML_CHUNK
    sub = lax.broadcasted_iota(jnp.int32, (lc, lc), 0)
    lan = lax.broadcasted_iota(jnp.int32, (lc, lc), 1)
    tri_incl = jnp.where(sub <= lan, 1.0, 0.0)
    visible = sub <= lan
    head = _head_of_lane(GROUP_W)
    same_head = lax.broadcasted_iota(jnp.int32, (GROUP_W, GROUP_W), 0) // HEAD_DIM == head
    n_rows = lax.broadcasted_iota(jnp.int32, (V7X_SUBLANES, GROUP_W), 0) == head
    pad_rows = jnp.zeros((lc - V7X_SUBLANES, lc), F32)
    ct_scr[...] = jnp.zeros_like(ct_scr)
    n_scr[...] = jnp.zeros_like(n_scr)
    m_scr[...] = jnp.zeros_like(m_scr)

    def chunk(ci, carry):
        rows = pl.ds(pl.multiple_of(ci * lc, lc), lc)
        q = z_ref[rows, 0:GROUP_W].astype(BF16)
        k = (z_ref[rows, GROUP_W:2 * GROUP_W] * (HEAD_DIM ** -0.5)).astype(BF16)
        v_t = jnp.transpose(z_ref[rows, 2 * GROUP_W:3 * GROUP_W])
        og = z_ref[rows, 3 * GROUP_W:4 * GROUP_W]
        pre = jnp.transpose(g_ref[rows, :])[0:V7X_SUBLANES, :] + gb_ref[...]
        log_f = jnp.minimum(pre, 0.0) - jnp.log1p(jnp.exp(-jnp.abs(pre)))
        b_row = pltpu.roll(_dot_exact(log_f, tri_incl), HEADS, axis=0)
        key_row = pre - b_row
        key_col = jnp.transpose(jnp.concatenate([key_row, pad_rows], axis=0))
        m_prev = m_scr[...]
        ct_prev = ct_scr[...]
        n_prev = n_scr[...]
        inter_all = b_row + m_prev

        qs = jnp.concatenate([jnp.where(head == h, q, jnp.zeros_like(q)) for h in range(HEADS)], axis=0)
        s_all = _dot_nt(k, qs)
        qc_t = _dot_nt(ct_prev.astype(BF16), q)
        qn = _dot_nt(n_prev.astype(BF16), q)
        outs = []
        for h in range(HEADS):
            b_t = b_row[h:h + 1, :]
            dmat = jnp.where(visible, b_t + key_col[:, h:h + 1], NEG_INF)
            inter = inter_all[h:h + 1, :]
            mt = jnp.maximum(jnp.max(dmat, axis=0, keepdims=True), inter)
            w_t = s_all[:, h * lc:(h + 1) * lc] * jnp.exp(dmat - mt)
            w_inter = jnp.exp(inter - mt)
            lanes = slice(h * HEAD_DIM, (h + 1) * HEAD_DIM)
            num = _dot(v_t[lanes, :].astype(BF16), w_t.astype(BF16)) + w_inter * qc_t[lanes, :]
            den = jnp.sum(w_t, axis=0, keepdims=True) + w_inter * qn[h:h + 1, :]
            outs.append(num * (1.0 / jnp.maximum(jnp.abs(den), jnp.exp(-mt))))
        o_ref[rows, :] = jax.nn.sigmoid(og) * jnp.transpose(jnp.concatenate(outs, axis=0))

        g_all = jnp.broadcast_to(b_row[:, lc - 1:lc], b_row.shape)
        wk = g_all + key_row
        m_new = jnp.maximum(g_all + m_prev, jnp.max(wk, axis=1, keepdims=True))
        decay = jnp.exp(g_all + m_prev - m_new)
        w_row = jnp.exp(wk - m_new)
        vw_t = (v_t * _per_head_rows(w_row, HEAD_DIM)).astype(BF16)
        decay_w = _lane_constant_to_width(decay, GROUP_W)
        ct_scr[...] = _per_head_rows(decay_w, HEAD_DIM) * ct_prev + jnp.where(same_head, _dot(vw_t, k), 0.0)
        n_scr[...] = decay_w * n_prev + jnp.where(n_rows, _dot(w_row.astype(BF16), k), 0.0)
        m_scr[...] = m_new
        return carry

    lax.fori_loop(0, seq // lc, chunk, 0, unroll=4)


def _mlstm(z, gates, gate_bias):
    b, s, _ = z.shape
    return pl.pallas_call(
        _mlstm_kernel,
        grid=(b,),
        in_specs=[pl.BlockSpec((None, s, Z_ML), lambda i: (i, 0, 0)),
                  pl.BlockSpec((None, s, Z_G), lambda i: (i, 0, 0)),
                  _const_spec(gate_bias.shape)],
        out_specs=pl.BlockSpec((None, s, GROUP_W), lambda i: (i, 0, 0)),
        out_shape=jax.ShapeDtypeStruct((b, s, GROUP_W), F32),
        scratch_shapes=[pltpu.VMEM((GROUP_W, GROUP_W), F32), pltpu.VMEM((V7X_SUBLANES, GROUP_W), F32),
                        pltpu.VMEM((V7X_SUBLANES, ML_CHUNK), F32)],
        compiler_params=pltpu.CompilerParams(dimension_semantics=("arbitrary",),
                                             vmem_limit_bytes=MIXER_VMEM_LIMIT),
        name="mlstm",
    )(z, gates, gate_bias)


def _block_diag(w):
    out = jnp.zeros((GROUP_W, GROUP_W), w.dtype)
    for h in range(HEADS):
        out = out.at[h * HEAD_DIM:(h + 1) * HEAD_DIM, h * HEAD_DIM:(h + 1) * HEAD_DIM].set(w[h])
    return out


def _mla_up_layout():
    per_q = MLA_NOPE + MLA_ROPE
    q_src, q_dst = [], []
    for h in range(HEADS):
        base = h * GROUP_W
        nope_lo = base + (h % 2) * HEAD_DIM
        q_src += [h * per_q + i for i in range(per_q)]
        q_dst += [nope_lo + i for i in range(MLA_NOPE)]
        q_dst += [base + V7X_LANES + i for i in range(ROPE_HALF)]
        q_dst += [base + V7X_LANES + HEAD_DIM + i for i in range(ROPE_HALF)]
    per_kv = MLA_NOPE + HEAD_DIM
    kn = [h * per_kv + i for h in range(HEADS) for i in range(MLA_NOPE)]
    vv = [h * per_kv + MLA_NOPE + i for h in range(HEADS) for i in range(HEAD_DIM)]
    return np.array(q_src), np.array(q_dst), np.array(kn + vv)


def _rope_tables(seq):
    freqs = ROPE_THETA ** (-jnp.arange(ROPE_HALF, dtype=F32) / ROPE_HALF)
    ang = jnp.arange(seq, dtype=F32)[:, None] * freqs[None, :]
    cos = jnp.tile(jnp.cos(ang), (1, 2 * HEADS))
    sin = jnp.tile(jnp.sin(ang), (1, HEADS))
    return cos, jnp.concatenate([-sin, sin], axis=1)


def kernel(x, ln_g, ln_b, ffn_w1, ffn_w3, ffn_w2, w_in, mla_q_norm, mla_kv_norm, mla_w_uq, mla_w_ukv,
           lru_conv_w, lru_conv_b, lru_w_a, lru_b_a, lru_w_x, lru_b_x, lru_lambda, ml_b_i, ml_b_f,
           out_norm, w_out):
    bsz, seq, d = x.shape
    n = bsz * seq
    assert d == D_MODEL and x.dtype == F32, (x.shape, x.dtype)
    assert seq % MLA_TQ == 0 and seq % ML_CHUNK == 0 and n % TOKEN_TILE == 0, (bsz, seq)
    assert ffn_w1.shape == (DEPTH, 2, D_MODEL, D_FF) and ffn_w2.shape == (DEPTH, 2, D_FF, D_MODEL), ffn_w1.shape
    row = lambda t: t.reshape(1, -1)
    q_src, q_dst, kv_perm = _mla_up_layout()
    cos, sin = _rope_tables(seq)
    swa_bias = _swa_bias_tiles(seq)
    xt = x.reshape(n, d)
    ffn_w = (ffn_w1.astype(BF16), ffn_w3.astype(BF16), ffn_w2.astype(BF16))
    w_in_p = w_in.astype(BF16)
    for l in range(DEPTH):
        x1, za, zlru, sq, sk, sv, zml, zg = _ffn_inproj(
            xt, *ffn_w, (l, 0), row(ln_g[l, 0]), row(ln_b[l, 0]), w_in_p)
        seqv = lambda t: t.reshape(bsz, seq, t.shape[-1])
        wuq = jnp.zeros((Q_LORA, HEADS * GROUP_W), F32).at[:, q_dst].set(mla_w_uq[l][:, q_src]).astype(BF16)
        y_a = _mla(seqv(za), row(mla_q_norm[l]), row(mla_kv_norm[l]), wuq,
                   mla_w_ukv[l][:, kv_perm].astype(BF16), cos, sin)
        y_b = _rglru(seqv(zlru), lru_conv_w[l], row(lru_conv_b[l]),
                     _block_diag(lru_w_a[l]).astype(BF16), row(lru_b_a[l]),
                     _block_diag(lru_w_x[l]).astype(BF16), row(lru_b_x[l]), row(lru_lambda[l]))
        y_c = _swa(seqv(sq), seqv(sk), seqv(sv), swa_bias)
        gate_bias = jnp.broadcast_to(jnp.concatenate([ml_b_i[l], ml_b_f[l]])[:, None], (2 * HEADS, ML_CHUNK))
        y_d = _mlstm(seqv(zml), seqv(zg), gate_bias)
        ys = [t.reshape(n, GROUP_W) for t in (y_a, y_b, y_c, y_d)]
        xt = _outproj_ffn(x1, ys, row(out_norm[l]), w_out[l].astype(BF16), row(ln_g[l, 1]), row(ln_b[l, 1]),
                          *ffn_w, (l, 1), row(ln_g[l, 2]), row(ln_b[l, 2]))
    return xt.reshape(bsz, seq, d)
```

```python
import numpy as np
import jax
import jax.numpy as jnp
from jax import lax
from jax.experimental import pallas as pl
from jax.experimental.pallas import tpu as pltpu

F32 = jnp.float32
BF16 = jnp.bfloat16

D_MODEL = 1024
DEPTH = 2
GROUP_W = 256
HEAD_DIM = 64
HEADS = 4
D_FF = 2816
ALPHA = (2.0 * DEPTH) ** 0.25
Q_LORA = 256
KV_LORA = 128
MLA_NOPE = 64
MLA_ROPE = 32
ROPE_HALF = MLA_ROPE // 2
ROPE_THETA = 10000.0
CONV_W = 4
LRU_C = 8.0
DILATED_PATTERNS = ((128, 1), (512, 4), (2048, 16))

V7X_LANES = 128
V7X_SUBLANES = 8
V7X_VMEM_BYTES = 64 * 1024 * 1024

FF_CHUNK = 256
N_FF_CHUNKS = D_FF // FF_CHUNK
TOKEN_TILE = 512
ML_CHUNK = 256
ATT_T = 256
ATT_COLS = 1024
MLA_TQ = 512
PAIR_W = 2 * HEAD_DIM
N_BIAS_TILES = 4

Z_A = 512
Z_LRU = 512
Z_ML = 1024
Z_G = 128
N_Z = Z_A + Z_LRU + 3 * GROUP_W + Z_ML + Z_G

NEG_INF = float("-inf")
LOG2E = float(np.log2(np.e))


MIB = 1024 * 1024
FFN_VMEM_LIMIT = 56 * MIB
MIXER_VMEM_LIMIT = 48 * MIB
assert max(FFN_VMEM_LIMIT, MIXER_VMEM_LIMIT) < V7X_VMEM_BYTES


def _const_spec(shape):
    nd = len(shape)
    return pl.BlockSpec(shape, lambda *_: (0,) * nd, pipeline_mode=pl.Buffered(1))


def _stacked_spec(shape, lead):
    tail = shape[len(lead):]
    return pl.BlockSpec((None,) * len(lead) + tuple(tail), lambda *_: tuple(lead) + (0,) * len(tail),
                        pipeline_mode=pl.Buffered(1))


def _layernorm(v, g, b):
    mu = jnp.mean(v, axis=-1, keepdims=True)
    d = v - mu
    var = jnp.mean(d * d, axis=-1, keepdims=True)
    return d * lax.rsqrt(var + 1e-5) * g + b


def _rmsnorm(v, g, eps=1e-6):
    return v * lax.rsqrt(jnp.mean(v * v, axis=-1, keepdims=True) + eps) * g


def _dot(a, b):
    return jnp.dot(a, b, preferred_element_type=F32)


def _dot_nt(a, b):
    return lax.dot_general(a, b, (((1,), (1,)), ((), ())), preferred_element_type=F32)


def _head_of_lane(width, group=HEAD_DIM):
    return lax.broadcasted_iota(jnp.int32, (1, width), 1) // group


def _ffn_ln(x, w1_ref, w3_ref, w2_ref, acc_ref, g, b):
    xb = x.astype(BF16)
    for c in range(N_FF_CHUNKS):
        cols = slice(c * FF_CHUNK, (c + 1) * FF_CHUNK)
        h1 = _dot(xb, w1_ref[:, cols])
        h3 = _dot(xb, w3_ref[:, cols])
        h = (h1 * jax.nn.sigmoid(h1)) * h3
        part = _dot(h.astype(BF16), w2_ref[cols, :])
        if c == 0:
            acc_ref[...] = part
        else:
            acc_ref[...] += part
    return _layernorm(ALPHA * x + 0.5 * acc_ref[...], g, b)


def _relayout_w_in(src_ref, dst_ref):
    kr = Q_LORA + KV_LORA
    body_lo = kr + MLA_ROPE
    body_w = 9 * GROUP_W
    dst_ref[:, kr:Z_A] = jnp.zeros((D_MODEL, Z_A - kr), BF16)
    dst_ref[:, N_Z - Z_G:] = jnp.zeros((D_MODEL, Z_G), BF16)
    dst_ref[:, 0:kr] = src_ref[:, 0:kr]
    dst_ref[:, kr:kr + ROPE_HALF] = src_ref[:, kr:kr + ROPE_HALF]
    dst_ref[:, kr + HEAD_DIM:kr + HEAD_DIM + ROPE_HALF] = src_ref[:, kr + ROPE_HALF:body_lo]
    dst_ref[:, Z_A:Z_A + body_w] = src_ref[:, body_lo:body_lo + body_w]
    dst_ref[:, N_Z - Z_G:N_Z - Z_G + 2 * HEADS] = src_ref[:, body_lo + body_w:]


def _ffn_inproj_kernel(x_ref, w1_ref, w3_ref, w2_ref, g_ref, b_ref, win_nat_ref,
                       x1_ref, za_ref, zlru_ref, sq_ref, sk_ref, sv_ref, zml_ref, zg_ref, acc_ref, win_ref):
    @pl.when(pl.program_id(0) == 0)
    def _():
        _relayout_w_in(win_nat_ref, win_ref)

    x1 = _ffn_ln(x_ref[...], w1_ref, w3_ref, w2_ref, acc_ref, g_ref[...], b_ref[...])
    x1_ref[...] = x1
    xb = x1.astype(BF16)
    off = 0
    for ref in (za_ref, zlru_ref, sq_ref, sk_ref, sv_ref, zml_ref, zg_ref):
        w = ref.shape[-1]
        ref[...] = _dot(xb, win_ref[:, off:off + w])
        off += w


def _ffn_inproj(x, w1, w3, w2, lj, g, b, win):
    n = x.shape[0]
    tm = TOKEN_TILE
    tile = lambda w: pl.BlockSpec((tm, w), lambda i: (i, 0))
    widths = (D_MODEL, Z_A, Z_LRU, GROUP_W, GROUP_W, GROUP_W, Z_ML, Z_G)
    return pl.pallas_call(
        _ffn_inproj_kernel,
        grid=(n // tm,),
        in_specs=[tile(D_MODEL), _stacked_spec(w1.shape, lj), _stacked_spec(w3.shape, lj), _stacked_spec(w2.shape, lj),
                  _const_spec(g.shape), _const_spec(b.shape), _stacked_spec(win.shape, lj[:1])],
        out_specs=[tile(w) for w in widths],
        out_shape=[jax.ShapeDtypeStruct((n, w), F32) for w in widths],
        scratch_shapes=[pltpu.VMEM((tm, D_MODEL), F32), pltpu.VMEM((D_MODEL, N_Z), BF16)],
        compiler_params=pltpu.CompilerParams(dimension_semantics=("arbitrary",),
                                             vmem_limit_bytes=FFN_VMEM_LIMIT),
        name="ffn_inproj",
    )(x, w1, w3, w2, g, b, win)


def _outproj_ffn_kernel(x_ref, ya_ref, yb_ref, yc_ref, yd_ref,
                        on_ref, wout_ref, g1_ref, b1_ref, w1_ref, w3_ref, w2_ref, g2_ref, b2_ref,
                        out_ref, acc_ref):
    y = jnp.zeros(x_ref.shape, F32)
    for gi, y_ref in enumerate((ya_ref, yb_ref, yc_ref, yd_ref)):
        lo = gi * GROUP_W
        yn = _rmsnorm(y_ref[...], on_ref[:, lo:lo + GROUP_W])
        y = y + _dot(yn.astype(BF16), wout_ref[lo:lo + GROUP_W, :])
    x2 = _layernorm(ALPHA * x_ref[...] + y, g1_ref[...], b1_ref[...])
    out_ref[...] = _ffn_ln(x2, w1_ref, w3_ref, w2_ref, acc_ref, g2_ref[...], b2_ref[...])


def _outproj_ffn(x, ys, on, wout, g1, b1, w1, w3, w2, lj, g2, b2):
    n = x.shape[0]
    tm = TOKEN_TILE
    tile = lambda w: pl.BlockSpec((tm, w), lambda i: (i, 0))
    consts = (on, wout, g1, b1, w1, w3, w2, g2, b2)
    const_specs = [_stacked_spec(c.shape, lj) if c.ndim == 4 else _const_spec(c.shape) for c in consts]
    return pl.pallas_call(
        _outproj_ffn_kernel,
        grid=(n // tm,),
        in_specs=[tile(D_MODEL)] + [tile(GROUP_W)] * len(ys) + const_specs,
        out_specs=tile(D_MODEL),
        out_shape=jax.ShapeDtypeStruct((n, D_MODEL), F32),
        scratch_shapes=[pltpu.VMEM((tm, D_MODEL), F32)],
        compiler_params=pltpu.CompilerParams(dimension_semantics=("arbitrary",),
                                             vmem_limit_bytes=FFN_VMEM_LIMIT),
        name="outproj_ffn",
    )(x, *ys, *consts)


def _flash_state_scratch():
    return [pltpu.VMEM((1, ATT_COLS), F32), pltpu.VMEM((1, ATT_COLS), F32), pltpu.VMEM((PAIR_W, ATT_COLS), F32)]


def _flash_reset(state):
    m_ref, l_ref, acc_ref = state
    m_ref[...] = jnp.full(m_ref.shape, NEG_INF, F32)
    l_ref[...] = jnp.zeros_like(l_ref)
    acc_ref[...] = jnp.zeros_like(acc_ref)


def _flash_pipe_scratch():
    return [pltpu.VMEM((2, ATT_T, ATT_COLS), F32), pltpu.VMEM((2, 1, ATT_COLS), F32)]


def _flash_attend(n_steps, qm, key_block, value_block, bias_block, state, pipe, groups):
    m_ref, l_ref, acc_ref = state
    s_ref, mx_ref = pipe
    _flash_reset(state)

    def scores(j, slot):
        bias = bias_block(j)
        s_t = _dot_nt(key_block(j), qm) + jnp.concatenate([bias] * (ATT_COLS // bias.shape[1]), axis=1)
        s_ref[slot] = s_t
        mx_ref[slot] = jnp.max(s_t, axis=0, keepdims=True)

    scores(0, 0)

    def consume(j, with_next):
        cur = lax.rem(j, 2)
        m = m_ref[...]
        m_new = jnp.maximum(m, mx_ref[cur])
        a = jnp.exp2(m - m_new)
        pr = jnp.exp2(s_ref[cur] - m_new)
        m_ref[...] = m_new
        l_ref[...] = a * l_ref[...] + jnp.sum(pr, axis=0, keepdims=True)
        prb = pr.astype(BF16)
        if with_next:
            scores(j + 1, 1 - cur)
        vt_blk = value_block(j)
        for v_lo, c_lo, c_hi in groups:
            acc_ref[:, c_lo:c_hi] = (a[:, c_lo:c_hi] * acc_ref[:, c_lo:c_hi]
                                     + _dot(vt_blk[v_lo:v_lo + PAIR_W, :], prb[:, c_lo:c_hi]))

    def step(j, carry):
        consume(j, True)
        return carry

    lax.fori_loop(0, n_steps - 1, step, 0)
    consume(n_steps - 1, False)


def _flash_result(state):
    _, l_ref, acc_ref = state
    return jnp.transpose(acc_ref[...] * (1.0 / l_ref[...]))


def _merge_pair(o_a, o_b):
    lane = lax.broadcasted_iota(jnp.int32, (1, PAIR_W), 1)
    return jnp.where(lane < HEAD_DIM, o_a, o_b)


def _store_transposed_values(v, vt_ref, j):
    vt_ref[j] = jnp.transpose(v).astype(BF16)


def _mla_kernel(za_ref, qn_ref, kvn_ref, wuq_ref, wukv_ref, cos_ref, sin_ref, o_ref,
                qm_scr, kp_scr, vt_scr, o_scr, bias_scr, m_scr, l_scr, acc_scr, s_scr, mx_scr):
    seq = za_ref.shape[0]
    state = (m_scr, l_scr, acc_scr)
    pipe = (s_scr, mx_scr)
    c_q = (MLA_NOPE + MLA_ROPE) ** -0.5 * LOG2E

    def prep(j, carry):
        rows = pl.ds(pl.multiple_of(j * ATT_T, ATT_T), ATT_T)
        cos, sin = cos_ref[rows, :], sin_ref[rows, :]

        def rope(r):
            return r * cos + pltpu.roll(r, V7X_LANES // 2, axis=1) * sin

        cq = _rmsnorm(za_ref[rows, 0:Q_LORA], qn_ref[...])
        q = _dot(cq.astype(BF16), wuq_ref[...]) * c_q
        for h in range(HEADS):
            lo = h * GROUP_W
            qm_scr[h, rows, 0:V7X_LANES] = q[:, lo:lo + V7X_LANES].astype(BF16)
            qm_scr[h, rows, V7X_LANES:] = rope(q[:, lo + V7X_LANES:lo + GROUP_W]).astype(BF16)
        ckv = _rmsnorm(za_ref[rows, Q_LORA:Q_LORA + KV_LORA], kvn_ref[...])
        kv = _dot(ckv.astype(BF16), wukv_ref[...])
        k_rope = rope(za_ref[rows, Q_LORA + KV_LORA:]).astype(BF16)
        for p in range(2):
            kp_scr[p, rows, 0:V7X_LANES] = kv[:, p * PAIR_W:(p + 1) * PAIR_W].astype(BF16)
            kp_scr[p, rows, V7X_LANES:] = k_rope
        _store_transposed_values(kv[:, GROUP_W:], vt_scr, j)
        return carry

    lax.fori_loop(0, seq // ATT_T, prep, 0)

    key = lax.broadcasted_iota(jnp.int32, (ATT_T, MLA_TQ), 0)
    qry = lax.broadcasted_iota(jnp.int32, (ATT_T, MLA_TQ), 1)
    n_diag = MLA_TQ // ATT_T
    bias_scr[0] = jnp.zeros((ATT_T, MLA_TQ), F32)
    for d in range(n_diag):
        bias_scr[1 + d] = jnp.where(key + d * ATT_T <= qry, 0.0, NEG_INF)
    groups = [(0, 0, ATT_COLS)]

    def q_block(i, carry):
        q0 = pl.multiple_of(i * MLA_TQ, MLA_TQ)
        q_rows = pl.ds(q0, MLA_TQ)
        first_diag = i * n_diag

        def pair(p, c):
            qm = jnp.concatenate([qm_scr[2 * p, q_rows, :], qm_scr[2 * p + 1, q_rows, :]], axis=0)
            v_lo = pl.multiple_of(p * PAIR_W, PAIR_W)
            _flash_attend(
                first_diag + n_diag, qm,
                lambda j: kp_scr[p, pl.ds(pl.multiple_of(j * ATT_T, ATT_T), ATT_T), :],
                lambda j: vt_scr.at[j, pl.ds(v_lo, PAIR_W), :],
                lambda j: bias_scr[jnp.maximum(j - first_diag + 1, 0)],
                state, pipe, groups)
            o_t = _flash_result(state)
            o_scr[p] = _merge_pair(o_t[:MLA_TQ], o_t[MLA_TQ:])
            return c

        lax.fori_loop(0, 2, pair, 0)
        for p in range(2):
            o_ref[q_rows, p * PAIR_W:(p + 1) * PAIR_W] = o_scr[p]
        return carry

    lax.fori_loop(0, seq // MLA_TQ, q_block, 0)


def _mla(za, qn, kvn, wuq, wukv, cos, sin):
    b, s, _ = za.shape
    consts = (qn, kvn, wuq, wukv, cos, sin)
    return pl.pallas_call(
        _mla_kernel,
        grid=(b,),
        in_specs=[pl.BlockSpec((None, s, Z_A), lambda i: (i, 0, 0))] + [_const_spec(c.shape) for c in consts],
        out_specs=pl.BlockSpec((None, s, GROUP_W), lambda i: (i, 0, 0)),
        out_shape=jax.ShapeDtypeStruct((b, s, GROUP_W), F32),
        scratch_shapes=([pltpu.VMEM((HEADS, s, GROUP_W), BF16), pltpu.VMEM((2, s, GROUP_W), BF16),
                         pltpu.VMEM((s // ATT_T, GROUP_W, ATT_T), BF16), pltpu.VMEM((2, MLA_TQ, PAIR_W), F32),
                         pltpu.VMEM((1 + MLA_TQ // ATT_T, ATT_T, MLA_TQ), F32)]
                        + _flash_state_scratch() + _flash_pipe_scratch()),
        compiler_params=pltpu.CompilerParams(dimension_semantics=("arbitrary",),
                                             vmem_limit_bytes=MIXER_VMEM_LIMIT),
        name="mla",
    )(za, *consts)


def _mla_latent_kernel(za_ref, qn_ref, kvn_ref, wuq_ref, wkt_ref, wv_ref, cos_ref, sin_ref, o_ref,
                       qm_scr, k_scr, vt_scr, bias_scr, m_scr, l_scr, acc_scr, s_scr, mx_scr):
    seq = za_ref.shape[0]
    state = (m_scr, l_scr, acc_scr)
    pipe = (s_scr, mx_scr)
    c_q = (MLA_NOPE + MLA_ROPE) ** -0.5 * LOG2E
    rope_lo = HEADS * MLA_NOPE

    def prep(j, carry):
        rows = pl.ds(pl.multiple_of(j * ATT_T, ATT_T), ATT_T)
        cos, sin = cos_ref[rows, :], sin_ref[rows, :]

        def rope(r):
            return r * cos + pltpu.roll(r, V7X_LANES // 2, axis=1) * sin

        cq = _rmsnorm(za_ref[rows, 0:Q_LORA], qn_ref[...])
        q = _dot(cq.astype(BF16), wuq_ref[...])
        q_lat = _dot(q[:, 0:rope_lo].astype(BF16), wkt_ref[...])
        for h in range(HEADS):
            lat = slice(h * KV_LORA, (h + 1) * KV_LORA)
            rot = slice(rope_lo + h * V7X_LANES, rope_lo + (h + 1) * V7X_LANES)
            qm_scr[h, rows, 0:KV_LORA] = (q_lat[:, lat] * c_q).astype(BF16)
            qm_scr[h, rows, KV_LORA:] = (rope(q[:, rot]) * c_q).astype(BF16)
        ckv = _rmsnorm(za_ref[rows, Q_LORA:Q_LORA + KV_LORA], kvn_ref[...])
        k_scr[rows, 0:KV_LORA] = ckv.astype(BF16)
        k_scr[rows, KV_LORA:] = rope(za_ref[rows, Q_LORA + KV_LORA:]).astype(BF16)
        vt_scr[j] = jnp.transpose(ckv).astype(BF16)
        return carry

    lax.fori_loop(0, seq // ATT_T, prep, 0)

    key = lax.broadcasted_iota(jnp.int32, (ATT_T, ATT_T), 0)
    qry = lax.broadcasted_iota(jnp.int32, (ATT_T, ATT_T), 1)
    bias_scr[0] = jnp.zeros((ATT_T, ATT_T), F32)
    bias_scr[1] = jnp.where(key <= qry, 0.0, NEG_INF)
    groups = [(0, 0, ATT_COLS)]

    def q_block(i, carry):
        q_rows = pl.ds(pl.multiple_of(i * ATT_T, ATT_T), ATT_T)
        qm = jnp.concatenate([qm_scr[h, q_rows, :] for h in range(HEADS)], axis=0)
        _flash_attend(
            i + 1, qm,
            lambda j: k_scr[pl.ds(pl.multiple_of(j * ATT_T, ATT_T), ATT_T), :],
            lambda j: vt_scr.at[j],
            lambda j: bias_scr[jnp.maximum(j - i + 1, 0)],
            state, pipe, groups)
        o_lat = _flash_result(state).astype(BF16)
        out = _dot(o_lat[0:ATT_T], wv_ref[0])
        for h in range(1, HEADS):
            out = out + _dot(o_lat[h * ATT_T:(h + 1) * ATT_T], wv_ref[h])
        o_ref[q_rows, :] = out
        return carry

    lax.fori_loop(0, seq // ATT_T, q_block, 0)


def _mla_latent(za, qn, kvn, wuq, wkt, wv, cos, sin):
    b, s, _ = za.shape
    consts = (qn, kvn, wuq, wkt, wv, cos, sin)
    return pl.pallas_call(
        _mla_latent_kernel,
        grid=(b,),
        in_specs=[pl.BlockSpec((None, s, Z_A), lambda i: (i, 0, 0))] + [_const_spec(c.shape) for c in consts],
        out_specs=pl.BlockSpec((None, s, GROUP_W), lambda i: (i, 0, 0)),
        out_shape=jax.ShapeDtypeStruct((b, s, GROUP_W), F32),
        scratch_shapes=([pltpu.VMEM((HEADS, s, GROUP_W), BF16), pltpu.VMEM((s, GROUP_W), BF16),
                         pltpu.VMEM((s // ATT_T, KV_LORA, ATT_T), BF16), pltpu.VMEM((2, ATT_T, ATT_T), F32)]
                        + _flash_state_scratch() + _flash_pipe_scratch()),
        compiler_params=pltpu.CompilerParams(dimension_semantics=("arbitrary",),
                                             vmem_limit_bytes=MIXER_VMEM_LIMIT),
        name="mla",
    )(za, *consts)


def _mla_latent_weights(w_uq, w_ukv):
    per_q = MLA_NOPE + MLA_ROPE
    per_kv = MLA_NOPE + HEAD_DIM
    rope_lo = HEADS * MLA_NOPE
    wuq = jnp.zeros((Q_LORA, rope_lo + HEADS * V7X_LANES), F32)
    wkt = jnp.zeros((rope_lo, HEADS * KV_LORA), F32)
    wv = jnp.zeros((HEADS, KV_LORA, GROUP_W), F32)
    for h in range(HEADS):
        q0, kv0, r0 = h * per_q, h * per_kv, rope_lo + h * V7X_LANES
        wuq = wuq.at[:, h * MLA_NOPE:(h + 1) * MLA_NOPE].set(w_uq[:, q0:q0 + MLA_NOPE])
        wuq = wuq.at[:, r0:r0 + ROPE_HALF].set(w_uq[:, q0 + MLA_NOPE:q0 + MLA_NOPE + ROPE_HALF])
        wuq = wuq.at[:, r0 + HEAD_DIM:r0 + HEAD_DIM + ROPE_HALF].set(w_uq[:, q0 + MLA_NOPE + ROPE_HALF:q0 + per_q])
        wkt = wkt.at[h * MLA_NOPE:(h + 1) * MLA_NOPE, h * KV_LORA:(h + 1) * KV_LORA].set(w_ukv[:, kv0:kv0 + MLA_NOPE].T)
        wv = wv.at[h, :, h * HEAD_DIM:(h + 1) * HEAD_DIM].set(w_ukv[:, kv0 + MLA_NOPE:kv0 + per_kv])
    return wuq.astype(BF16), wkt.astype(BF16), wv.astype(BF16)


def _swa_kernel(q_ref, k_ref, v_ref, bias_ref, o_ref, qm_scr, k_scr, vt_scr, m_scr, l_scr, acc_scr,
                s_scr, mx_scr):
    seq = q_ref.shape[0]
    state = (m_scr, l_scr, acc_scr)
    pipe = (s_scr, mx_scr)
    head = _head_of_lane(GROUP_W)
    c_q = HEAD_DIM ** -0.5 * LOG2E
    half = ATT_COLS // 2
    groups = [(0, 0, half), (PAIR_W, half, ATT_COLS)]

    def prep(j, carry):
        rows = pl.ds(pl.multiple_of(j * ATT_T, ATT_T), ATT_T)
        q = (q_ref[rows, :] * c_q).astype(BF16)
        for h in range(HEADS):
            qm_scr[h, rows, :] = jnp.where(head == h, q, jnp.zeros_like(q))
        k_scr[rows, :] = k_ref[rows, :].astype(BF16)
        _store_transposed_values(v_ref[rows, :], vt_scr, j)
        return carry

    lax.fori_loop(0, seq // ATT_T, prep, 0)

    def q_block(i, carry):
        q_rows = pl.ds(pl.multiple_of(i * ATT_T, ATT_T), ATT_T)
        qm = jnp.concatenate([qm_scr[h, q_rows, :] for h in range(HEADS)], axis=0)
        _flash_attend(
            i + 1, qm,
            lambda j: k_scr[pl.ds(pl.multiple_of(j * ATT_T, ATT_T), ATT_T), :],
            lambda j: vt_scr.at[j],
            lambda j: bias_ref[jnp.minimum(i - j, N_BIAS_TILES - 1)],
            state, pipe, groups)
        o_t = _flash_result(state)
        for p in range(2):
            lo = 2 * p * ATT_T
            o_ref[q_rows, p * PAIR_W:(p + 1) * PAIR_W] = _merge_pair(o_t[lo:lo + ATT_T], o_t[lo + ATT_T:lo + 2 * ATT_T])
        return carry

    lax.fori_loop(0, seq // ATT_T, q_block, 0)


def _swa_bias_tiles(seq):
    assert DILATED_PATTERNS[-1][0] >= seq - 1, "far tiles assume the widest window covers the sequence"
    assert all(w <= (N_BIAS_TILES - 2) * ATT_T for w, _ in DILATED_PATTERNS[:-1]), "narrow windows must end before the far tile"
    key = np.arange(ATT_T)[:, None]
    qry = np.arange(ATT_T)[None, :]
    tiles = []
    for dist in range(N_BIAS_TILES):
        d = dist * ATT_T + qry - key
        mult = sum(((d >= 0) & (d % dil == 0) & (d <= window)).astype(np.float32) for window, dil in DILATED_PATTERNS)
        with np.errstate(divide="ignore"):
            tiles.append(np.log2(mult))
    return jnp.asarray(np.stack(tiles), F32)


def _swa(q, k, v, bias):
    b, s, w = q.shape
    spec = pl.BlockSpec((None, s, w), lambda i: (i, 0, 0))
    return pl.pallas_call(
        _swa_kernel,
        grid=(b,),
        in_specs=[spec, spec, spec, _const_spec(bias.shape)],
        out_specs=spec,
        out_shape=jax.ShapeDtypeStruct((b, s, w), F32),
        scratch_shapes=([pltpu.VMEM((HEADS, s, w), BF16), pltpu.VMEM((s, w), BF16),
                         pltpu.VMEM((s // ATT_T, w, ATT_T), BF16)] + _flash_state_scratch()
                        + _flash_pipe_scratch()),
        compiler_params=pltpu.CompilerParams(dimension_semantics=("arbitrary",),
                                             vmem_limit_bytes=MIXER_VMEM_LIMIT),
        name="swa",
    )(q, k, v, bias)


def _softplus(x):
    return jnp.maximum(x, 0.0) + jnp.log1p(jnp.exp(-jnp.abs(x)))


def _gelu_tanh(x):
    c = np.sqrt(2.0 / np.pi).astype(np.float32)
    return 0.5 * x * (1.0 + jnp.tanh(c * (x + 0.044715 * (x * x * x))))


def _rglru_kernel(z_ref, cw_ref, cb_ref, wa_ref, ba_ref, wx_ref, bx_ref, lam_ref, o_ref, a_scr, h_scr):
    seq = z_ref.shape[0]
    h_scr[0:V7X_SUBLANES, :] = jnp.zeros((V7X_SUBLANES, GROUP_W), F32)
    h_scr[V7X_SUBLANES:V7X_SUBLANES + seq, :] = z_ref[:, 0:GROUP_W]
    xc = cb_ref[...]
    for j in range(CONV_W):
        start = V7X_SUBLANES - (CONV_W - 1 - j)
        xc = xc + cw_ref[j:j + 1, :] * h_scr[start:start + seq, :]
    xcb = xc.astype(BF16)
    r = jax.nn.sigmoid(_dot(xcb, wa_ref[...]) + ba_ref[...])
    i = jax.nn.sigmoid(_dot(xcb, wx_ref[...]) + bx_ref[...])
    log_a = -LRU_C * r * _softplus(-lam_ref[...])
    a = jnp.exp(log_a)
    gain2 = -jnp.tanh(log_a) * (a * a + 1.0)
    gain = jnp.where(gain2 > 0.0, gain2 * lax.rsqrt(gain2), 0.0)
    u = gain * (i * xc)
    groups = (seq // V7X_SUBLANES, V7X_SUBLANES, GROUP_W)
    a, u = a.reshape(groups), u.reshape(groups)
    in_group = lax.broadcasted_iota(jnp.int32, groups, 1)
    d = 1
    while d < V7X_SUBLANES:
        keep = in_group >= d
        u = a * jnp.where(keep, pltpu.roll(u, d, axis=1), 0.0) + u
        a = a * jnp.where(keep, pltpu.roll(a, d, axis=1), 1.0)
        d *= 2
    a_scr[...] = a.reshape(seq, GROUP_W)
    h_scr[0:seq, :] = u.reshape(seq, GROUP_W)

    def group(g, h_prev):
        rows = pl.ds(pl.multiple_of(g * V7X_SUBLANES, V7X_SUBLANES), V7X_SUBLANES)
        h = h_scr[rows, :] + a_scr[rows, :] * h_prev
        h_scr[rows, :] = h
        return jnp.broadcast_to(h[V7X_SUBLANES - 1:, :], h.shape)

    lax.fori_loop(0, seq // V7X_SUBLANES, group, jnp.zeros((V7X_SUBLANES, GROUP_W), F32), unroll=8)
    o_ref[...] = h_scr[0:seq, :] * _gelu_tanh(z_ref[:, GROUP_W:])


def _rglru(z, cw, cb, wa, ba, wx, bx, lam):
    b, s, _ = z.shape
    consts = (cw, cb, wa, ba, wx, bx, lam)
    return pl.pallas_call(
        _rglru_kernel,
        grid=(b,),
        in_specs=[pl.BlockSpec((None, s, Z_LRU), lambda i: (i, 0, 0))] + [_const_spec(c.shape) for c in consts],
        out_specs=pl.BlockSpec((None, s, GROUP_W), lambda i: (i, 0, 0)),
        out_shape=jax.ShapeDtypeStruct((b, s, GROUP_W), F32),
        scratch_shapes=[pltpu.VMEM((s, GROUP_W), F32), pltpu.VMEM((s + V7X_SUBLANES, GROUP_W), F32)],
        compiler_params=pltpu.CompilerParams(dimension_semantics=("arbitrary",),
                                             vmem_limit_bytes=MIXER_VMEM_LIMIT),
        name="rglru",
    )(z, *consts)


def _dot_exact(a, b):
    return jnp.dot(a, b, precision=lax.Precision.HIGHEST, preferred_element_type=F32)


def _per_head_rows(rows, height):
    return jnp.concatenate([jnp.broadcast_to(rows[h:h + 1, :], (height, rows.shape[1])) for h in range(HEADS)], axis=0)


def _lane_constant_to_width(x, width):
    w = x.shape[1]
    return x[:, :width] if w >= width else jnp.concatenate([x] * (width // w), axis=1)


def _mlstm_kernel(z_ref, g_ref, gb_ref, o_ref, ct_scr, n_scr, m_scr):
    seq = z_ref.shape[0]
    lc = ML_CHUNK
    sub = lax.broadcasted_iota(jnp.int32, (lc, lc), 0)
    lan = lax.broadcasted_iota(jnp.int32, (lc, lc), 1)
    tri_incl = jnp.where(sub <= lan, 1.0, 0.0)
    visible = sub <= lan
    head = _head_of_lane(GROUP_W)
    same_head = lax.broadcasted_iota(jnp.int32, (GROUP_W, GROUP_W), 0) // HEAD_DIM == head
    n_rows = lax.broadcasted_iota(jnp.int32, (V7X_SUBLANES, GROUP_W), 0) == head
    pad_rows = jnp.zeros((lc - V7X_SUBLANES, lc), F32)
    ct_scr[...] = jnp.zeros_like(ct_scr)
    n_scr[...] = jnp.zeros_like(n_scr)
    m_scr[...] = jnp.zeros_like(m_scr)

    def chunk(ci, carry):
        rows = pl.ds(pl.multiple_of(ci * lc, lc), lc)
        q = z_ref[rows, 0:GROUP_W].astype(BF16)
        k = (z_ref[rows, GROUP_W:2 * GROUP_W] * (HEAD_DIM ** -0.5)).astype(BF16)
        v_t = jnp.transpose(z_ref[rows, 2 * GROUP_W:3 * GROUP_W])
        og = z_ref[rows, 3 * GROUP_W:4 * GROUP_W]
        pre = jnp.transpose(g_ref[rows, :])[0:V7X_SUBLANES, :] + gb_ref[...]
        log_f = jnp.minimum(pre, 0.0) - jnp.log1p(jnp.exp(-jnp.abs(pre)))
        b_row = pltpu.roll(_dot_exact(log_f, tri_incl), HEADS, axis=0)
        key_row = pre - b_row
        key_col = jnp.transpose(jnp.concatenate([key_row, pad_rows], axis=0))
        m_prev = m_scr[...]
        ct_prev = ct_scr[...]
        n_prev = n_scr[...]
        inter_all = b_row + m_prev

        qs = jnp.concatenate([jnp.where(head == h, q, jnp.zeros_like(q)) for h in range(HEADS)], axis=0)
        s_all = _dot_nt(k, qs)
        qc_t = _dot_nt(ct_prev.astype(BF16), q)
        qn = _dot_nt(n_prev.astype(BF16), q)
        outs = []
        for h in range(HEADS):
            b_t = b_row[h:h + 1, :]
            dmat = jnp.where(visible, b_t + key_col[:, h:h + 1], NEG_INF)
            inter = inter_all[h:h + 1, :]
            mt = jnp.maximum(jnp.max(dmat, axis=0, keepdims=True), inter)
            w_t = s_all[:, h * lc:(h + 1) * lc] * jnp.exp(dmat - mt)
            w_inter = jnp.exp(inter - mt)
            lanes = slice(h * HEAD_DIM, (h + 1) * HEAD_DIM)
            num = _dot(v_t[lanes, :].astype(BF16), w_t.astype(BF16)) + w_inter * qc_t[lanes, :]
            den = jnp.sum(w_t, axis=0, keepdims=True) + w_inter * qn[h:h + 1, :]
            outs.append(num * (1.0 / jnp.maximum(jnp.abs(den), jnp.exp(-mt))))
        o_ref[rows, :] = jax.nn.sigmoid(og) * jnp.transpose(jnp.concatenate(outs, axis=0))

        g_all = jnp.broadcast_to(b_row[:, lc - 1:lc], b_row.shape)
        wk = g_all + key_row
        m_new = jnp.maximum(g_all + m_prev, jnp.max(wk, axis=1, keepdims=True))
        decay = jnp.exp(g_all + m_prev - m_new)
        w_row = jnp.exp(wk - m_new)
        vw_t = (v_t * _per_head_rows(w_row, HEAD_DIM)).astype(BF16)
        decay_w = _lane_constant_to_width(decay, GROUP_W)
        ct_scr[...] = _per_head_rows(decay_w, HEAD_DIM) * ct_prev + jnp.where(same_head, _dot(vw_t, k), 0.0)
        n_scr[...] = decay_w * n_prev + jnp.where(n_rows, _dot(w_row.astype(BF16), k), 0.0)
        m_scr[...] = m_new
        return carry

    lax.fori_loop(0, seq // lc, chunk, 0, unroll=4)


def _mlstm(z, gates, gate_bias):
    b, s, _ = z.shape
    return pl.pallas_call(
        _mlstm_kernel,
        grid=(b,),
        in_specs=[pl.BlockSpec((None, s, Z_ML), lambda i: (i, 0, 0)),
                  pl.BlockSpec((None, s, Z_G), lambda i: (i, 0, 0)),
                  _const_spec(gate_bias.shape)],
        out_specs=pl.BlockSpec((None, s, GROUP_W), lambda i: (i, 0, 0)),
        out_shape=jax.ShapeDtypeStruct((b, s, GROUP_W), F32),
        scratch_shapes=[pltpu.VMEM((GROUP_W, GROUP_W), F32), pltpu.VMEM((V7X_SUBLANES, GROUP_W), F32),
                        pltpu.VMEM((V7X_SUBLANES, ML_CHUNK), F32)],
        compiler_params=pltpu.CompilerParams(dimension_semantics=("arbitrary",),
                                             vmem_limit_bytes=MIXER_VMEM_LIMIT),
        name="mlstm",
    )(z, gates, gate_bias)


def _block_diag(w):
    out = jnp.zeros((GROUP_W, GROUP_W), w.dtype)
    for h in range(HEADS):
        out = out.at[h * HEAD_DIM:(h + 1) * HEAD_DIM, h * HEAD_DIM:(h + 1) * HEAD_DIM].set(w[h])
    return out


def _mla_up_layout():
    per_q = MLA_NOPE + MLA_ROPE
    q_src, q_dst = [], []
    for h in range(HEADS):
        base = h * GROUP_W
        nope_lo = base + (h % 2) * HEAD_DIM
        q_src += [h * per_q + i for i in range(per_q)]
        q_dst += [nope_lo + i for i in range(MLA_NOPE)]
        q_dst += [base + V7X_LANES + i for i in range(ROPE_HALF)]
        q_dst += [base + V7X_LANES + HEAD_DIM + i for i in range(ROPE_HALF)]
    per_kv = MLA_NOPE + HEAD_DIM
    kn = [h * per_kv + i for h in range(HEADS) for i in range(MLA_NOPE)]
    vv = [h * per_kv + MLA_NOPE + i for h in range(HEADS) for i in range(HEAD_DIM)]
    return np.array(q_src), np.array(q_dst), np.array(kn + vv)


def _rope_tables(seq):
    freqs = ROPE_THETA ** (-jnp.arange(ROPE_HALF, dtype=F32) / ROPE_HALF)
    ang = jnp.arange(seq, dtype=F32)[:, None] * freqs[None, :]
    cos = jnp.tile(jnp.cos(ang), (1, 2 * HEADS))
    sin = jnp.tile(jnp.sin(ang), (1, HEADS))
    return cos, jnp.concatenate([-sin, sin], axis=1)


def kernel(x, ln_g, ln_b, ffn_w1, ffn_w3, ffn_w2, w_in, mla_q_norm, mla_kv_norm, mla_w_uq, mla_w_ukv,
           lru_conv_w, lru_conv_b, lru_w_a, lru_b_a, lru_w_x, lru_b_x, lru_lambda, ml_b_i, ml_b_f,
           out_norm, w_out):
    bsz, seq, d = x.shape
    n = bsz * seq
    assert d == D_MODEL and x.dtype == F32, (x.shape, x.dtype)
    assert seq % MLA_TQ == 0 and seq % ML_CHUNK == 0 and n % TOKEN_TILE == 0, (bsz, seq)
    assert ffn_w1.shape == (DEPTH, 2, D_MODEL, D_FF) and ffn_w2.shape == (DEPTH, 2, D_FF, D_MODEL), ffn_w1.shape
    row = lambda t: t.reshape(1, -1)
    q_src, q_dst, kv_perm = _mla_up_layout()
    cos, sin = _rope_tables(seq)
    swa_bias = _swa_bias_tiles(seq)
    xt = x.reshape(n, d)
    ffn_w = (ffn_w1.astype(BF16), ffn_w3.astype(BF16), ffn_w2.astype(BF16))
    w_in_p = w_in.astype(BF16)
    for l in range(DEPTH):
        x1, za, zlru, sq, sk, sv, zml, zg = _ffn_inproj(
            xt, *ffn_w, (l, 0), row(ln_g[l, 0]), row(ln_b[l, 0]), w_in_p)
        seqv = lambda t: t.reshape(bsz, seq, t.shape[-1])
        y_a = _mla_latent(seqv(za), row(mla_q_norm[l]), row(mla_kv_norm[l]),
                          *_mla_latent_weights(mla_w_uq[l], mla_w_ukv[l]), cos, sin)
        y_b = _rglru(seqv(zlru), lru_conv_w[l], row(lru_conv_b[l]),
                     _block_diag(lru_w_a[l]).astype(BF16), row(lru_b_a[l]),
                     _block_diag(lru_w_x[l]).astype(BF16), row(lru_b_x[l]), row(lru_lambda[l]))
        y_c = _swa(seqv(sq), seqv(sk), seqv(sv), swa_bias)
        gate_bias = jnp.broadcast_to(jnp.concatenate([ml_b_i[l], ml_b_f[l]])[:, None], (2 * HEADS, ML_CHUNK))
        y_d = _mlstm(seqv(zml), seqv(zg), gate_bias)
        ys = [t.reshape(n, GROUP_W) for t in (y_a, y_b, y_c, y_d)]
        xt = _outproj_ffn(x1, ys, row(out_norm[l]), w_out[l].astype(BF16), row(ln_g[l, 1]), row(ln_b[l, 1]),
                          *ffn_w, (l, 1), row(ln_g[l, 2]), row(ln_b[l, 2]))
    return xt.reshape(bsz, seq, d)
```
